```python
import math
import jax, jax.numpy as jnp
from jax import lax
import numpy as np

D_MODEL = 1024
BATCH = 2
SEQ = 8192
DEPTH = 2

GRID_W = 64
CTX_LEN = 256
HEAD_DIM = 64
ROPE_THETA = 10000.0
EPS = 1e-6
Q_BLOCK = 128
NEG = -1e30

A_HEADS = 4
A_DQK = 64
A_DV = 2 * A_DQK
B_HEADS = 8
B_KV_HEADS = 2
B_DH = 64
WINDOW = 128
C_HEADS = 8
C_Q_RANK = 256
C_KV_RANK = 128
C_NOPE = 64
C_ROPE = 32
C_DV = 64
D_HEADS = 4
D_DK = 64
D_DV = 128
D_GATE_RANK = 16
D_GATE_TAU = 16.0
D_CHUNK = 64
P_HEADS = 8
P_NKEYS = 128
P_EXPERTS = P_NKEYS * P_NKEYS
P_DKEY = 256
P_TOPK = 16
P_TOK_BLOCK = 128

EVEN_SIZES = (A_HEADS * 2 * A_DQK, A_HEADS * 2 * A_DQK, A_HEADS * A_DV,
              B_HEADS * B_DH, B_KV_HEADS * B_DH, B_KV_HEADS * B_DH)
ODD_SIZES = (C_Q_RANK, C_KV_RANK, C_ROPE,
             D_HEADS * D_DK, D_HEADS * D_DK, D_HEADS * D_DV, D_GATE_RANK, D_GATE_RANK, D_HEADS * D_DV)
EVEN_IN = sum(EVEN_SIZES)
ODD_IN = sum(ODD_SIZES)

kernel_name = "hybrid_diffusion_diffattn_swa_mla_gla_peer"

F32 = jnp.float32


def rmsnorm(x, g):
    xf = x.astype(F32)
    y = xf * lax.rsqrt(jnp.mean(xf * xf, axis=-1, keepdims=True) + EPS)
    return (y * g.astype(F32)).astype(x.dtype)


def modulate(h, shift, scale):
    return h * (1 + scale) + shift


def modulation(cvec, w_mod, b_mod):
    m = jax.nn.silu(cvec) @ w_mod + b_mod
    return jnp.split(m, 6, axis=-1)


def split_cols(z, sizes):
    out, o = [], 0
    for s in sizes:
        out.append(z[..., o:o + s])
        o += s
    return out


def axial_rope_tables(n_tok, rot_dim):
    rows = n_tok // GRID_W
    r, cc = jnp.meshgrid(jnp.arange(rows, dtype=F32), jnp.arange(GRID_W, dtype=F32), indexing="ij")
    axis_dim = rot_dim // 2
    inv = 1.0 / (ROPE_THETA ** (jnp.arange(0, axis_dim, 2, dtype=F32) / axis_dim))
    ang = jnp.concatenate([r.reshape(-1, 1) * inv, cc.reshape(-1, 1) * inv], axis=-1)
    return jnp.cos(ang), jnp.sin(ang)


def apply_rope(t, cos, sin):
    shp = (1, cos.shape[0]) + (1,) * (t.ndim - 3) + (cos.shape[-1],)
    cs, sn = cos.reshape(shp), sin.reshape(shp)
    t1, t2 = t[..., 0::2].astype(F32), t[..., 1::2].astype(F32)
    out = jnp.stack([t1 * cs - t2 * sn, t1 * sn + t2 * cs], axis=-1).reshape(t.shape)
    return out.astype(t.dtype)


def sweep_query_blocks(fn, q, *ops):
    B, T = q.shape[:2]
    nb = T // Q_BLOCK
    qb = jnp.moveaxis(q.reshape((B, nb, Q_BLOCK) + q.shape[2:]), 1, 0)
    out = lax.map(lambda qq: fn(qq, *ops), qb)
    out = jnp.moveaxis(out, 0, 1)
    return out.reshape((B, T) + out.shape[3:])


def diff_attn(q, k, v, lam):
    s = jnp.einsum("bqhmd,bkhmd->bhmqk", q, k).astype(F32) * (A_DQK ** -0.5)
    p = jax.nn.softmax(s, axis=-1)
    w = p[:, :, 0] - lam * p[:, :, 1]
    return jnp.einsum("bhqk,bkhv->bqhv", w.astype(v.dtype), v)


def softmax_attn(q, k, v, scale):
    s = jnp.einsum("bqhd,bkhd->bhqk", q, k).astype(F32) * scale
    p = jax.nn.softmax(s, axis=-1)
    return jnp.einsum("bhqk,bkhv->bqhv", p.astype(v.dtype), v)


def window_gqa_latent(q, k, v, kc, vc, sink):
    B, S, Hq, d = q.shape
    Hkv = k.shape[2]
    G = Hq // Hkv
    nb = S // Q_BLOCK
    L = kc.shape[1]
    scale = d ** -0.5

    def band(t):
        tp = jnp.pad(t, ((0, 0), (Q_BLOCK, Q_BLOCK), (0, 0), (0, 0))).reshape(B, nb + 2, Q_BLOCK, Hkv, d)
        return jnp.concatenate([tp[:, :-2], tp[:, 1:-1], tp[:, 2:]], axis=2)

    kb = jnp.moveaxis(band(k), 1, 0)
    vb = jnp.moveaxis(band(v), 1, 0)
    qb = jnp.moveaxis(q.reshape(B, nb, Q_BLOCK, Hkv, G, d), 1, 0)
    starts = jnp.arange(nb) * Q_BLOCK
    rel = jnp.arange(3 * Q_BLOCK)[None, :] - Q_BLOCK - jnp.arange(Q_BLOCK)[:, None]
    sink_l = sink.reshape(Hkv, G).astype(F32)

    def one(args):
        qq, kk, vv, st = args
        kpos = st - Q_BLOCK + jnp.arange(3 * Q_BLOCK)
        valid = (jnp.abs(rel) <= WINDOW) & ((kpos >= 0) & (kpos < S))[None, :]
        s_loc = jnp.einsum("bqhgd,bkhd->bhgqk", qq, kk).astype(F32) * scale
        s_loc = jnp.where(valid, s_loc, NEG)
        s_ctx = jnp.einsum("bqhgd,blhd->bhgql", qq, kc).astype(F32) * scale
        s_snk = jnp.broadcast_to(sink_l[None, :, :, None, None], s_ctx.shape[:-1] + (1,))
        p = jax.nn.softmax(jnp.concatenate([s_ctx, s_loc, s_snk], axis=-1), axis=-1)
        o = (jnp.einsum("bhgql,blhd->bqhgd", p[..., :L].astype(vc.dtype), vc)
             + jnp.einsum("bhgqk,bkhd->bqhgd", p[..., L:L + 3 * Q_BLOCK].astype(vv.dtype), vv))
        return o.reshape(B, Q_BLOCK, Hq, d)

    out = lax.map(one, (qb, kb, vb, starts))
    return jnp.moveaxis(out, 0, 1).reshape(B, S, Hq, d)


def ctx_gqa_sink(q, k, v, sink):
    B, L, Hq, d = q.shape
    Hkv = k.shape[2]
    G = Hq // Hkv
    qg = q.reshape(B, L, Hkv, G, d)
    s = jnp.einsum("bqhgd,bkhd->bhgqk", qg, k).astype(F32) * (d ** -0.5)
    snk = jnp.broadcast_to(sink.reshape(Hkv, G).astype(F32)[None, :, :, None, None], s.shape[:-1] + (1,))
    p = jax.nn.softmax(jnp.concatenate([s, snk], axis=-1), axis=-1)[..., :-1]
    return jnp.einsum("bhgqk,bkhd->bqhgd", p.astype(v.dtype), v).reshape(B, L, Hq, d)


def gla_chunked(q, k, v, g, s0):
    B, T, H, dk = q.shape
    dv = v.shape[-1]
    C = D_CHUNK
    n = T // C
    rs = lambda t: t.astype(F32).reshape(B, n, C, H, t.shape[-1])
    qf, kf, vf, gf = rs(q), rs(k), rs(v), rs(g)
    b = jnp.cumsum(gf, axis=2)
    b_last = b[:, :, -1:]
    q_t = qf * jnp.exp(b)
    k_in = kf * jnp.exp(-b)
    k_st = kf * jnp.exp(b_last - b)
    mask = jnp.tril(jnp.ones((C, C), dtype=bool))
    a = jnp.where(mask, jnp.einsum("bnchk,bnshk->bnhcs", q_t, k_in), 0.0)
    o_intra = jnp.einsum("bnhcs,bnshv->bnchv", a, vf)
    ds = jnp.einsum("bnchk,bnchv->nbhkv", k_st, vf)
    decay = jnp.moveaxis(jnp.exp(b_last[:, :, 0]), 1, 0)

    def step(s, inp):
        dcy, dsn = inp
        return dcy[..., None] * s + dsn, s

    s_fin, s_start = lax.scan(step, s0.astype(F32), (decay, ds))
    o_inter = jnp.einsum("bnchk,nbhkv->bnchv", q_t, s_start)
    o = (o_intra + o_inter).reshape(B, T, H, dv)
    return o.astype(v.dtype), s_fin


def mix_even(hx, hc, w_in, w_out, a_qn, a_kn, a_lq1, a_lk1, a_lq2, a_lk2, a_gn, b_qn, b_kn, b_sink,
             lam_init, rope, need_ctx):
    cos, sin = rope

    def proj(h):
        B, T = h.shape[:2]
        aq, ak, av, bq, bk, bv = split_cols(h @ w_in, EVEN_SIZES)
        aq = rmsnorm(aq.reshape(B, T, A_HEADS, 2, A_DQK), a_qn)
        ak = rmsnorm(ak.reshape(B, T, A_HEADS, 2, A_DQK), a_kn)
        av = av.reshape(B, T, A_HEADS, A_DV)
        bq = rmsnorm(bq.reshape(B, T, B_HEADS, B_DH), b_qn)
        bk = rmsnorm(bk.reshape(B, T, B_KV_HEADS, B_DH), b_kn)
        bv = bv.reshape(B, T, B_KV_HEADS, B_DH)
        return aq, ak, av, bq, bk, bv

    xaq, xak, xav, xbq, xbk, xbv = proj(hx)
    caq, cak, cav, cbq, cbk, cbv = proj(hc)
    xaq, xak = apply_rope(xaq, cos, sin), apply_rope(xak, cos, sin)
    xbq, xbk = apply_rope(xbq, cos, sin), apply_rope(xbk, cos, sin)
    lam = (jnp.exp(jnp.sum(a_lq1.astype(F32) * a_lk1.astype(F32)))
           - jnp.exp(jnp.sum(a_lq2.astype(F32) * a_lk2.astype(F32))) + lam_init)

    ka = jnp.concatenate([cak, xak], axis=1)
    va = jnp.concatenate([cav, xav], axis=1)
    ox_a = sweep_query_blocks(diff_attn, xaq, ka, va, lam)
    ox_b = window_gqa_latent(xbq, xbk, xbv, cbk, cbv, b_sink)

    def merge(oa, ob):
        B, T = oa.shape[:2]
        oa = rmsnorm(oa, a_gn) * (1.0 - lam_init)
        return jnp.concatenate([oa.reshape(B, T, -1), ob.reshape(B, T, -1).astype(oa.dtype)], axis=-1) @ w_out

    out_x = merge(ox_a, ox_b)
    out_c = merge(diff_attn(caq, cak, cav, lam), ctx_gqa_sink(cbq, cbk, cbv, b_sink)) if need_ctx else None
    return out_x, out_c


def mix_odd(hx, hc, w_in, w_out, c_gq, c_gkv, c_wq_up, c_wkv_up, c_qn, c_kn,
            d_wg_f, d_bg_f, d_wg_b, d_bg_b, d_gn, rope, need_ctx):
    cos, sin = rope

    def proj(h):
        B, T = h.shape[:2]
        cq, ckv, kpe, dq, dk, dv, gf, gb, r = split_cols(h @ w_in, ODD_SIZES)
        q = (rmsnorm(cq, c_gq) @ c_wq_up).reshape(B, T, C_HEADS, C_NOPE + C_ROPE)
        kv = (rmsnorm(ckv, c_gkv) @ c_wkv_up).reshape(B, T, C_HEADS, C_NOPE + C_DV)
        k = jnp.concatenate([kv[..., :C_NOPE],
                             jnp.broadcast_to(kpe[:, :, None, :], (B, T, C_HEADS, C_ROPE))], axis=-1)
        q, k = rmsnorm(q, c_qn), rmsnorm(k, c_kn)
        v = kv[..., C_NOPE:]
        dq = dq.reshape(B, T, D_HEADS, D_DK) * (D_DK ** -0.5)
        dk = dk.reshape(B, T, D_HEADS, D_DK)
        dv = dv.reshape(B, T, D_HEADS, D_DV)
        lg_f = (jax.nn.log_sigmoid((gf @ d_wg_f + d_bg_f).astype(F32)) / D_GATE_TAU).reshape(B, T, D_HEADS, D_DK)
        lg_b = (jax.nn.log_sigmoid((gb @ d_wg_b + d_bg_b).astype(F32)) / D_GATE_TAU).reshape(B, T, D_HEADS, D_DK)
        r = r.reshape(B, T, D_HEADS, D_DV)
        return q, k, v, dq, dk, dv, lg_f, lg_b, r

    xq, xk, xv, xdq, xdk, xdv, xlf, xlb, xr = proj(hx)
    cq, ck, cv, cdq, cdk, cdv, clf, clb, cr = proj(hc)

    def rope_tail(t):
        return jnp.concatenate([t[..., :C_NOPE], apply_rope(t[..., C_NOPE:], cos, sin)], axis=-1)

    xq, xk = rope_tail(xq), rope_tail(xk)
    scale = (C_NOPE + C_ROPE) ** -0.5
    kc_all = jnp.concatenate([ck, xk], axis=1)
    vc_all = jnp.concatenate([cv, xv], axis=1)
    ox_c = sweep_query_blocks(softmax_attn, xq, kc_all, vc_all, scale)

    B = hx.shape[0]
    zeros = jnp.zeros((B, D_HEADS, D_DK, D_DV), F32)
    flip = lambda t: jnp.flip(t, axis=1)
    oc_f, sc_f = gla_chunked(cdq, cdk, cdv, clf, zeros)
    ox_f, _ = gla_chunked(xdq, xdk, xdv, xlf, sc_f)
    oc_br, sc_b = gla_chunked(flip(cdq), flip(cdk), flip(cdv), flip(clb), zeros)
    ox_br, _ = gla_chunked(flip(xdq), flip(xdk), flip(xdv), flip(xlb), sc_b)

    def merge(oc_mla, o_gla, r):
        Bn, T = oc_mla.shape[:2]
        g = (rmsnorm(o_gla, d_gn) * jax.nn.silu(r)).astype(oc_mla.dtype)
        return jnp.concatenate([oc_mla.reshape(Bn, T, -1), g.reshape(Bn, T, -1)], axis=-1) @ w_out

    out_x = merge(ox_c, ox_f + flip(ox_br), xr)
    out_c = merge(softmax_attn(cq, ck, cv, scale), oc_f + flip(oc_br), cr) if need_ctx else None
    return out_x, out_c


def peer(h, wq, sub_keys, u_tab, v_tab):
    T, D = h.shape
    q = (h @ wq).reshape(T, P_HEADS, 2, P_DKEY // 2)
    s = jnp.einsum("thpd,hpnd->thpn", q, sub_keys).astype(F32)
    sv, si = lax.top_k(s, P_TOPK)
    cand = (sv[:, :, 0, :, None] + sv[:, :, 1, None, :]).reshape(T, P_HEADS, P_TOPK * P_TOPK)
    cidx = (si[:, :, 0, :, None] * P_NKEYS + si[:, :, 1, None, :]).reshape(T, P_HEADS, P_TOPK * P_TOPK)
    fv, fi = lax.top_k(cand, P_TOPK)
    eidx = jnp.take_along_axis(cidx, fi, axis=-1).reshape(T, P_HEADS * P_TOPK)
    gate = jax.nn.softmax(fv, axis=-1).reshape(T, P_HEADS * P_TOPK)
    nb = T // P_TOK_BLOCK

    def blk(args):
        hb, ib, gb = args
        a = jax.nn.gelu(jnp.einsum("tkd,td->tk", u_tab[ib], hb).astype(F32), approximate=False) * gb
        return jnp.einsum("tk,tkd->td", a.astype(hb.dtype), v_tab[ib])

    out = lax.map(blk, (h.reshape(nb, P_TOK_BLOCK, D), eidx.reshape(nb, P_TOK_BLOCK, -1),
                        gate.reshape(nb, P_TOK_BLOCK, -1)))
    return out.reshape(T, D)


def setup_inputs(seed: int = 0) -> dict:
    key = jax.random.key(seed)
    ks = iter(jax.random.split(key, 64))
    nrm = lambda shape, scale: jax.random.normal(next(ks), shape, F32) * scale
    gain = lambda shape: 1.0 + nrm(shape, 0.02)
    D = D_MODEL
    NE = (DEPTH + 1) // 2
    NO = DEPTH // 2
    return {
        "x": nrm((BATCH, SEQ, D), 1.0),
        "c": nrm((BATCH, D), 1.0),
        "ctx": nrm((BATCH, CTX_LEN, D), 1.0),
        "c_ctx": nrm((D,), 1.0),
        "w_mod": nrm((DEPTH, D, 6 * D), 0.5 * D ** -0.5),
        "b_mod": nrm((DEPTH, 6 * D), 0.02),
        "norm1": gain((DEPTH, D)),
        "norm2": gain((DEPTH, D)),
        "e_w_in": nrm((NE, D, EVEN_IN), D ** -0.5),
        "e_w_out": nrm((NE, D, D), D ** -0.5),
        "a_qn": gain((NE, A_DQK)),
        "a_kn": gain((NE, A_DQK)),
        "a_lq1": nrm((NE, A_DQK), 0.1),
        "a_lk1": nrm((NE, A_DQK), 0.1),
        "a_lq2": nrm((NE, A_DQK), 0.1),
        "a_lk2": nrm((NE, A_DQK), 0.1),
        "a_gn": gain((NE, A_DV)),
        "b_qn": gain((NE, B_DH)),
        "b_kn": gain((NE, B_DH)),
        "b_sink": nrm((NE, B_HEADS), 0.5),
        "o_w_in": nrm((NO, D, ODD_IN), D ** -0.5),
        "o_w_out": nrm((NO, D, D), D ** -0.5),
        "c_gq": gain((NO, C_Q_RANK)),
        "c_gkv": gain((NO, C_KV_RANK)),
        "c_wq_up": nrm((NO, C_Q_RANK, C_HEADS * (C_NOPE + C_ROPE)), C_Q_RANK ** -0.5),
        "c_wkv_up": nrm((NO, C_KV_RANK, C_HEADS * (C_NOPE + C_DV)), C_KV_RANK ** -0.5),
        "c_qn": gain((NO, C_NOPE + C_ROPE)),
        "c_kn": gain((NO, C_NOPE + C_ROPE)),
        "d_wg_f": nrm((NO, D_GATE_RANK, D_HEADS * D_DK), D_GATE_RANK ** -0.5),
        "d_bg_f": nrm((NO, D_HEADS * D_DK), 0.1),
        "d_wg_b": nrm((NO, D_GATE_RANK, D_HEADS * D_DK), D_GATE_RANK ** -0.5),
        "d_bg_b": nrm((NO, D_HEADS * D_DK), 0.1),
        "d_gn": gain((NO, D_DV)),
        "p_wq": nrm((DEPTH, D, P_HEADS * P_DKEY), D ** -0.5),
        "p_keys": nrm((DEPTH, P_HEADS, 2, P_NKEYS, P_DKEY // 2), (P_DKEY // 2) ** -0.5),
        "p_u": nrm((DEPTH, P_EXPERTS, D), D ** -0.5),
        "p_v": nrm((DEPTH, P_EXPERTS, D), 1.0),
    }


def reference(x, c, ctx, c_ctx, w_mod, b_mod, norm1, norm2, e_w_in, e_w_out, a_qn, a_kn, a_lq1, a_lk1,
              a_lq2, a_lk2, a_gn, b_qn, b_kn, b_sink, o_w_in, o_w_out, c_gq, c_gkv, c_wq_up, c_wkv_up,
              c_qn, c_kn, d_wg_f, d_bg_f, d_wg_b, d_bg_b, d_gn, p_wq, p_keys, p_u, p_v):
    B, S, D = x.shape
    L = ctx.shape[1]
    rope_ab = axial_rope_tables(S, HEAD_DIM)
    rope_c = axial_rope_tables(S, C_ROPE)
    h = ctx
    for i in range(DEPTH):
        last = i == DEPTH - 1
        j = i // 2
        mx = [m[:, None, :] for m in modulation(c, w_mod[i], b_mod[i])]
        mc = modulation(c_ctx, w_mod[i], b_mod[i])
        hx = modulate(rmsnorm(x, norm1[i]), mx[0], mx[1])
        hc = modulate(rmsnorm(h, norm1[i]), mc[0], mc[1])
        if i % 2 == 0:
            lam_init = 0.8 - 0.6 * math.exp(-0.3 * i)
            ox, oc = mix_even(hx, hc, e_w_in[j], e_w_out[j], a_qn[j], a_kn[j], a_lq1[j], a_lk1[j],
                              a_lq2[j], a_lk2[j], a_gn[j], b_qn[j], b_kn[j], b_sink[j],
                              lam_init, rope_ab, not last)
        else:
            ox, oc = mix_odd(hx, hc, o_w_in[j], o_w_out[j], c_gq[j], c_gkv[j], c_wq_up[j], c_wkv_up[j],
                             c_qn[j], c_kn[j], d_wg_f[j], d_bg_f[j], d_wg_b[j], d_bg_b[j], d_gn[j],
                             rope_c, not last)
        x = x + mx[2] * ox
        hx = modulate(rmsnorm(x, norm2[i]), mx[3], mx[4])
        x = x + mx[5] * peer(hx.reshape(B * S, D), p_wq[i], p_keys[i], p_u[i], p_v[i]).reshape(B, S, D)
        if not last:
            h = h + mc[2] * oc
            hc = modulate(rmsnorm(h, norm2[i]), mc[3], mc[4])
            h = h + mc[5] * peer(hc.reshape(B * L, D), p_wq[i], p_keys[i], p_u[i], p_v[i]).reshape(B, L, D)
    return x
```

```python
import functools
import math

import numpy as np
import jax
import jax.numpy as jnp
from jax import lax
from jax.experimental import pallas as pl
from jax.experimental.pallas import tpu as pltpu

F32 = jnp.float32
MXU_DTYPE = jnp.bfloat16

GRID_W = 64
ROPE_THETA = 10000.0
EPS = 1e-6
NEG = -1e30
A_HEADS, A_DQK, A_DV = 4, 64, 128
B_HEADS, B_KV_HEADS, B_DH, WINDOW = 8, 2, 64, 128
C_HEADS, C_Q_RANK, C_KV_RANK, C_NOPE, C_ROPE, C_DV = 8, 256, 128, 64, 32, 64
D_HEADS, D_DK, D_DV, D_GATE_RANK, D_GATE_TAU, D_CHUNK = 4, 64, 128, 16, 16.0, 64
P_HEADS, P_NKEYS, P_DKEY, P_TOPK = 8, 128, 256, 16

LANES = 128
ROW_TILE = 256
VMEM_LIMIT = 56 * 1024 * 1024


def _cparams(*sem):
    return pltpu.CompilerParams(dimension_semantics=sem, vmem_limit_bytes=VMEM_LIMIT)


def _rms(x, g):
    return x * lax.rsqrt(jnp.mean(x * x, axis=-1, keepdims=True) + EPS) * g


def _nt(a, b):
    return lax.dot_general(a, b, (((1,), (1,)), ((), ())), preferred_element_type=F32)


def _tn(a, b):
    return lax.dot_general(a, b, (((0,), (0,)), ((), ())), preferred_element_type=F32)


def _mm(a, b):
    return jnp.dot(a, b, preferred_element_type=F32)


def _mod_kernel(c_ref, w_ref, b_ref, o_ref):
    cv = c_ref[...]
    s = cv * jax.nn.sigmoid(cv)
    o_ref[0] = jnp.dot(s, w_ref[0], precision=lax.Precision.HIGHEST, preferred_element_type=F32) + b_ref[0]


def _modulation(cvecs, w_mod, b_mod):
    depth, d, n = w_mod.shape
    rows = cvecs.shape[0]
    tn = 1536
    return pl.pallas_call(
        _mod_kernel,
        grid=(depth, n // tn),
        in_specs=[pl.BlockSpec((rows, d), lambda l, j: (0, 0)),
                  pl.BlockSpec((1, d, tn), lambda l, j: (l, 0, j)),
                  pl.BlockSpec((1, 1, tn), lambda l, j: (l, 0, j))],
        out_specs=pl.BlockSpec((1, rows, tn), lambda l, j: (l, 0, j)),
        out_shape=jax.ShapeDtypeStruct((depth, rows, n), F32),
        compiler_params=_cparams("parallel", "parallel"),
        name="modulation",
    )(cvecs, w_mod, b_mod.reshape(depth, 1, n))


def _stream_prologue(has_prev, it):
    xs_ref = next(it)
    x = xs_ref[0]
    if has_prev:
        pp_ref, pg_ref = next(it), next(it)
        x = x + pg_ref[...] * pp_ref[0]
    g1_ref, sh_ref, sc_ref, w_ref = next(it), next(it), next(it), next(it)
    h = _rms(x, g1_ref[...]) * (1.0 + sc_ref[...]) + sh_ref[...]
    z = _mm(h.astype(MXU_DTYPE), w_ref[...])
    return x, z


def _norm_rope_pair64(zg, gain, cosf, sinf):
    lane = lax.broadcasted_iota(jnp.int32, zg.shape, 1)
    lo = lane < 64
    sq = zg * zg
    s_lo = jnp.sum(jnp.where(lo, sq, 0.0), axis=-1, keepdims=True)
    s_hi = jnp.sum(jnp.where(lo, 0.0, sq), axis=-1, keepdims=True)
    ms = jnp.where(lo, s_lo, s_hi) * (1.0 / 64.0)
    y = zg * lax.rsqrt(ms + EPS) * gain
    partner = jnp.where((lane & 63) < 32, pltpu.roll(y, 96, 1), pltpu.roll(y, 32, 1))
    return y * cosf + partner * sinf


def _inproj_even_kernel(has_prev, *refs):
    it = iter(refs)
    x, z = _stream_prologue(has_prev, it)
    gain_ref, cos_ref, sin_ref = next(it), next(it), next(it)
    if has_prev:
        xo_ref = next(it)
        xo_ref[0] = x
    aq_ref, ak_ref, bq_ref, bk_ref, av_ref, bv_ref = (next(it) for _ in range(6))
    cosf, sinf = cos_ref[...], sin_ref[...]
    g = 0
    for ref, ngroups in ((aq_ref, 4), (ak_ref, 4), (bq_ref, 4), (bk_ref, 1)):
        for k in range(ngroups):
            sl = slice(g * LANES, (g + 1) * LANES)
            y = _norm_rope_pair64(z[:, sl], gain_ref[:, sl], cosf, sinf)
            ref[0, :, k * LANES:(k + 1) * LANES] = y.astype(ref.dtype)
            g += 1
    o = g * LANES
    av_ref[0] = z[:, o:o + 512].astype(av_ref.dtype)
    bv_ref[0] = z[:, o + 512:o + 640].astype(bv_ref.dtype)


def _norm_rope_mla(y, gain, cosf, sinf):
    lane = lax.broadcasted_iota(jnp.int32, y.shape, 1)
    ms = jnp.sum(y * y, axis=-1, keepdims=True) * (1.0 / (C_NOPE + C_ROPE))
    y = y * lax.rsqrt(ms + EPS) * gain
    partner = jnp.where(lane < C_NOPE + C_ROPE // 2, pltpu.roll(y, LANES - C_ROPE // 2, 1),
                        pltpu.roll(y, C_ROPE // 2, 1))
    return y * cosf + partner * sinf


def _log_sigmoid(x):
    return jnp.minimum(x, 0.0) - jnp.log1p(jnp.exp(-jnp.abs(x)))


def _inproj_odd_kernel(has_prev, *refs):
    it = iter(refs)
    x, z = _stream_prologue(has_prev, it)
    (gq_ref, gkv_ref, wq_ref, wkv_ref, qn_ref, kn_ref, wg_ref, bg_ref, cos_ref, sin_ref) = (next(it) for _ in range(10))
    if has_prev:
        xo_ref = next(it)
        xo_ref[0] = x
    cq_ref, ck_ref, cv_ref, dq_ref, dk_ref, dv_ref, lg_ref, r_ref = (next(it) for _ in range(8))
    cosf, sinf = cos_ref[...], sin_ref[...]
    qn = _rms(z[:, 0:256], gq_ref[...])
    q_up = _mm(qn.astype(MXU_DTYPE), wq_ref[...])
    kvn = _rms(z[:, 256:384], gkv_ref[...])
    kv_up = _mm(kvn.astype(MXU_DTYPE), wkv_ref[...])
    kpe = pltpu.roll(z[:, 384:512], C_NOPE, 1)
    for h in range(C_HEADS):
        sl = slice(h * LANES, (h + 1) * LANES)
        cq_ref[0, :, sl] = _norm_rope_mla(q_up[:, sl], qn_ref[:, sl], cosf, sinf).astype(cq_ref.dtype)
        ck_ref[0, :, sl] = _norm_rope_mla(kv_up[:, sl] + kpe, kn_ref[:, sl], cosf, sinf).astype(ck_ref.dtype)
    cv_ref[0] = kv_up[:, C_HEADS * LANES:].astype(cv_ref.dtype)
    dq_ref[0] = z[:, 512:768] * (D_DK ** -0.5)
    dk_ref[0] = z[:, 768:1024]
    dv_ref[0] = z[:, 1024:1536].astype(dv_ref.dtype)
    gl = _mm(z[:, 1536:1664].astype(MXU_DTYPE), wg_ref[...]) + bg_ref[...]
    lg_ref[0] = _log_sigmoid(gl) * (1.0 / D_GATE_TAU)
    r_ref[0] = z[:, 1664:2176]


def _inproj(kind, xs, prev, norm_g, shift, scale, w, extra, out_cols, LT):
    B, T, D = xs.shape
    tm = ROW_TILE
    nT = T // tm
    has_prev = prev is not None
    row = lambda b, i: (b, i, 0)
    cls = lambda b, i: (b, jnp.where(i >= LT, 1, 0), 0, 0)
    full2 = lambda b, i: (0, 0)
    args, specs = [xs], [pl.BlockSpec((1, tm, D), row)]
    if has_prev:
        args += [prev[0], prev[1]]
        specs += [pl.BlockSpec((1, tm, D), row), pl.BlockSpec((None, None, 1, D), cls)]
    args += [norm_g, shift, scale, w]
    specs += [pl.BlockSpec((1, D), full2), pl.BlockSpec((None, None, 1, D), cls),
              pl.BlockSpec((None, None, 1, D), cls), pl.BlockSpec(w.shape, full2)]
    for arr, k in extra:
        args.append(arr)
        if k == "full":
            specs.append(pl.BlockSpec(arr.shape, full2))
        else:
            specs.append(pl.BlockSpec((tm, arr.shape[1]), lambda b, i: (i, 0)))
    out_shape, out_specs = [], []
    if has_prev:
        out_shape.append(jax.ShapeDtypeStruct((B, T, D), F32))
        out_specs.append(pl.BlockSpec((1, tm, D), row))
    for n, dt in out_cols:
        out_shape.append(jax.ShapeDtypeStruct((B, T, n), dt))
        out_specs.append(pl.BlockSpec((1, tm, n), row))
    body = _inproj_even_kernel if kind == "even" else _inproj_odd_kernel
    outs = pl.pallas_call(
        functools.partial(body, has_prev),
        grid=(B, nT), in_specs=specs, out_specs=out_specs, out_shape=out_shape,
        compiler_params=_cparams("parallel", "parallel"),
        name="inproj_" + kind,
    )(*args)
    if has_prev:
        return outs[0], outs[1:]
    return xs, outs


def _online_softmax_step(s, v, m_ref, l_ref, acc_ref):
    m_prev = m_ref[...]
    m_new = jnp.maximum(m_prev, jnp.max(s, axis=-1, keepdims=True))
    alpha = jnp.exp(m_prev - m_new)
    p = jnp.exp(s - m_new)
    l_ref[...] = alpha * l_ref[...] + jnp.sum(p, axis=-1, keepdims=True)
    acc_ref[...] = alpha * acc_ref[...] + _mm(p.astype(MXU_DTYPE), v)
    m_ref[...] = m_new


def _flash_guard(L, tq, step, finalize):
    i, j, nk = pl.program_id(2), pl.program_id(3), pl.num_programs(3)
    is_ctx = i < L // tq

    @pl.when(jnp.logical_not(is_ctx))
    def _():
        step(False)

    @pl.when(jnp.logical_and(is_ctx, j == 0))
    def _():
        step(True)

    @pl.when(j == nk - 1)
    def _():
        finalize()


def _diff_attn_kernel(L, lam_init, q_ref, k_ref, v_ref, lqk_ref, gn_ref, o_ref, qs_ref, m_ref, l_ref, acc_ref):
    tq = q_ref.shape[1]

    @pl.when(pl.program_id(3) == 0)
    def _():
        q = q_ref[0]
        lane = lax.broadcasted_iota(jnp.int32, q.shape, 1)
        zero = jnp.zeros_like(q)
        qs_ref[:tq] = jnp.where(lane < A_DQK, q, zero)
        qs_ref[tq:] = jnp.where(lane < A_DQK, zero, q)
        m_ref[...] = jnp.full(m_ref.shape, -jnp.inf, F32)
        l_ref[...] = jnp.zeros(l_ref.shape, F32)
        acc_ref[...] = jnp.zeros(acc_ref.shape, F32)

    def step(masked):
        s = _nt(qs_ref[...], k_ref[0])
        if masked:
            col = lax.broadcasted_iota(jnp.int32, s.shape, 1)
            s = jnp.where(col < L, s, NEG)
        _online_softmax_step(s, v_ref[0], m_ref, l_ref, acc_ref)

    def finalize():
        lqk = lqk_ref[...]
        lam = (jnp.exp(jnp.sum(lqk[0:1] * lqk[1:2], axis=-1, keepdims=True))
               - jnp.exp(jnp.sum(lqk[2:3] * lqk[3:4], axis=-1, keepdims=True)) + lam_init)
        o = acc_ref[:tq] / l_ref[:tq] - lam * (acc_ref[tq:] / l_ref[tq:])
        o_ref[0] = (_rms(o, gn_ref[...]) * (1.0 - lam_init)).astype(o_ref.dtype)

    _flash_guard(L, tq, step, finalize)


def _key_tile(T):
    for tk in (768, 512, 256):
        if T % tk == 0:
            return tk
    raise ValueError("stream length must be a multiple of 256")


def _diff_attn(q, k, v, lqk, gn, L, lam_init):
    B, T, _ = q.shape
    tq, tk = ROW_TILE, _key_tile(T)
    assert L % tq == 0 and L <= tk
    return pl.pallas_call(
        functools.partial(_diff_attn_kernel, L, lam_init),
        grid=(B, A_HEADS, T // tq, T // tk),
        in_specs=[pl.BlockSpec((1, tq, LANES), lambda b, h, i, j: (b, i, h)),
                  pl.BlockSpec((1, tk, LANES), lambda b, h, i, j: (b, j, h)),
                  pl.BlockSpec((1, tk, LANES), lambda b, h, i, j: (b, j, h)),
                  pl.BlockSpec(lqk.shape, lambda b, h, i, j: (0, 0)),
                  pl.BlockSpec(gn.shape, lambda b, h, i, j: (0, 0))],
        out_specs=pl.BlockSpec((1, tq, LANES), lambda b, h, i, j: (b, i, h)),
        out_shape=jax.ShapeDtypeStruct((B, T, A_HEADS * A_DV), MXU_DTYPE),
        scratch_shapes=[pltpu.VMEM((2 * tq, LANES), MXU_DTYPE), pltpu.VMEM((2 * tq, 1), F32),
                        pltpu.VMEM((2 * tq, 1), F32), pltpu.VMEM((2 * tq, LANES), F32)],
        compiler_params=_cparams("parallel", "parallel", "parallel", "arbitrary"),
        name="diff_attn",
    )(q, k, v, lqk, gn)


def _mla_attn_kernel(L, q_ref, k_ref, v_ref, o_ref, m_ref, l_ref, acc_ref):
    tq = q_ref.shape[1]

    @pl.when(pl.program_id(3) == 0)
    def _():
        m_ref[...] = jnp.full(m_ref.shape, -jnp.inf, F32)
        l_ref[...] = jnp.zeros(l_ref.shape, F32)
        acc_ref[...] = jnp.zeros(acc_ref.shape, F32)

    def step(masked):
        s = jnp.concatenate([_nt(q_ref[0, :, :LANES], k_ref[0, :, :LANES]),
                             _nt(q_ref[0, :, LANES:], k_ref[0, :, LANES:])], axis=0)
        if masked:
            col = lax.broadcasted_iota(jnp.int32, s.shape, 1)
            s = jnp.where(col < L, s, NEG)
        _online_softmax_step(s, v_ref[0], m_ref, l_ref, acc_ref)

    def finalize():
        lane = lax.broadcasted_iota(jnp.int32, (tq, LANES), 1)
        o = jnp.where(lane < C_DV, acc_ref[:tq] / l_ref[:tq], acc_ref[tq:] / l_ref[tq:])
        o_ref[0] = o.astype(o_ref.dtype)

    _flash_guard(L, tq, step, finalize)


def _mla_attn(q, k, v, L):
    B, T, _ = q.shape
    tq, tk = ROW_TILE, _key_tile(T)
    assert L % tq == 0 and L <= tk
    return pl.pallas_call(
        functools.partial(_mla_attn_kernel, L),
        grid=(B, C_HEADS // 2, T // tq, T // tk),
        in_specs=[pl.BlockSpec((1, tq, 2 * LANES), lambda b, h, i, j: (b, i, h)),
                  pl.BlockSpec((1, tk, 2 * LANES), lambda b, h, i, j: (b, j, h)),
                  pl.BlockSpec((1, tk, LANES), lambda b, h, i, j: (b, j, h))],
        out_specs=pl.BlockSpec((1, tq, LANES), lambda b, h, i, j: (b, i, h)),
        out_shape=jax.ShapeDtypeStruct((B, T, C_HEADS * C_DV), MXU_DTYPE),
        scratch_shapes=[pltpu.VMEM((2 * tq, 1), F32), pltpu.VMEM((2 * tq, 1), F32),
                        pltpu.VMEM((2 * tq, LANES), F32)],
        compiler_params=_cparams("parallel", "parallel", "parallel", "arbitrary"),
        name="mla_attn",
    )(q, k, v)


def _win_attn_kernel(LT, sink_ref, q_ref, kc_ref, kp_ref, kk_ref, kn_ref, vc_ref, vp_ref, vk_ref, vn_ref, o_ref):
    i, nT = pl.program_id(1), pl.num_programs(1)
    tq = q_ref.shape[1]
    tk = kc_ref.shape[1]
    r = lax.broadcasted_iota(jnp.int32, (2 * tq, tk), 0)
    c = lax.broadcasted_iota(jnp.int32, (2 * tq, tk), 1)
    d = c - jnp.where(r >= tq, r - tq, r)
    is_x = i >= LT
    far = 4 * tk
    ok_p = d >= jnp.where(jnp.logical_and(is_x, i - 1 >= LT), tk - WINDOW, far)
    ok_k = jnp.abs(d) <= jnp.where(is_x, WINDOW, -1)
    ok_n = d <= jnp.where(jnp.logical_and(is_x, i + 1 <= nT - 1), WINDOW - tk, -far)
    k_tiles = (kc_ref[0], kp_ref[0], kk_ref[0], kn_ref[0])
    masks = (None, ok_p, ok_k, ok_n)
    vals = jnp.concatenate([vc_ref[0], vp_ref[0], vk_ref[0], vn_ref[0]], axis=0)
    lane = lax.broadcasted_iota(jnp.int32, (tq, LANES), 1)
    lo = lane < B_DH
    rows = lax.broadcasted_iota(jnp.int32, (2 * tq, 1), 0)
    group = B_HEADS // B_KV_HEADS
    for g in range(B_HEADS // 2):
        kv = (2 * g) // group
        qg = q_ref[0, :, g * LANES:(g + 1) * LANES].astype(F32)
        qr = pltpu.roll(qg, B_DH, 1)
        if kv == 0:
            qa, qb = jnp.where(lo, qg, 0.0), jnp.where(lo, qr, 0.0)
        else:
            qa, qb = jnp.where(lo, 0.0, qr), jnp.where(lo, 0.0, qg)
        qs = jnp.concatenate([qa, qb], axis=0).astype(MXU_DTYPE)
        s = jnp.concatenate([_nt(qs, kt) if ok is None else jnp.where(ok, _nt(qs, kt), NEG)
                             for kt, ok in zip(k_tiles, masks)], axis=1)
        sk = jnp.where(rows < tq, sink_ref[2 * g], sink_ref[2 * g + 1])
        m = jnp.maximum(jnp.max(s, axis=-1, keepdims=True), sk)
        p = jnp.exp(s - m)
        l = jnp.sum(p, axis=-1, keepdims=True) + jnp.exp(sk - m)
        acc = _mm(p.astype(MXU_DTYPE), vals) / l
        if kv == 0:
            out = jnp.where(lo, acc[:tq], pltpu.roll(acc[tq:], B_DH, 1))
        else:
            out = jnp.where(lo, pltpu.roll(acc[:tq], B_DH, 1), acc[tq:])
        o_ref[0, :, g * LANES:(g + 1) * LANES] = out.astype(o_ref.dtype)


def _win_attn(q, k, v, sink, L):
    B, T, _ = q.shape
    t = ROW_TILE
    assert L == t and WINDOW <= t
    nT, LT = T // t, L // t
    kspec = lambda f: pl.BlockSpec((1, t, LANES), f)
    maps = [lambda b, i: (b, 0, 0),
            lambda b, i: (b, jnp.maximum(i - 1, LT), 0),
            lambda b, i: (b, i, 0),
            lambda b, i: (b, jnp.minimum(i + 1, nT - 1), 0)]
    return pl.pallas_call(
        functools.partial(_win_attn_kernel, LT),
        grid=(B, nT),
        in_specs=[pl.BlockSpec(memory_space=pltpu.SMEM),
                  pl.BlockSpec((1, t, B_HEADS * B_DH), lambda b, i: (b, i, 0))]
                 + [kspec(f) for f in maps] + [kspec(f) for f in maps],
        out_specs=pl.BlockSpec((1, t, B_HEADS * B_DH), lambda b, i: (b, i, 0)),
        out_shape=jax.ShapeDtypeStruct((B, T, B_HEADS * B_DH), MXU_DTYPE),
        compiler_params=_cparams("parallel", "parallel"),
        name="window_attn",
    )(sink, q, k, k, k, k, v, v, v, v)


def _gla_kernel(reverse, q_ref, k_ref, v_ref, g_ref, o_ref, st_ref):
    @pl.when(pl.program_id(1) == 0)
    def _():
        st_ref[...] = jnp.zeros(st_ref.shape, F32)

    C = D_CHUNK
    tb = q_ref.shape[1]
    r = lax.broadcasted_iota(jnp.int32, (C, C), 0)
    c = lax.broadcasted_iota(jnp.int32, (C, C), 1)
    keep = (c >= r) if reverse else (c <= r)
    tri = jnp.where(keep, 1.0, 0.0).astype(F32)
    chunks = range(tb // C)
    for ch in (reversed(chunks) if reverse else chunks):
        rows = slice(ch * C, (ch + 1) * C)
        b = jnp.dot(tri, g_ref[0, rows, :], precision=lax.Precision.HIGHEST, preferred_element_type=F32)
        b_last = b[0:1] if reverse else b[C - 1:C]
        q_t = q_ref[0, rows, :] * jnp.exp(b)
        kk = k_ref[0, rows, :]
        k_in = kk * jnp.exp(-b)
        k_st = kk * jnp.exp(b_last - b)
        decay = jnp.exp(b_last)
        for h in range(D_HEADS):
            sl = slice(h * D_DK, (h + 1) * D_DK)
            vsl = slice(h * D_DV, (h + 1) * D_DV)
            qh = q_t[:, sl].astype(MXU_DTYPE)
            a = jnp.where(keep, _nt(qh, k_in[:, sl].astype(MXU_DTYPE)), 0.0)
            vh = v_ref[0, rows, vsl]
            st = st_ref[h]
            o_ref[0, rows, vsl] = _mm(a.astype(MXU_DTYPE), vh) + _nt(qh, st.astype(MXU_DTYPE))
            st_ref[h] = st * decay[:, sl] + _tn(vh, k_st[:, sl].astype(MXU_DTYPE))


def _gla(q, k, v, lg, direction, L):
    B, T, _ = q.shape
    tb = ROW_TILE
    nB, LB = T // tb, L // tb
    reverse = direction == 1
    if reverse:
        blk = lambda s: jnp.where(s < LB, LB - 1 - s, LB + nB - 1 - s)
    else:
        blk = lambda s: s
    return pl.pallas_call(
        functools.partial(_gla_kernel, reverse),
        grid=(B, nB),
        in_specs=[pl.BlockSpec((1, tb, D_HEADS * D_DK), lambda b, s: (b, blk(s), 0)),
                  pl.BlockSpec((1, tb, D_HEADS * D_DK), lambda b, s: (b, blk(s), 0)),
                  pl.BlockSpec((1, tb, D_HEADS * D_DV), lambda b, s: (b, blk(s), 0)),
                  pl.BlockSpec((1, tb, D_HEADS * D_DK), lambda b, s: (b, blk(s), direction))],
        out_specs=pl.BlockSpec((1, tb, D_HEADS * D_DV), lambda b, s: (b, blk(s), 0)),
        out_shape=jax.ShapeDtypeStruct((B, T, D_HEADS * D_DV), F32),
        scratch_shapes=[pltpu.VMEM((D_HEADS, D_DV, D_DK), F32)],
        compiler_params=_cparams("parallel", "arbitrary"),
        name="gla_bwd" if reverse else "gla_fwd",
    )(q, k, v, lg)


def _outproj_kernel(gla, *refs):
    it = iter(refs)
    xs_ref, o1_ref = next(it), next(it)
    if gla:
        of_ref, ob_ref, r_ref, dgn_ref = (next(it) for _ in range(4))
        og = of_ref[0] + ob_ref[0]
        rr = r_ref[0]
        parts = []
        for h in range(D_HEADS):
            sl = slice(h * D_DV, (h + 1) * D_DV)
            rh = rr[:, sl]
            parts.append((_rms(og[:, sl], dgn_ref[...]) * (rh * jax.nn.sigmoid(rh))).astype(MXU_DTYPE))
        o2 = jnp.concatenate(parts, axis=1)
    else:
        o2 = next(it)[0]
    w_ref, gate_ref, n2_ref, sh_ref, sc_ref, xo_ref, ht_ref = (next(it) for _ in range(7))
    cat = jnp.concatenate([o1_ref[0], o2], axis=1)
    x = xs_ref[0] + gate_ref[...] * _mm(cat, w_ref[...])
    xo_ref[0] = x
    h2 = _rms(x, n2_ref[...]) * (1.0 + sc_ref[...]) + sh_ref[...]
    ht_ref[...] = h2.T.astype(ht_ref.dtype)


def _outproj(xs, o1, o2s, w, gate, norm_g, shift, scale, LT, gla):
    B, T, D = xs.shape
    tm = ROW_TILE
    nT = T // tm
    row = lambda b, i: (b, i, 0)
    cls = lambda b, i: (b, jnp.where(i >= LT, 1, 0), 0, 0)
    full2 = lambda b, i: (0, 0)
    args = [xs, o1]
    specs = [pl.BlockSpec((1, tm, D), row), pl.BlockSpec((1, tm, o1.shape[2]), row)]
    for a in o2s:
        args.append(a)
        specs.append(pl.BlockSpec(a.shape, full2) if a.ndim == 2 else pl.BlockSpec((1, tm, a.shape[2]), row))
    args += [w, gate, norm_g, shift, scale]
    specs += [pl.BlockSpec(w.shape, full2), pl.BlockSpec((None, None, 1, D), cls), pl.BlockSpec((1, D), full2),
              pl.BlockSpec((None, None, 1, D), cls), pl.BlockSpec((None, None, 1, D), cls)]
    return pl.pallas_call(
        functools.partial(_outproj_kernel, gla),
        grid=(B, nT), in_specs=specs,
        out_specs=[pl.BlockSpec((1, tm, D), row), pl.BlockSpec((D, tm), lambda b, i: (0, b * nT + i))],
        out_shape=[jax.ShapeDtypeStruct((B, T, D), F32), jax.ShapeDtypeStruct((D, B * T), MXU_DTYPE)],
        compiler_params=_cparams("parallel", "parallel"),
        name="outproj_gla" if gla else "outproj",
    )(*args)


def _top_values(s, k):
    vals = []
    for a in range(k):
        m = jnp.max(s, axis=0, keepdims=True)
        vals.append(m)
        if a + 1 < k:
            s = jnp.where(s == m, -jnp.inf, s)
    return jnp.concatenate(vals, axis=0)


def _peer_gate_kernel(ht_ref, wqt_ref, keys_ref, s2_ref, e2_ref, th_ref, cc_ref):
    qt = _mm(wqt_ref[...], ht_ref[...])
    n = P_NKEYS
    s1 = _mm(keys_ref[0], qt[:n].astype(MXU_DTYPE))
    s2 = _mm(keys_ref[1], qt[n:].astype(MXU_DTYPE))
    sv1 = _top_values(s1, P_TOPK)
    sv2 = _top_values(s2, P_TOPK)
    cand = jnp.concatenate([sv1[a:a + 1] + sv2 for a in range(P_TOPK)], axis=0)
    tau = _top_values(cand, P_TOPK)[P_TOPK - 1:P_TOPK]
    top = sv1[0:1] + sv2[0:1]
    z = jnp.sum(jnp.where(cand >= tau, jnp.exp(cand - top), 0.0), axis=0, keepdims=True)
    th = jnp.full(s1.shape, jnp.inf, F32)
    for b in range(P_TOPK):
        v = sv2[b:b + 1]
        th = jnp.minimum(th, jnp.where(s1 + v >= tau, v, jnp.inf))
    s2_ref[0] = s2
    e2_ref[0] = jnp.exp(s2 - sv2[0:1])
    th_ref[0] = th
    cc_ref[0] = jnp.exp(s1 - sv1[0:1]) / z


def _peer_token_tile(n_tok):
    return 512 if n_tok % 512 == 0 else 256


def _peer_gate(ht, wqt, keys):
    D, N = ht.shape
    tm = _peer_token_tile(N)
    out = jax.ShapeDtypeStruct((P_HEADS, P_NKEYS, N), F32)
    ospec = pl.BlockSpec((1, P_NKEYS, tm), lambda i, h: (h, 0, i))
    return pl.pallas_call(
        _peer_gate_kernel,
        grid=(N // tm, P_HEADS),
        in_specs=[pl.BlockSpec((D, tm), lambda i, h: (0, i)),
                  pl.BlockSpec((P_DKEY, D), lambda i, h: (h, 0)),
                  pl.BlockSpec((2, P_NKEYS, P_DKEY // 2), lambda i, h: (h, 0, 0))],
        out_specs=[ospec] * 4, out_shape=[out] * 4,
        compiler_params=_cparams("parallel", "arbitrary"),
        name="peer_gate",
    )(ht, wqt, keys)


def _peer_dense_kernel(ht_ref, s2_ref, e2_ref, th_ref, cc_ref, u_ref, vt_ref, o_ref, acc_ref, a_ref):
    j = pl.program_id(1)

    @pl.when(j == 0)
    def _():
        acc_ref[...] = jnp.zeros(acc_ref.shape, F32)

    n = P_NKEYS
    zt = _mm(u_ref[...], ht_ref[...])
    for ii in range(u_ref.shape[0] // n):
        g = None
        for h in range(P_HEADS):
            t = jnp.where(s2_ref[h] >= th_ref[h, ii:ii + 1, :], e2_ref[h], 0.0) * cc_ref[h, ii:ii + 1, :]
            g = t if g is None else g + t
        z = zt[ii * n:(ii + 1) * n]
        a = 0.5 * z * (1.0 + lax.erf(z * math.sqrt(0.5))) * g
        a_ref[ii * n:(ii + 1) * n, :] = a.astype(a_ref.dtype)
    acc_ref[...] += _mm(vt_ref[...], a_ref[...])

    @pl.when(j == pl.num_programs(1) - 1)
    def _():
        o_ref[...] = acc_ref[...].T


def _peer_dense(ht, s2, e2, th, cc, u, vt):
    D, N = ht.shape
    E = u.shape[0]
    tm = _peer_token_tile(N)
    te = 8 * P_NKEYS
    gspec = pl.BlockSpec((P_HEADS, P_NKEYS, tm), lambda i, j: (0, 0, i))
    rspec = pl.BlockSpec((P_HEADS, te // P_NKEYS, tm), lambda i, j: (0, j, i))
    return pl.pallas_call(
        _peer_dense_kernel,
        grid=(N // tm, E // te),
        in_specs=[pl.BlockSpec((D, tm), lambda i, j: (0, i)), gspec, gspec, rspec, rspec,
                  pl.BlockSpec((te, D), lambda i, j: (j, 0)),
                  pl.BlockSpec((D, te), lambda i, j: (0, j))],
        out_specs=pl.BlockSpec((tm, D), lambda i, j: (i, 0)),
        out_shape=jax.ShapeDtypeStruct((N, D), F32),
        scratch_shapes=[pltpu.VMEM((D, tm), F32), pltpu.VMEM((te, tm), MXU_DTYPE)],
        compiler_params=_cparams("parallel", "arbitrary"),
        name="peer_dense",
    )(ht, s2, e2, th, cc, u, vt)


def _final_kernel(xs_ref, pp_ref, g_ref, o_ref):
    o_ref[0] = xs_ref[0] + g_ref[...] * pp_ref[0]


def _final(xs, pp, gate, L):
    B, T, D = xs.shape
    tm = ROW_TILE
    LT = L // tm
    src = lambda b, i: (b, i + LT, 0)
    return pl.pallas_call(
        _final_kernel,
        grid=(B, (T - L) // tm),
        in_specs=[pl.BlockSpec((1, tm, D), src), pl.BlockSpec((1, tm, D), src),
                  pl.BlockSpec((None, None, 1, D), lambda b, i: (b, 1, 0, 0))],
        out_specs=pl.BlockSpec((1, tm, D), lambda b, i: (b, i, 0)),
        out_shape=jax.ShapeDtypeStruct((B, T - L, D), F32),
        compiler_params=_cparams("parallel", "parallel"),
        name="final_residual",
    )(xs, pp, gate)


def _even_odd(n):
    return np.concatenate([np.arange(0, n, 2), np.arange(1, n, 2)])


def _rope_angles(S, L, rot_dim):
    pos = jnp.arange(S)
    r = (pos // GRID_W).astype(F32)
    cc = (pos % GRID_W).astype(F32)
    axis_dim = rot_dim // 2
    inv = 1.0 / (ROPE_THETA ** (jnp.arange(0, axis_dim, 2, dtype=F32) / axis_dim))
    ang = jnp.concatenate([r[:, None] * inv, cc[:, None] * inv], axis=-1)
    cos = jnp.concatenate([jnp.ones((L, rot_dim // 2), F32), jnp.cos(ang)], axis=0)
    sin = jnp.concatenate([jnp.zeros((L, rot_dim // 2), F32), jnp.sin(ang)], axis=0)
    return cos, sin


def _rope_tables_64(S, L):
    cos, sin = _rope_angles(S, L, 64)
    return jnp.tile(jnp.concatenate([cos, cos], -1), (1, 2)), jnp.tile(jnp.concatenate([-sin, sin], -1), (1, 2))


def _rope_tables_mla(S, L):
    cos, sin = _rope_angles(S, L, C_ROPE)
    T = cos.shape[0]
    one, zero = jnp.ones((T, C_NOPE), F32), jnp.zeros((T, C_NOPE), F32)
    pad1, pad0 = jnp.ones((T, LANES - C_NOPE - C_ROPE), F32), jnp.zeros((T, LANES - C_NOPE - C_ROPE), F32)
    return (jnp.concatenate([one, cos, cos, pad1], -1), jnp.concatenate([zero, -sin, sin, pad0], -1))


def _prep_even(w_in, a_qn, a_kn, b_qn, b_kn):
    sizes = (512, 512, 512, 512, 128, 128)
    off = np.concatenate([[0], np.cumsum(sizes)])
    p64 = _even_odd(64)

    def seg(k, permute):
        idx = np.arange(off[k], off[k + 1])
        if permute:
            idx = idx.reshape(-1, 64)[:, p64].reshape(-1)
        return idx

    cols = np.concatenate([seg(0, True), seg(1, True), seg(3, True), seg(4, True), seg(2, False), seg(5, False)])
    w = w_in[:, cols].astype(MXU_DTYPE)
    q_scale = A_DQK ** -0.5
    gain = jnp.concatenate([jnp.tile(a_qn[p64], 8) * q_scale, jnp.tile(a_kn[p64], 8),
                            jnp.tile(b_qn[p64], 8) * (B_DH ** -0.5), jnp.tile(b_kn[p64], 2)])[None, :]
    return w, gain


def _pad_cols(a, n):
    return jnp.concatenate([a, jnp.zeros(a.shape[:-1] + (n - a.shape[-1],), a.dtype)], axis=-1)


def _prep_odd(w_in, c_wq_up, c_wkv_up, c_qn, c_kn, d_wg_f, d_bg_f, d_wg_b, d_bg_b):
    D = w_in.shape[0]
    p32 = _even_odd(C_ROPE)
    o = np.concatenate([[0], np.cumsum((C_Q_RANK, C_KV_RANK, C_ROPE, 256, 256, 512, D_GATE_RANK, D_GATE_RANK, 512))])
    seg = lambda k: w_in[:, o[k]:o[k + 1]]
    w = jnp.concatenate([seg(0), seg(1), _pad_cols(seg(2)[:, p32], LANES), seg(3), seg(4), seg(5),
                         _pad_cols(jnp.concatenate([seg(6), seg(7)], -1), LANES), seg(8)], axis=-1).astype(MXU_DTYPE)
    hd = C_NOPE + C_ROPE
    head_perm = np.concatenate([np.arange(C_NOPE), C_NOPE + p32])
    wq = c_wq_up.reshape(C_Q_RANK, C_HEADS, hd)[:, :, head_perm]
    wq = _pad_cols(wq, LANES).reshape(C_Q_RANK, C_HEADS * LANES).astype(MXU_DTYPE)
    wkv = c_wkv_up.reshape(C_KV_RANK, C_HEADS, C_NOPE + C_DV)
    wk = _pad_cols(wkv[:, :, :C_NOPE], LANES).reshape(C_KV_RANK, C_HEADS * LANES)
    wv = wkv[:, :, C_NOPE:].reshape(C_KV_RANK, C_HEADS * C_DV)
    wkv = jnp.concatenate([wk, wv], axis=-1).astype(MXU_DTYPE)
    qn = jnp.tile(_pad_cols(c_qn[head_perm] * (hd ** -0.5), LANES), C_HEADS)[None, :]
    kn = jnp.tile(_pad_cols(c_kn[head_perm], LANES), C_HEADS)[None, :]
    n = D_HEADS * D_DK
    wg = jnp.zeros((LANES, 2 * n), F32)
    wg = wg.at[:D_GATE_RANK, :n].set(d_wg_f).at[D_GATE_RANK:2 * D_GATE_RANK, n:].set(d_wg_b).astype(MXU_DTYPE)
    bg = jnp.concatenate([d_bg_f, d_bg_b])[None, :]
    return w, wq, wkv, qn, kn, wg, bg


def kernel(x, c, ctx, c_ctx, w_mod, b_mod, norm1, norm2, e_w_in, e_w_out, a_qn, a_kn, a_lq1, a_lk1, a_lq2, a_lk2, a_gn, b_qn, b_kn, b_sink, o_w_in, o_w_out, c_gq, c_gkv, c_wq_up, c_wkv_up, c_qn, c_kn, d_wg_f, d_bg_f, d_wg_b, d_bg_b, d_gn, p_wq, p_keys, p_u, p_v):
    B, S, D = x.shape
    L = ctx.shape[1]
    depth = w_mod.shape[0]
    assert L == ROW_TILE and S % ROW_TILE == 0 and S % GRID_W == 0
    LT = L // ROW_TILE
    bf = MXU_DTYPE

    cv = jnp.concatenate([c, c_ctx[None, :], jnp.zeros((8 - B - 1, D), F32)], axis=0)
    mods = _modulation(cv, w_mod, b_mod).reshape(depth, 8, 6, D)
    mod_tab = jnp.stack([jnp.broadcast_to(mods[:, B][:, None], (depth, B, 6, D)), mods[:, :B]], axis=2)
    mod = lambda i, k: mod_tab[i, :, :, k][:, :, None, :]

    cos64, sin64 = _rope_tables_64(S, L)
    cos_c, sin_c = _rope_tables_mla(S, L)

    xs = jnp.concatenate([ctx, x], axis=1)
    prev = None
    for i in range(depth):
        j = i // 2
        if i % 2 == 0:
            lam_init = 0.8 - 0.6 * math.exp(-0.3 * i)
            w, gain = _prep_even(e_w_in[j], a_qn[j], a_kn[j], b_qn[j], b_kn[j])
            xs, (aq, ak, bq, bk, av, bv) = _inproj(
                "even", xs, prev, norm1[i][None], mod(i, 0), mod(i, 1), w,
                [(gain, "full"), (cos64, "rows"), (sin64, "rows")],
                [(512, bf), (512, bf), (512, bf), (128, bf), (512, bf), (128, bf)], LT)
            lqk = jnp.stack([a_lq1[j], a_lk1[j], a_lq2[j], a_lk2[j]])
            o1 = _diff_attn(aq, ak, av, lqk, a_gn[j][None], L, lam_init)
            o2 = _win_attn(bq, bk, bv, b_sink[j], L)
            xs, ht = _outproj(xs, o1, [o2], e_w_out[j].astype(bf), mod(i, 2), norm2[i][None], mod(i, 3), mod(i, 4),
                              LT, gla=False)
        else:
            w, wq, wkv, qn, kn, wg, bg = _prep_odd(o_w_in[j], c_wq_up[j], c_wkv_up[j], c_qn[j], c_kn[j],
                                                   d_wg_f[j], d_bg_f[j], d_wg_b[j], d_bg_b[j])
            xs, (cq, ck, cvv, dq, dk, dv, lg, r) = _inproj(
                "odd", xs, prev, norm1[i][None], mod(i, 0), mod(i, 1), w,
                [(c_gq[j][None], "full"), (c_gkv[j][None], "full"), (wq, "full"), (wkv, "full"), (qn, "full"),
                 (kn, "full"), (wg, "full"), (bg, "full"), (cos_c, "rows"), (sin_c, "rows")],
                [(1024, bf), (1024, bf), (512, bf), (256, F32), (256, F32), (512, bf), (512, F32), (512, F32)], LT)
            o1 = _mla_attn(cq, ck, cvv, L)
            of = _gla(dq, dk, dv, lg, 0, L)
            ob = _gla(dq, dk, dv, lg, 1, L)
            xs, ht = _outproj(xs, o1, [of, ob, r, d_gn[j][None]], o_w_out[j].astype(bf), mod(i, 2), norm2[i][None],
                              mod(i, 3), mod(i, 4), LT, gla=True)
        wqt = p_wq[i].T.astype(bf)
        keys = p_keys[i].reshape(2 * P_HEADS, P_NKEYS, P_DKEY // 2).astype(bf)
        s2, e2, th, cc = _peer_gate(ht, wqt, keys)
        pp = _peer_dense(ht, s2, e2, th, cc, p_u[i].astype(bf), p_v[i].T.astype(bf))
        prev = (pp.reshape(B, L + S, D), mod(i, 5))
    return _final(xs, prev[0], prev[1], L)
```

```python
import functools
import math

import numpy as np
import jax
import jax.numpy as jnp
from jax import lax
from jax.experimental import pallas as pl
from jax.experimental.pallas import tpu as pltpu

F32 = jnp.float32
MXU_DTYPE = jnp.bfloat16
GATE_DTYPE = jnp.bfloat16
PACKED_ROWS = 16
PEER_SUB = 256

GRID_W = 64
ROPE_THETA = 10000.0
EPS = 1e-6
NEG = -1e30
A_HEADS, A_DQK, A_DV = 4, 64, 128
B_HEADS, B_KV_HEADS, B_DH, WINDOW = 8, 2, 64, 128
C_HEADS, C_Q_RANK, C_KV_RANK, C_NOPE, C_ROPE, C_DV = 8, 256, 128, 64, 32, 64
D_HEADS, D_DK, D_DV, D_GATE_RANK, D_GATE_TAU, D_CHUNK = 4, 64, 128, 16, 16.0, 64
P_HEADS, P_NKEYS, P_DKEY, P_TOPK = 8, 128, 256, 16

LANES = 128
ROW_TILE = 256
VMEM_LIMIT = 56 * 1024 * 1024


def _cparams(*sem):
    return pltpu.CompilerParams(dimension_semantics=sem, vmem_limit_bytes=VMEM_LIMIT)


def _rms(x, g):
    return x * lax.rsqrt(jnp.mean(x * x, axis=-1, keepdims=True) + EPS) * g


def _nt(a, b):
    return lax.dot_general(a, b, (((1,), (1,)), ((), ())), preferred_element_type=F32)


def _tn(a, b):
    return lax.dot_general(a, b, (((0,), (0,)), ((), ())), preferred_element_type=F32)


def _mm(a, b):
    return jnp.dot(a, b, preferred_element_type=F32)


def _mod_kernel(c_ref, w_ref, b_ref, o_ref):
    cv = c_ref[...]
    s = cv * jax.nn.sigmoid(cv)
    o_ref[0] = jnp.dot(s, w_ref[0], precision=lax.Precision.HIGHEST, preferred_element_type=F32) + b_ref[0]


def _modulation(cvecs, w_mod, b_mod):
    depth, d, n = w_mod.shape
    rows = cvecs.shape[0]
    tn = 1536
    return pl.pallas_call(
        _mod_kernel,
        grid=(depth, n // tn),
        in_specs=[pl.BlockSpec((rows, d), lambda l, j: (0, 0)),
                  pl.BlockSpec((1, d, tn), lambda l, j: (l, 0, j)),
                  pl.BlockSpec((1, 1, tn), lambda l, j: (l, 0, j))],
        out_specs=pl.BlockSpec((1, rows, tn), lambda l, j: (l, 0, j)),
        out_shape=jax.ShapeDtypeStruct((depth, rows, n), F32),
        compiler_params=_cparams("parallel", "parallel"),
        name="modulation",
    )(cvecs, w_mod, b_mod.reshape(depth, 1, n))


def _stream_prologue(has_prev, it):
    xs_ref = next(it)
    x = xs_ref[0]
    if has_prev:
        pp_ref, pg_ref = next(it), next(it)
        x = x + pg_ref[...] * pp_ref[0]
    g1_ref, sh_ref, sc_ref, w_ref = next(it), next(it), next(it), next(it)
    h = _rms(x, g1_ref[...]) * (1.0 + sc_ref[...]) + sh_ref[...]
    z = _mm(h.astype(MXU_DTYPE), w_ref[...])
    return x, z


def _norm_rope_pair64(zg, gain, cosf, sinf):
    lane = lax.broadcasted_iota(jnp.int32, zg.shape, 1)
    lo = lane < 64
    sq = zg * zg
    s_lo = jnp.sum(jnp.where(lo, sq, 0.0), axis=-1, keepdims=True)
    s_hi = jnp.sum(jnp.where(lo, 0.0, sq), axis=-1, keepdims=True)
    ms = jnp.where(lo, s_lo, s_hi) * (1.0 / 64.0)
    y = zg * lax.rsqrt(ms + EPS) * gain
    partner = jnp.where((lane & 63) < 32, pltpu.roll(y, 96, 1), pltpu.roll(y, 32, 1))
    return y * cosf + partner * sinf


def _inproj_even_kernel(has_prev, *refs):
    it = iter(refs)
    x, z = _stream_prologue(has_prev, it)
    gain_ref, cos_ref, sin_ref = next(it), next(it), next(it)
    if has_prev:
        xo_ref = next(it)
        xo_ref[0] = x
    aq_ref, ak_ref, bq_ref, bk_ref, av_ref, bv_ref = (next(it) for _ in range(6))
    cosf, sinf = cos_ref[...], sin_ref[...]
    g = 0
    for ref, ngroups in ((aq_ref, 4), (ak_ref, 4), (bq_ref, 4), (bk_ref, 1)):
        for k in range(ngroups):
            sl = slice(g * LANES, (g + 1) * LANES)
            y = _norm_rope_pair64(z[:, sl], gain_ref[:, sl], cosf, sinf)
            ref[0, :, k * LANES:(k + 1) * LANES] = y.astype(ref.dtype)
            g += 1
    o = g * LANES
    av_ref[0] = z[:, o:o + 512].astype(av_ref.dtype)
    bv_ref[0] = z[:, o + 512:o + 640].astype(bv_ref.dtype)


def _norm_rope_mla(y, gain, cosf, sinf):
    lane = lax.broadcasted_iota(jnp.int32, y.shape, 1)
    ms = jnp.sum(y * y, axis=-1, keepdims=True) * (1.0 / (C_NOPE + C_ROPE))
    y = y * lax.rsqrt(ms + EPS) * gain
    partner = jnp.where(lane < C_NOPE + C_ROPE // 2, pltpu.roll(y, LANES - C_ROPE // 2, 1),
                        pltpu.roll(y, C_ROPE // 2, 1))
    return y * cosf + partner * sinf


def _log_sigmoid(x):
    return jnp.minimum(x, 0.0) - jnp.log1p(jnp.exp(-jnp.abs(x)))


def _inproj_odd_kernel(has_prev, *refs):
    it = iter(refs)
    x, z = _stream_prologue(has_prev, it)
    (gq_ref, gkv_ref, wq_ref, wkv_ref, qn_ref, kn_ref, wg_ref, bg_ref, cos_ref, sin_ref) = (next(it) for _ in range(10))
    if has_prev:
        xo_ref = next(it)
        xo_ref[0] = x
    cq_ref, ck_ref, cv_ref, dq_ref, dk_ref, dv_ref, lg_ref, r_ref = (next(it) for _ in range(8))
    cosf, sinf = cos_ref[...], sin_ref[...]
    qn = _rms(z[:, 0:256], gq_ref[...])
    q_up = _mm(qn.astype(MXU_DTYPE), wq_ref[...])
    kvn = _rms(z[:, 256:384], gkv_ref[...])
    kv_up = _mm(kvn.astype(MXU_DTYPE), wkv_ref[...])
    kpe = pltpu.roll(z[:, 384:512], C_NOPE, 1)
    for h in range(C_HEADS):
        sl = slice(h * LANES, (h + 1) * LANES)
        cq_ref[0, :, sl] = _norm_rope_mla(q_up[:, sl], qn_ref[:, sl], cosf, sinf).astype(cq_ref.dtype)
        ck_ref[0, :, sl] = _norm_rope_mla(kv_up[:, sl] + kpe, kn_ref[:, sl], cosf, sinf).astype(ck_ref.dtype)
    cv_ref[0] = kv_up[:, C_HEADS * LANES:].astype(cv_ref.dtype)
    dq_ref[0] = z[:, 512:768] * (D_DK ** -0.5)
    dk_ref[0] = z[:, 768:1024]
    dv_ref[0] = z[:, 1024:1536].astype(dv_ref.dtype)
    gl = _mm(z[:, 1536:1664].astype(MXU_DTYPE), wg_ref[...]) + bg_ref[...]
    lg_ref[0] = _log_sigmoid(gl) * (1.0 / D_GATE_TAU)
    r_ref[0] = z[:, 1664:2176]


def _inproj(kind, xs, prev, norm_g, shift, scale, w, extra, out_cols, LT):
    B, T, D = xs.shape
    tm = ROW_TILE
    nT = T // tm
    has_prev = prev is not None
    row = lambda b, i: (b, i, 0)
    cls = lambda b, i: (b, jnp.where(i >= LT, 1, 0), 0, 0)
    full2 = lambda b, i: (0, 0)
    args, specs = [xs], [pl.BlockSpec((1, tm, D), row)]
    if has_prev:
        args += [prev[0], prev[1]]
        specs += [pl.BlockSpec((1, tm, D), row), pl.BlockSpec((None, None, 1, D), cls)]
    args += [norm_g, shift, scale, w]
    specs += [pl.BlockSpec((1, D), full2), pl.BlockSpec((None, None, 1, D), cls),
              pl.BlockSpec((None, None, 1, D), cls), pl.BlockSpec(w.shape, full2)]
    for arr, k in extra:
        args.append(arr)
        if k == "full":
            specs.append(pl.BlockSpec(arr.shape, full2))
        else:
            specs.append(pl.BlockSpec((tm, arr.shape[1]), lambda b, i: (i, 0)))
    out_shape, out_specs = [], []
    if has_prev:
        out_shape.append(jax.ShapeDtypeStruct((B, T, D), F32))
        out_specs.append(pl.BlockSpec((1, tm, D), row))
    for n, dt in out_cols:
        out_shape.append(jax.ShapeDtypeStruct((B, T, n), dt))
        out_specs.append(pl.BlockSpec((1, tm, n), row))
    body = _inproj_even_kernel if kind == "even" else _inproj_odd_kernel
    outs = pl.pallas_call(
        functools.partial(body, has_prev),
        grid=(B, nT), in_specs=specs, out_specs=out_specs, out_shape=out_shape,
        compiler_params=_cparams("parallel", "parallel"),
        name="inproj_" + kind,
    )(*args)
    if has_prev:
        return outs[0], outs[1:]
    return xs, outs


def _online_softmax_step(s, v, m_ref, l_ref, acc_ref):
    m_prev = m_ref[...]
    m_new = jnp.maximum(m_prev, jnp.max(s, axis=-1, keepdims=True))
    alpha = jnp.exp(m_prev - m_new)
    p = jnp.exp(s - m_new)
    l_ref[...] = alpha * l_ref[...] + jnp.sum(p, axis=-1, keepdims=True)
    acc_ref[...] = alpha * acc_ref[...] + _mm(p.astype(MXU_DTYPE), v)
    m_ref[...] = m_new


def _flash_guard(L, tq, step, finalize):
    i, j, nk = pl.program_id(2), pl.program_id(3), pl.num_programs(3)
    is_ctx = i < L // tq

    @pl.when(jnp.logical_not(is_ctx))
    def _():
        step(False)

    @pl.when(jnp.logical_and(is_ctx, j == 0))
    def _():
        step(True)

    @pl.when(j == nk - 1)
    def _():
        finalize()


def _diff_attn_kernel(L, lam_init, q_ref, k_ref, v_ref, lqk_ref, gn_ref, o_ref, qs_ref, m_ref, l_ref, acc_ref):
    tq = q_ref.shape[1]

    @pl.when(pl.program_id(3) == 0)
    def _():
        q = q_ref[0]
        lane = lax.broadcasted_iota(jnp.int32, q.shape, 1)
        zero = jnp.zeros_like(q)
        qs_ref[:tq] = jnp.where(lane < A_DQK, q, zero)
        qs_ref[tq:] = jnp.where(lane < A_DQK, zero, q)
        m_ref[...] = jnp.full(m_ref.shape, -jnp.inf, F32)
        l_ref[...] = jnp.zeros(l_ref.shape, F32)
        acc_ref[...] = jnp.zeros(acc_ref.shape, F32)

    def step(masked):
        s = _nt(qs_ref[...], k_ref[0])
        if masked:
            col = lax.broadcasted_iota(jnp.int32, s.shape, 1)
            s = jnp.where(col < L, s, NEG)
        _online_softmax_step(s, v_ref[0], m_ref, l_ref, acc_ref)

    def finalize():
        lqk = lqk_ref[...]
        lam = (jnp.exp(jnp.sum(lqk[0:1] * lqk[1:2], axis=-1, keepdims=True))
               - jnp.exp(jnp.sum(lqk[2:3] * lqk[3:4], axis=-1, keepdims=True)) + lam_init)
        o = acc_ref[:tq] / l_ref[:tq] - lam * (acc_ref[tq:] / l_ref[tq:])
        o_ref[0] = (_rms(o, gn_ref[...]) * (1.0 - lam_init)).astype(o_ref.dtype)

    _flash_guard(L, tq, step, finalize)


def _key_tile(T):
    for tk in (768, 512, 256):
        if T % tk == 0:
            return tk
    raise ValueError("stream length must be a multiple of 256")


def _diff_attn(q, k, v, lqk, gn, L, lam_init):
    B, T, _ = q.shape
    tq, tk = ROW_TILE, _key_tile(T)
    assert L % tq == 0 and L <= tk
    return pl.pallas_call(
        functools.partial(_diff_attn_kernel, L, lam_init),
        grid=(B, A_HEADS, T // tq, T // tk),
        in_specs=[pl.BlockSpec((1, tq, LANES), lambda b, h, i, j: (b, i, h)),
                  pl.BlockSpec((1, tk, LANES), lambda b, h, i, j: (b, j, h)),
                  pl.BlockSpec((1, tk, LANES), lambda b, h, i, j: (b, j, h)),
                  pl.BlockSpec(lqk.shape, lambda b, h, i, j: (0, 0)),
                  pl.BlockSpec(gn.shape, lambda b, h, i, j: (0, 0))],
        out_specs=pl.BlockSpec((1, tq, LANES), lambda b, h, i, j: (b, i, h)),
        out_shape=jax.ShapeDtypeStruct((B, T, A_HEADS * A_DV), MXU_DTYPE),
        scratch_shapes=[pltpu.VMEM((2 * tq, LANES), MXU_DTYPE), pltpu.VMEM((2 * tq, 1), F32),
                        pltpu.VMEM((2 * tq, 1), F32), pltpu.VMEM((2 * tq, LANES), F32)],
        compiler_params=_cparams("parallel", "parallel", "parallel", "arbitrary"),
        name="diff_attn",
    )(q, k, v, lqk, gn)


def _mla_attn_kernel(L, q_ref, k_ref, v_ref, o_ref, m_ref, l_ref, acc_ref):
    tq = q_ref.shape[1]

    @pl.when(pl.program_id(3) == 0)
    def _():
        m_ref[...] = jnp.full(m_ref.shape, -jnp.inf, F32)
        l_ref[...] = jnp.zeros(l_ref.shape, F32)
        acc_ref[...] = jnp.zeros(acc_ref.shape, F32)

    def step(masked):
        s = jnp.concatenate([_nt(q_ref[0, :, :LANES], k_ref[0, :, :LANES]),
                             _nt(q_ref[0, :, LANES:], k_ref[0, :, LANES:])], axis=0)
        if masked:
            col = lax.broadcasted_iota(jnp.int32, s.shape, 1)
            s = jnp.where(col < L, s, NEG)
        _online_softmax_step(s, v_ref[0], m_ref, l_ref, acc_ref)

    def finalize():
        lane = lax.broadcasted_iota(jnp.int32, (tq, LANES), 1)
        o = jnp.where(lane < C_DV, acc_ref[:tq] / l_ref[:tq], acc_ref[tq:] / l_ref[tq:])
        o_ref[0] = o.astype(o_ref.dtype)

    _flash_guard(L, tq, step, finalize)


def _mla_attn(q, k, v, L):
    B, T, _ = q.shape
    tq, tk = ROW_TILE, _key_tile(T)
    assert L % tq == 0 and L <= tk
    return pl.pallas_call(
        functools.partial(_mla_attn_kernel, L),
        grid=(B, C_HEADS // 2, T // tq, T // tk),
        in_specs=[pl.BlockSpec((1, tq, 2 * LANES), lambda b, h, i, j: (b, i, h)),
                  pl.BlockSpec((1, tk, 2 * LANES), lambda b, h, i, j: (b, j, h)),
                  pl.BlockSpec((1, tk, LANES), lambda b, h, i, j: (b, j, h))],
        out_specs=pl.BlockSpec((1, tq, LANES), lambda b, h, i, j: (b, i, h)),
        out_shape=jax.ShapeDtypeStruct((B, T, C_HEADS * C_DV), MXU_DTYPE),
        scratch_shapes=[pltpu.VMEM((2 * tq, 1), F32), pltpu.VMEM((2 * tq, 1), F32),
                        pltpu.VMEM((2 * tq, LANES), F32)],
        compiler_params=_cparams("parallel", "parallel", "parallel", "arbitrary"),
        name="mla_attn",
    )(q, k, v)


def _win_attn_kernel(LT, sink_ref, q_ref, kc_ref, kp_ref, kk_ref, kn_ref, vc_ref, vp_ref, vk_ref, vn_ref, o_ref):
    i, nT = pl.program_id(1), pl.num_programs(1)
    tq = q_ref.shape[1]
    tk = kc_ref.shape[1]
    r = lax.broadcasted_iota(jnp.int32, (2 * tq, tk), 0)
    c = lax.broadcasted_iota(jnp.int32, (2 * tq, tk), 1)
    d = c - jnp.where(r >= tq, r - tq, r)
    is_x = i >= LT
    far = 4 * tk
    ok_p = d >= jnp.where(jnp.logical_and(is_x, i - 1 >= LT), tk - WINDOW, far)
    ok_k = jnp.abs(d) <= jnp.where(is_x, WINDOW, -1)
    ok_n = d <= jnp.where(jnp.logical_and(is_x, i + 1 <= nT - 1), WINDOW - tk, -far)
    k_tiles = (kc_ref[0], kp_ref[0], kk_ref[0], kn_ref[0])
    masks = (None, ok_p, ok_k, ok_n)
    vals = jnp.concatenate([vc_ref[0], vp_ref[0], vk_ref[0], vn_ref[0]], axis=0)
    lane = lax.broadcasted_iota(jnp.int32, (tq, LANES), 1)
    lo = lane < B_DH
    rows = lax.broadcasted_iota(jnp.int32, (2 * tq, 1), 0)
    group = B_HEADS // B_KV_HEADS
    for g in range(B_HEADS // 2):
        kv = (2 * g) // group
        qg = q_ref[0, :, g * LANES:(g + 1) * LANES].astype(F32)
        qr = pltpu.roll(qg, B_DH, 1)
        if kv == 0:
            qa, qb = jnp.where(lo, qg, 0.0), jnp.where(lo, qr, 0.0)
        else:
            qa, qb = jnp.where(lo, 0.0, qr), jnp.where(lo, 0.0, qg)
        qs = jnp.concatenate([qa, qb], axis=0).astype(MXU_DTYPE)
        s = jnp.concatenate([_nt(qs, kt) if ok is None else jnp.where(ok, _nt(qs, kt), NEG)
                             for kt, ok in zip(k_tiles, masks)], axis=1)
        sk = jnp.where(rows < tq, sink_ref[2 * g], sink_ref[2 * g + 1])
        m = jnp.maximum(jnp.max(s, axis=-1, keepdims=True), sk)
        p = jnp.exp(s - m)
        l = jnp.sum(p, axis=-1, keepdims=True) + jnp.exp(sk - m)
        acc = _mm(p.astype(MXU_DTYPE), vals) / l
        if kv == 0:
            out = jnp.where(lo, acc[:tq], pltpu.roll(acc[tq:], B_DH, 1))
        else:
            out = jnp.where(lo, pltpu.roll(acc[:tq], B_DH, 1), acc[tq:])
        o_ref[0, :, g * LANES:(g + 1) * LANES] = out.astype(o_ref.dtype)


def _win_attn(q, k, v, sink, L):
    B, T, _ = q.shape
    t = ROW_TILE
    assert L == t and WINDOW <= t
    nT, LT = T // t, L // t
    kspec = lambda f: pl.BlockSpec((1, t, LANES), f)
    maps = [lambda b, i: (b, 0, 0),
            lambda b, i: (b, jnp.maximum(i - 1, LT), 0),
            lambda b, i: (b, i, 0),
            lambda b, i: (b, jnp.minimum(i + 1, nT - 1), 0)]
    return pl.pallas_call(
        functools.partial(_win_attn_kernel, LT),
        grid=(B, nT),
        in_specs=[pl.BlockSpec(memory_space=pltpu.SMEM),
                  pl.BlockSpec((1, t, B_HEADS * B_DH), lambda b, i: (b, i, 0))]
                 + [kspec(f) for f in maps] + [kspec(f) for f in maps],
        out_specs=pl.BlockSpec((1, t, B_HEADS * B_DH), lambda b, i: (b, i, 0)),
        out_shape=jax.ShapeDtypeStruct((B, T, B_HEADS * B_DH), MXU_DTYPE),
        compiler_params=_cparams("parallel", "parallel"),
        name="window_attn",
    )(sink, q, k, k, k, k, v, v, v, v)


def _gla_kernel(reverse, q_ref, k_ref, v_ref, g_ref, o_ref, st_ref):
    @pl.when(pl.program_id(1) == 0)
    def _():
        st_ref[...] = jnp.zeros(st_ref.shape, F32)

    C = D_CHUNK
    tb = q_ref.shape[1]
    r = lax.broadcasted_iota(jnp.int32, (C, C), 0)
    c = lax.broadcasted_iota(jnp.int32, (C, C), 1)
    keep = (c >= r) if reverse else (c <= r)
    tri = jnp.where(keep, 1.0, 0.0).astype(F32)
    chunks = range(tb // C)
    for ch in (reversed(chunks) if reverse else chunks):
        rows = slice(ch * C, (ch + 1) * C)
        b = jnp.dot(tri, g_ref[0, rows, :], precision=lax.Precision.HIGHEST, preferred_element_type=F32)
        b_last = b[0:1] if reverse else b[C - 1:C]
        q_t = q_ref[0, rows, :] * jnp.exp(b)
        kk = k_ref[0, rows, :]
        k_in = kk * jnp.exp(-b)
        k_st = kk * jnp.exp(b_last - b)
        decay = jnp.exp(b_last)
        for h in range(D_HEADS):
            sl = slice(h * D_DK, (h + 1) * D_DK)
            vsl = slice(h * D_DV, (h + 1) * D_DV)
            qh = q_t[:, sl].astype(MXU_DTYPE)
            a = jnp.where(keep, _nt(qh, k_in[:, sl].astype(MXU_DTYPE)), 0.0)
            vh = v_ref[0, rows, vsl]
            st = st_ref[h]
            o_ref[0, rows, vsl] = _mm(a.astype(MXU_DTYPE), vh) + _nt(qh, st.astype(MXU_DTYPE))
            st_ref[h] = st * decay[:, sl] + _tn(vh, k_st[:, sl].astype(MXU_DTYPE))


def _gla(q, k, v, lg, direction, L):
    B, T, _ = q.shape
    tb = ROW_TILE
    nB, LB = T // tb, L // tb
    reverse = direction == 1
    if reverse:
        blk = lambda s: jnp.where(s < LB, LB - 1 - s, LB + nB - 1 - s)
    else:
        blk = lambda s: s
    return pl.pallas_call(
        functools.partial(_gla_kernel, reverse),
        grid=(B, nB),
        in_specs=[pl.BlockSpec((1, tb, D_HEADS * D_DK), lambda b, s: (b, blk(s), 0)),
                  pl.BlockSpec((1, tb, D_HEADS * D_DK), lambda b, s: (b, blk(s), 0)),
                  pl.BlockSpec((1, tb, D_HEADS * D_DV), lambda b, s: (b, blk(s), 0)),
                  pl.BlockSpec((1, tb, D_HEADS * D_DK), lambda b, s: (b, blk(s), direction))],
        out_specs=pl.BlockSpec((1, tb, D_HEADS * D_DV), lambda b, s: (b, blk(s), 0)),
        out_shape=jax.ShapeDtypeStruct((B, T, D_HEADS * D_DV), F32),
        scratch_shapes=[pltpu.VMEM((D_HEADS, D_DV, D_DK), F32)],
        compiler_params=_cparams("parallel", "arbitrary"),
        name="gla_bwd" if reverse else "gla_fwd",
    )(q, k, v, lg)


def _outproj_kernel(gla, *refs):
    it = iter(refs)
    xs_ref, o1_ref = next(it), next(it)
    if gla:
        of_ref, ob_ref, r_ref, dgn_ref = (next(it) for _ in range(4))
        og = of_ref[0] + ob_ref[0]
        rr = r_ref[0]
        parts = []
        for h in range(D_HEADS):
            sl = slice(h * D_DV, (h + 1) * D_DV)
            rh = rr[:, sl]
            parts.append((_rms(og[:, sl], dgn_ref[...]) * (rh * jax.nn.sigmoid(rh))).astype(MXU_DTYPE))
        o2 = jnp.concatenate(parts, axis=1)
    else:
        o2 = next(it)[0]
    w_ref, gate_ref, n2_ref, sh_ref, sc_ref, xo_ref, ht_ref = (next(it) for _ in range(7))
    cat = jnp.concatenate([o1_ref[0], o2], axis=1)
    x = xs_ref[0] + gate_ref[...] * _mm(cat, w_ref[...])
    xo_ref[0] = x
    h2 = _rms(x, n2_ref[...]) * (1.0 + sc_ref[...]) + sh_ref[...]
    ht_ref[...] = h2.T.astype(ht_ref.dtype)


def _outproj(xs, o1, o2s, w, gate, norm_g, shift, scale, LT, gla):
    B, T, D = xs.shape
    tm = ROW_TILE
    nT = T // tm
    row = lambda b, i: (b, i, 0)
    cls = lambda b, i: (b, jnp.where(i >= LT, 1, 0), 0, 0)
    full2 = lambda b, i: (0, 0)
    args = [xs, o1]
    specs = [pl.BlockSpec((1, tm, D), row), pl.BlockSpec((1, tm, o1.shape[2]), row)]
    for a in o2s:
        args.append(a)
        specs.append(pl.BlockSpec(a.shape, full2) if a.ndim == 2 else pl.BlockSpec((1, tm, a.shape[2]), row))
    args += [w, gate, norm_g, shift, scale]
    specs += [pl.BlockSpec(w.shape, full2), pl.BlockSpec((None, None, 1, D), cls), pl.BlockSpec((1, D), full2),
              pl.BlockSpec((None, None, 1, D), cls), pl.BlockSpec((None, None, 1, D), cls)]
    return pl.pallas_call(
        functools.partial(_outproj_kernel, gla),
        grid=(B, nT), in_specs=specs,
        out_specs=[pl.BlockSpec((1, tm, D), row), pl.BlockSpec((D, tm), lambda b, i: (0, b * nT + i))],
        out_shape=[jax.ShapeDtypeStruct((B, T, D), F32), jax.ShapeDtypeStruct((D, B * T), MXU_DTYPE)],
        compiler_params=_cparams("parallel", "parallel"),
        name="outproj_gla" if gla else "outproj",
    )(*args)


def _top_values(s, k, with_rank=False):
    vals = []
    rank = jnp.full(s.shape, float(k), F32) if with_rank else None
    for a in range(k):
        m = jnp.max(s, axis=0, keepdims=True)
        vals.append(m)
        hit = s == m
        if with_rank:
            rank = jnp.where(hit, float(a), rank)
        if a + 1 < k:
            s = jnp.where(hit, -jnp.inf, s)
    vals = jnp.concatenate(vals, axis=0)
    return (vals, rank) if with_rank else vals


def _peer_gate_kernel(ht_ref, wqt_ref, keys_ref, r2_ref, e2_ref, cnt_ref, cc_ref):
    qt = _mm(wqt_ref[...], ht_ref[...])
    n, k = P_NKEYS, P_TOPK
    s1 = _mm(keys_ref[0], qt[:n].astype(MXU_DTYPE))
    s2 = _mm(keys_ref[1], qt[n:].astype(MXU_DTYPE))
    sv1 = _top_values(s1, k)
    sv2, rank2 = _top_values(s2, k, with_rank=True)
    cand = jnp.concatenate([sv1[a:a + 1] + sv2[:k // (a + 1)] for a in range(k)], axis=0)
    tau = _top_values(cand, k)[k - 1:k]
    top = sv1[0:1] + sv2[0:1]
    z = jnp.sum(jnp.where(cand >= tau, jnp.exp(cand - top), 0.0), axis=0, keepdims=True)
    cnt = jnp.zeros(s1.shape, F32)
    for b in range(k):
        cnt = cnt + jnp.where(s1 + sv2[b:b + 1] >= tau, 1.0, 0.0)
    r2_ref[0] = rank2.astype(r2_ref.dtype)
    e2_ref[0] = jnp.exp(s2 - sv2[0:1]).astype(e2_ref.dtype)
    cnt_ref[0] = cnt
    cc_ref[0] = jnp.exp(s1 - sv1[0:1]) / z


def _peer_token_tile(n_tok):
    return 512 if n_tok % 512 == 0 else 256


def _peer_gate(ht, wqt, keys):
    D, N = ht.shape
    tm = _peer_token_tile(N)
    shape = lambda dt: jax.ShapeDtypeStruct((P_HEADS, P_NKEYS, N), dt)
    ospec = pl.BlockSpec((1, P_NKEYS, tm), lambda i, h: (h, 0, i))
    return pl.pallas_call(
        _peer_gate_kernel,
        grid=(N // tm, P_HEADS),
        in_specs=[pl.BlockSpec((D, tm), lambda i, h: (0, i)),
                  pl.BlockSpec((P_DKEY, D), lambda i, h: (h, 0)),
                  pl.BlockSpec((2, P_NKEYS, P_DKEY // 2), lambda i, h: (h, 0, 0))],
        out_specs=[ospec] * 4,
        out_shape=[shape(GATE_DTYPE), shape(GATE_DTYPE), shape(F32), shape(F32)],
        compiler_params=_cparams("parallel", "arbitrary"),
        name="peer_gate",
    )(ht, wqt, keys)


def _rows_packed(row, n):
    tile = jnp.broadcast_to(row, (PACKED_ROWS, row.shape[1])).astype(GATE_DTYPE)
    return jnp.concatenate([tile] * (n // PACKED_ROWS), axis=0)


def _peer_dense_kernel(ht_ref, r2_ref, e2_ref, cnt_ref, cc_ref, u_ref, vt_ref, o_ref, acc_ref):
    j = pl.program_id(1)

    @pl.when(j == 0)
    def _():
        acc_ref[...] = jnp.zeros(acc_ref.shape, F32)

    n = P_NKEYS
    zero = jnp.zeros((n, ht_ref.shape[1]), GATE_DTYPE)
    ht = ht_ref[...]
    total = None
    nsub = u_ref.shape[0] // PEER_SUB
    scores = lambda k: _mm(u_ref[k * PEER_SUB:(k + 1) * PEER_SUB, :], ht)
    zt_next = scores(0)
    for k in range(nsub):
        zt = zt_next
        if k + 1 < nsub:
            zt_next = scores(k + 1)
        rows = []
        for ii in range(PEER_SUB // n):
            i1 = k * (PEER_SUB // n) + ii
            g = None
            for h in range(P_HEADS):
                sel = r2_ref[h] < _rows_packed(cnt_ref[h, i1:i1 + 1, :], n)
                t = jnp.where(sel, e2_ref[h], zero) * _rows_packed(cc_ref[h, i1:i1 + 1, :], n)
                g = t if g is None else g + t
            z = zt[ii * n:(ii + 1) * n]
            gelu = 0.5 * z * (1.0 + lax.erf(z * math.sqrt(0.5)))
            rows.append((gelu.astype(GATE_DTYPE) * g).astype(MXU_DTYPE))
        part = _mm(vt_ref[:, k * PEER_SUB:(k + 1) * PEER_SUB], jnp.concatenate(rows, axis=0))
        total = part if total is None else total + part
    acc_ref[...] += total

    @pl.when(j == pl.num_programs(1) - 1)
    def _():
        o_ref[...] = acc_ref[...].T


def _peer_dense(ht, r2, e2, cnt, cc, u, vt):
    D, N = ht.shape
    E = u.shape[0]
    tm = _peer_token_tile(N)
    te = 8 * P_NKEYS
    gspec = pl.BlockSpec((P_HEADS, P_NKEYS, tm), lambda i, j: (0, 0, i))
    rspec = pl.BlockSpec((P_HEADS, te // P_NKEYS, tm), lambda i, j: (0, j, i))
    return pl.pallas_call(
        _peer_dense_kernel,
        grid=(N // tm, E // te),
        in_specs=[pl.BlockSpec((D, tm), lambda i, j: (0, i)), gspec, gspec, rspec, rspec,
                  pl.BlockSpec((te, D), lambda i, j: (j, 0)),
                  pl.BlockSpec((D, te), lambda i, j: (0, j))],
        out_specs=pl.BlockSpec((tm, D), lambda i, j: (i, 0)),
        out_shape=jax.ShapeDtypeStruct((N, D), F32),
        scratch_shapes=[pltpu.VMEM((D, tm), F32)],
        compiler_params=_cparams("parallel", "arbitrary"),
        name="peer_dense",
    )(ht, r2, e2, cnt, cc, u, vt)


def _final_kernel(xs_ref, pp_ref, g_ref, o_ref):
    o_ref[0] = xs_ref[0] + g_ref[...] * pp_ref[0]


def _final(xs, pp, gate, L):
    B, T, D = xs.shape
    tm = ROW_TILE
    LT = L // tm
    src = lambda b, i: (b, i + LT, 0)
    return pl.pallas_call(
        _final_kernel,
        grid=(B, (T - L) // tm),
        in_specs=[pl.BlockSpec((1, tm, D), src), pl.BlockSpec((1, tm, D), src),
                  pl.BlockSpec((None, None, 1, D), lambda b, i: (b, 1, 0, 0))],
        out_specs=pl.BlockSpec((1, tm, D), lambda b, i: (b, i, 0)),
        out_shape=jax.ShapeDtypeStruct((B, T - L, D), F32),
        compiler_params=_cparams("parallel", "parallel"),
        name="final_residual",
    )(xs, pp, gate)


def _even_odd(n):
    return np.concatenate([np.arange(0, n, 2), np.arange(1, n, 2)])


def _rope_angles(S, L, rot_dim):
    pos = jnp.arange(S)
    r = (pos // GRID_W).astype(F32)
    cc = (pos % GRID_W).astype(F32)
    axis_dim = rot_dim // 2
    inv = 1.0 / (ROPE_THETA ** (jnp.arange(0, axis_dim, 2, dtype=F32) / axis_dim))
    ang = jnp.concatenate([r[:, None] * inv, cc[:, None] * inv], axis=-1)
    cos = jnp.concatenate([jnp.ones((L, rot_dim // 2), F32), jnp.cos(ang)], axis=0)
    sin = jnp.concatenate([jnp.zeros((L, rot_dim // 2), F32), jnp.sin(ang)], axis=0)
    return cos, sin


def _rope_tables_64(S, L):
    cos, sin = _rope_angles(S, L, 64)
    return jnp.tile(jnp.concatenate([cos, cos], -1), (1, 2)), jnp.tile(jnp.concatenate([-sin, sin], -1), (1, 2))


def _rope_tables_mla(S, L):
    cos, sin = _rope_angles(S, L, C_ROPE)
    T = cos.shape[0]
    one, zero = jnp.ones((T, C_NOPE), F32), jnp.zeros((T, C_NOPE), F32)
    pad1, pad0 = jnp.ones((T, LANES - C_NOPE - C_ROPE), F32), jnp.zeros((T, LANES - C_NOPE - C_ROPE), F32)
    return (jnp.concatenate([one, cos, cos, pad1], -1), jnp.concatenate([zero, -sin, sin, pad0], -1))


def _prep_even(w_in, a_qn, a_kn, b_qn, b_kn):
    sizes = (512, 512, 512, 512, 128, 128)
    off = np.concatenate([[0], np.cumsum(sizes)])
    p64 = _even_odd(64)

    def seg(k, permute):
        idx = np.arange(off[k], off[k + 1])
        if permute:
            idx = idx.reshape(-1, 64)[:, p64].reshape(-1)
        return idx

    cols = np.concatenate([seg(0, True), seg(1, True), seg(3, True), seg(4, True), seg(2, False), seg(5, False)])
    w = w_in[:, cols].astype(MXU_DTYPE)
    q_scale = A_DQK ** -0.5
    gain = jnp.concatenate([jnp.tile(a_qn[p64], 8) * q_scale, jnp.tile(a_kn[p64], 8),
                            jnp.tile(b_qn[p64], 8) * (B_DH ** -0.5), jnp.tile(b_kn[p64], 2)])[None, :]
    return w, gain


def _pad_cols(a, n):
    return jnp.concatenate([a, jnp.zeros(a.shape[:-1] + (n - a.shape[-1],), a.dtype)], axis=-1)


def _prep_odd(w_in, c_wq_up, c_wkv_up, c_qn, c_kn, d_wg_f, d_bg_f, d_wg_b, d_bg_b):
    D = w_in.shape[0]
    p32 = _even_odd(C_ROPE)
    o = np.concatenate([[0], np.cumsum((C_Q_RANK, C_KV_RANK, C_ROPE, 256, 256, 512, D_GATE_RANK, D_GATE_RANK, 512))])
    seg = lambda k: w_in[:, o[k]:o[k + 1]]
    w = jnp.concatenate([seg(0), seg(1), _pad_cols(seg(2)[:, p32], LANES), seg(3), seg(4), seg(5),
                         _pad_cols(jnp.concatenate([seg(6), seg(7)], -1), LANES), seg(8)], axis=-1).astype(MXU_DTYPE)
    hd = C_NOPE + C_ROPE
    head_perm = np.concatenate([np.arange(C_NOPE), C_NOPE + p32])
    wq = c_wq_up.reshape(C_Q_RANK, C_HEADS, hd)[:, :, head_perm]
    wq = _pad_cols(wq, LANES).reshape(C_Q_RANK, C_HEADS * LANES).astype(MXU_DTYPE)
    wkv = c_wkv_up.reshape(C_KV_RANK, C_HEADS, C_NOPE + C_DV)
    wk = _pad_cols(wkv[:, :, :C_NOPE], LANES).reshape(C_KV_RANK, C_HEADS * LANES)
    wv = wkv[:, :, C_NOPE:].reshape(C_KV_RANK, C_HEADS * C_DV)
    wkv = jnp.concatenate([wk, wv], axis=-1).astype(MXU_DTYPE)
    qn = jnp.tile(_pad_cols(c_qn[head_perm] * (hd ** -0.5), LANES), C_HEADS)[None, :]
    kn = jnp.tile(_pad_cols(c_kn[head_perm], LANES), C_HEADS)[None, :]
    n = D_HEADS * D_DK
    wg = jnp.zeros((LANES, 2 * n), F32)
    wg = wg.at[:D_GATE_RANK, :n].set(d_wg_f).at[D_GATE_RANK:2 * D_GATE_RANK, n:].set(d_wg_b).astype(MXU_DTYPE)
    bg = jnp.concatenate([d_bg_f, d_bg_b])[None, :]
    return w, wq, wkv, qn, kn, wg, bg


def kernel(x, c, ctx, c_ctx, w_mod, b_mod, norm1, norm2, e_w_in, e_w_out, a_qn, a_kn, a_lq1, a_lk1, a_lq2, a_lk2, a_gn, b_qn, b_kn, b_sink, o_w_in, o_w_out, c_gq, c_gkv, c_wq_up, c_wkv_up, c_qn, c_kn, d_wg_f, d_bg_f, d_wg_b, d_bg_b, d_gn, p_wq, p_keys, p_u, p_v):
    B, S, D = x.shape
    L = ctx.shape[1]
    depth = w_mod.shape[0]
    assert L == ROW_TILE and S % ROW_TILE == 0 and S % GRID_W == 0
    LT = L // ROW_TILE
    bf = MXU_DTYPE

    cv = jnp.concatenate([c, c_ctx[None, :], jnp.zeros((8 - B - 1, D), F32)], axis=0)
    mods = _modulation(cv, w_mod, b_mod).reshape(depth, 8, 6, D)
    mod_tab = jnp.stack([jnp.broadcast_to(mods[:, B][:, None], (depth, B, 6, D)), mods[:, :B]], axis=2)
    mod = lambda i, k: mod_tab[i, :, :, k][:, :, None, :]

    cos64, sin64 = _rope_tables_64(S, L)
    cos_c, sin_c = _rope_tables_mla(S, L)

    xs = jnp.concatenate([ctx, x], axis=1)
    prev = None
    for i in range(depth):
        j = i // 2
        if i % 2 == 0:
            lam_init = 0.8 - 0.6 * math.exp(-0.3 * i)
            w, gain = _prep_even(e_w_in[j], a_qn[j], a_kn[j], b_qn[j], b_kn[j])
            xs, (aq, ak, bq, bk, av, bv) = _inproj(
                "even", xs, prev, norm1[i][None], mod(i, 0), mod(i, 1), w,
                [(gain, "full"), (cos64, "rows"), (sin64, "rows")],
                [(512, bf), (512, bf), (512, bf), (128, bf), (512, bf), (128, bf)], LT)
            lqk = jnp.stack([a_lq1[j], a_lk1[j], a_lq2[j], a_lk2[j]])
            o1 = _diff_attn(aq, ak, av, lqk, a_gn[j][None], L, lam_init)
            o2 = _win_attn(bq, bk, bv, b_sink[j], L)
            xs, ht = _outproj(xs, o1, [o2], e_w_out[j].astype(bf), mod(i, 2), norm2[i][None], mod(i, 3), mod(i, 4),
                              LT, gla=False)
        else:
            w, wq, wkv, qn, kn, wg, bg = _prep_odd(o_w_in[j], c_wq_up[j], c_wkv_up[j], c_qn[j], c_kn[j],
                                                   d_wg_f[j], d_bg_f[j], d_wg_b[j], d_bg_b[j])
            xs, (cq, ck, cvv, dq, dk, dv, lg, r) = _inproj(
                "odd", xs, prev, norm1[i][None], mod(i, 0), mod(i, 1), w,
                [(c_gq[j][None], "full"), (c_gkv[j][None], "full"), (wq, "full"), (wkv, "full"), (qn, "full"),
                 (kn, "full"), (wg, "full"), (bg, "full"), (cos_c, "rows"), (sin_c, "rows")],
                [(1024, bf), (1024, bf), (512, bf), (256, F32), (256, F32), (512, bf), (512, F32), (512, F32)], LT)
            o1 = _mla_attn(cq, ck, cvv, L)
            of = _gla(dq, dk, dv, lg, 0, L)
            ob = _gla(dq, dk, dv, lg, 1, L)
            xs, ht = _outproj(xs, o1, [of, ob, r, d_gn[j][None]], o_w_out[j].astype(bf), mod(i, 2), norm2[i][None],
                              mod(i, 3), mod(i, 4), LT, gla=True)
        wqt = p_wq[i].T.astype(bf)
        keys = p_keys[i].reshape(2 * P_HEADS, P_NKEYS, P_DKEY // 2).astype(bf)
        r2, e2, cnt, cc = _peer_gate(ht, wqt, keys)
        pp = _peer_dense(ht, r2, e2, cnt, cc, p_u[i].astype(bf), p_v[i].T.astype(bf))
        prev = (pp.reshape(B, L + S, D), mod(i, 5))
    return _final(xs, prev[0], prev[1], L)
```

```python
import functools
import math

import numpy as np
import jax
import jax.numpy as jnp
from jax import lax
from jax.experimental import pallas as pl
from jax.experimental.pallas import tpu as pltpu

F32 = jnp.float32
MXU_DTYPE = jnp.bfloat16
GATE_DTYPE = jnp.bfloat16
PACKED_ROWS = 16
PEER_SUB = 256

GRID_W = 64
ROPE_THETA = 10000.0
EPS = 1e-6
NEG = -1e30
A_HEADS, A_DQK, A_DV = 4, 64, 128
B_HEADS, B_KV_HEADS, B_DH, WINDOW = 8, 2, 64, 128
C_HEADS, C_Q_RANK, C_KV_RANK, C_NOPE, C_ROPE, C_DV = 8, 256, 128, 64, 32, 64
D_HEADS, D_DK, D_DV, D_GATE_RANK, D_GATE_TAU, D_CHUNK = 4, 64, 128, 16, 16.0, 64
P_HEADS, P_NKEYS, P_DKEY, P_TOPK = 8, 128, 256, 16

LANES = 128
ROW_TILE = 256
KEY_CHUNK = 256
KEY_CHUNK_BIG = 768
LOG2_E = math.log2(math.e)
VMEM_LIMIT = 56 * 1024 * 1024


def _cparams(*sem):
    return pltpu.CompilerParams(dimension_semantics=sem, vmem_limit_bytes=VMEM_LIMIT)


def _rms(x, g):
    return x * lax.rsqrt(jnp.mean(x * x, axis=-1, keepdims=True) + EPS) * g


def _nt(a, b):
    return lax.dot_general(a, b, (((1,), (1,)), ((), ())), preferred_element_type=F32)


def _tn(a, b):
    return lax.dot_general(a, b, (((0,), (0,)), ((), ())), preferred_element_type=F32)


def _mm(a, b):
    return jnp.dot(a, b, preferred_element_type=F32)


def _mod_kernel(c_ref, w_ref, b_ref, o_ref):
    cv = c_ref[...]
    s = cv * jax.nn.sigmoid(cv)
    o_ref[0] = jnp.dot(s, w_ref[0], precision=lax.Precision.HIGHEST, preferred_element_type=F32) + b_ref[0]


def _modulation(cvecs, w_mod, b_mod):
    depth, d, n = w_mod.shape
    rows = cvecs.shape[0]
    tn = 1536
    return pl.pallas_call(
        _mod_kernel,
        grid=(depth, n // tn),
        in_specs=[pl.BlockSpec((rows, d), lambda l, j: (0, 0)),
                  pl.BlockSpec((1, d, tn), lambda l, j: (l, 0, j)),
                  pl.BlockSpec((1, 1, tn), lambda l, j: (l, 0, j))],
        out_specs=pl.BlockSpec((1, rows, tn), lambda l, j: (l, 0, j)),
        out_shape=jax.ShapeDtypeStruct((depth, rows, n), F32),
        compiler_params=_cparams("parallel", "parallel"),
        name="modulation",
    )(cvecs, w_mod, b_mod.reshape(depth, 1, n))


def _stream_prologue(has_prev, it):
    xs_ref = next(it)
    x = xs_ref[0]
    if has_prev:
        pp_ref, pg_ref = next(it), next(it)
        x = x + pg_ref[...] * pp_ref[0]
    g1_ref, sh_ref, sc_ref, w_ref = next(it), next(it), next(it), next(it)
    h = _rms(x, g1_ref[...]) * (1.0 + sc_ref[...]) + sh_ref[...]
    z = _mm(h.astype(MXU_DTYPE), w_ref[...])
    return x, z


def _norm_rope_pair64(zg, gain, cosf, sinf):
    lane = lax.broadcasted_iota(jnp.int32, zg.shape, 1)
    lo = lane < 64
    sq = zg * zg
    s_lo = jnp.sum(jnp.where(lo, sq, 0.0), axis=-1, keepdims=True)
    s_hi = jnp.sum(jnp.where(lo, 0.0, sq), axis=-1, keepdims=True)
    ms = jnp.where(lo, s_lo, s_hi) * (1.0 / 64.0)
    y = zg * lax.rsqrt(ms + EPS) * gain
    partner = jnp.where((lane & 63) < 32, pltpu.roll(y, 96, 1), pltpu.roll(y, 32, 1))
    return y * cosf + partner * sinf


def _inproj_even_kernel(has_prev, *refs):
    it = iter(refs)
    x, z = _stream_prologue(has_prev, it)
    gain_ref, cos_ref, sin_ref = next(it), next(it), next(it)
    if has_prev:
        xo_ref = next(it)
        xo_ref[0] = x
    aq_ref, ak_ref, bq_ref, bk_ref, av_ref, bv_ref = (next(it) for _ in range(6))
    cosf, sinf = cos_ref[...], sin_ref[...]
    g = 0
    for ref, ngroups, transposed in ((aq_ref, 4, True), (ak_ref, 4, False), (bq_ref, 4, False), (bk_ref, 1, False)):
        for k in range(ngroups):
            sl = slice(g * LANES, (g + 1) * LANES)
            y = _norm_rope_pair64(z[:, sl], gain_ref[:, sl], cosf, sinf)
            if transposed:
                ref[0, k * LANES:(k + 1) * LANES, :] = y.T.astype(ref.dtype)
            else:
                ref[0, :, k * LANES:(k + 1) * LANES] = y.astype(ref.dtype)
            g += 1
    o = g * LANES
    av_ref[0] = z[:, o:o + 512].T.astype(av_ref.dtype)
    bv_ref[0] = z[:, o + 512:o + 640].astype(bv_ref.dtype)


def _norm_rope_mla(y, gain, cosf, sinf):
    lane = lax.broadcasted_iota(jnp.int32, y.shape, 1)
    ms = jnp.sum(y * y, axis=-1, keepdims=True) * (1.0 / (C_NOPE + C_ROPE))
    y = y * lax.rsqrt(ms + EPS) * gain
    partner = jnp.where(lane < C_NOPE + C_ROPE // 2, pltpu.roll(y, LANES - C_ROPE // 2, 1),
                        pltpu.roll(y, C_ROPE // 2, 1))
    return y * cosf + partner * sinf


def _log_sigmoid(x):
    return jnp.minimum(x, 0.0) - jnp.log1p(jnp.exp(-jnp.abs(x)))


def _inproj_odd_kernel(has_prev, *refs):
    it = iter(refs)
    x, z = _stream_prologue(has_prev, it)
    (gq_ref, gkv_ref, wq_ref, wkv_ref, qn_ref, kn_ref, wg_ref, bg_ref, cos_ref, sin_ref) = (next(it) for _ in range(10))
    if has_prev:
        xo_ref = next(it)
        xo_ref[0] = x
    cq_ref, ck_ref, cv_ref, dq_ref, dk_ref, dv_ref, lg_ref, r_ref = (next(it) for _ in range(8))
    cosf, sinf = cos_ref[...], sin_ref[...]
    qn = _rms(z[:, 0:256], gq_ref[...])
    q_up = _mm(qn.astype(MXU_DTYPE), wq_ref[...])
    kvn = _rms(z[:, 256:384], gkv_ref[...])
    kv_up = _mm(kvn.astype(MXU_DTYPE), wkv_ref[...])
    kpe = pltpu.roll(z[:, 384:512], C_NOPE, 1)
    for h in range(C_HEADS):
        sl = slice(h * LANES, (h + 1) * LANES)
        cq_ref[0, sl, :] = _norm_rope_mla(q_up[:, sl], qn_ref[:, sl], cosf, sinf).T.astype(cq_ref.dtype)
        ck_ref[0, :, sl] = _norm_rope_mla(kv_up[:, sl] + kpe, kn_ref[:, sl], cosf, sinf).astype(ck_ref.dtype)
    cv_ref[0] = kv_up[:, C_HEADS * LANES:].T.astype(cv_ref.dtype)
    dq_ref[0] = z[:, 512:768] * (D_DK ** -0.5)
    dk_ref[0] = z[:, 768:1024]
    dv_ref[0] = z[:, 1024:1536].astype(dv_ref.dtype)
    gl = _mm(z[:, 1536:1664].astype(MXU_DTYPE), wg_ref[...]) + bg_ref[...]
    lg_ref[0] = _log_sigmoid(gl) * (1.0 / D_GATE_TAU)
    r_ref[0] = z[:, 1664:2176]


def _inproj(kind, xs, prev, norm_g, shift, scale, w, extra, out_cols, LT):
    B, T, D = xs.shape
    tm = ROW_TILE
    nT = T // tm
    has_prev = prev is not None
    row = lambda b, i: (b, i, 0)
    cls = lambda b, i: (b, jnp.where(i >= LT, 1, 0), 0, 0)
    full2 = lambda b, i: (0, 0)
    args, specs = [xs], [pl.BlockSpec((1, tm, D), row)]
    if has_prev:
        args += [prev[0], prev[1]]
        specs += [pl.BlockSpec((1, tm, D), row), pl.BlockSpec((None, None, 1, D), cls)]
    args += [norm_g, shift, scale, w]
    specs += [pl.BlockSpec((1, D), full2), pl.BlockSpec((None, None, 1, D), cls),
              pl.BlockSpec((None, None, 1, D), cls), pl.BlockSpec(w.shape, full2)]
    for arr, k in extra:
        args.append(arr)
        if k == "full":
            specs.append(pl.BlockSpec(arr.shape, full2))
        else:
            specs.append(pl.BlockSpec((tm, arr.shape[1]), lambda b, i: (i, 0)))
    out_shape, out_specs = [], []
    if has_prev:
        out_shape.append(jax.ShapeDtypeStruct((B, T, D), F32))
        out_specs.append(pl.BlockSpec((1, tm, D), row))
    for n, dt, transposed in out_cols:
        if transposed:
            out_shape.append(jax.ShapeDtypeStruct((B, n, T), dt))
            out_specs.append(pl.BlockSpec((1, n, tm), lambda b, i: (b, 0, i)))
        else:
            out_shape.append(jax.ShapeDtypeStruct((B, T, n), dt))
            out_specs.append(pl.BlockSpec((1, tm, n), row))
    body = _inproj_even_kernel if kind == "even" else _inproj_odd_kernel
    outs = pl.pallas_call(
        functools.partial(body, has_prev),
        grid=(B, nT), in_specs=specs, out_specs=out_specs, out_shape=out_shape,
        compiler_params=_cparams("parallel", "parallel"),
        name="inproj_" + kind,
    )(*args)
    if has_prev:
        return outs[0], outs[1:]
    return xs, outs


def _flash_streams(L, q_streams, k_cols, v_rows, k_ref, vt_ref, ml_ref, acc_ref):
    T = k_ref.shape[1]
    tq = q_streams[0].shape[1]
    acc_ref[...] = jnp.zeros(acc_ref.shape, F32)

    def chunk(start, size, ml):
        out = []
        scs = [_mm(k_ref[0, pl.ds(start, size), k_cols[s]], q_streams[s]) for s in range(2)]
        for s in range(2):
            m_prev, l_prev = ml[2 * s], ml[2 * s + 1]
            m_new = jnp.maximum(m_prev, jnp.max(scs[s], axis=0, keepdims=True))
            alpha = jnp.exp2(m_prev - m_new)
            p = jnp.exp2(scs[s] - m_new)
            acc_ref[s] = alpha * acc_ref[s] + _mm(vt_ref[0, v_rows[s], pl.ds(start, size)], p.astype(MXU_DTYPE))
            out += [m_new, alpha * l_prev + jnp.sum(p, axis=0, keepdims=True)]
        return tuple(out)

    init = (jnp.full((1, tq), -jnp.inf, F32), jnp.zeros((1, tq), F32)) * 2
    is_ctx = pl.program_id(2) < L // tq
    big = KEY_CHUNK_BIG if T % KEY_CHUNK_BIG == 0 else KEY_CHUNK
    n_big = jnp.where(is_ctx, 0, T // big)
    ml = lax.fori_loop(0, n_big, lambda it, ml: chunk(pl.multiple_of(it * big, big), big, ml), init)
    n_small = jnp.where(is_ctx, L // KEY_CHUNK, 0)
    ml = lax.fori_loop(0, n_small, lambda it, ml: chunk(pl.multiple_of(it * KEY_CHUNK, KEY_CHUNK), KEY_CHUNK, ml), ml)
    ml_ref[...] = jnp.concatenate(ml, axis=0)


def _diff_attn_kernel(L, lam_init, q_ref, k_ref, vt_ref, lqk_ref, gn_ref, o_ref, ml_ref, acc_ref):
    q = q_ref[0]
    row = lax.broadcasted_iota(jnp.int32, q.shape, 0)
    zero = jnp.zeros_like(q)
    streams = (jnp.where(row < A_DQK, q, zero), jnp.where(row < A_DQK, zero, q))
    every = slice(None)
    _flash_streams(L, streams, (every, every), (every, every), k_ref, vt_ref, ml_ref, acc_ref)
    lqk = lqk_ref[...]
    lam = (jnp.exp(jnp.sum(lqk[0:1] * lqk[1:2], axis=-1, keepdims=True))
           - jnp.exp(jnp.sum(lqk[2:3] * lqk[3:4], axis=-1, keepdims=True)) + lam_init)
    o = acc_ref[0] / ml_ref[1:2, :] - lam * (acc_ref[1] / ml_ref[3:4, :])
    o = o * lax.rsqrt(jnp.mean(o * o, axis=0, keepdims=True) + EPS) * gn_ref[...]
    o_ref[0] = (o * (1.0 - lam_init)).astype(o_ref.dtype)


def _diff_attn(qt, k, vt, lqk, gn, L, lam_init):
    B, T, _ = k.shape
    tq = ROW_TILE
    assert L % tq == 0 and L % KEY_CHUNK == 0 and T % KEY_CHUNK == 0
    gn_b = jnp.broadcast_to(gn.reshape(A_DV, 1), (A_DV, tq))
    return pl.pallas_call(
        functools.partial(_diff_attn_kernel, L, lam_init),
        grid=(B, A_HEADS, T // tq),
        in_specs=[pl.BlockSpec((1, LANES, tq), lambda b, h, i: (b, h, i)),
                  pl.BlockSpec((1, T, LANES), lambda b, h, i: (b, 0, h)),
                  pl.BlockSpec((1, A_DV, T), lambda b, h, i: (b, h, 0)),
                  pl.BlockSpec(lqk.shape, lambda b, h, i: (0, 0)),
                  pl.BlockSpec(gn_b.shape, lambda b, h, i: (0, 0))],
        out_specs=pl.BlockSpec((1, A_DV, tq), lambda b, h, i: (b, h, i)),
        out_shape=jax.ShapeDtypeStruct((B, A_HEADS * A_DV, T), MXU_DTYPE),
        scratch_shapes=[pltpu.VMEM((4, tq), F32), pltpu.VMEM((2, A_DV, tq), F32)],
        compiler_params=_cparams("parallel", "parallel", "parallel"),
        name="diff_attn",
    )(qt, k, vt, lqk, gn_b)


def _mla_attn_kernel(L, q_ref, k_ref, vt_ref, o_ref, ml_ref, acc_ref):
    streams = (q_ref[0, :LANES, :], q_ref[0, LANES:, :])
    k_cols = (slice(0, LANES), slice(LANES, 2 * LANES))
    v_rows = (slice(0, C_DV), slice(C_DV, 2 * C_DV))
    _flash_streams(L, streams, k_cols, v_rows, k_ref, vt_ref, ml_ref, acc_ref)
    o_ref[0] = jnp.concatenate([acc_ref[0] / ml_ref[1:2, :], acc_ref[1] / ml_ref[3:4, :]], axis=0).astype(o_ref.dtype)


def _mla_attn(qt, k, vt, L):
    B, T, _ = k.shape
    tq = ROW_TILE
    assert L % tq == 0 and L % KEY_CHUNK == 0 and T % KEY_CHUNK == 0
    return pl.pallas_call(
        functools.partial(_mla_attn_kernel, L),
        grid=(B, C_HEADS // 2, T // tq),
        in_specs=[pl.BlockSpec((1, 2 * LANES, tq), lambda b, h, i: (b, h, i)),
                  pl.BlockSpec((1, T, 2 * LANES), lambda b, h, i: (b, 0, h)),
                  pl.BlockSpec((1, 2 * C_DV, T), lambda b, h, i: (b, h, 0))],
        out_specs=pl.BlockSpec((1, 2 * C_DV, tq), lambda b, h, i: (b, h, i)),
        out_shape=jax.ShapeDtypeStruct((B, C_HEADS * C_DV, T), MXU_DTYPE),
        scratch_shapes=[pltpu.VMEM((4, tq), F32), pltpu.VMEM((2, C_DV, tq), F32)],
        compiler_params=_cparams("parallel", "parallel", "parallel"),
        name="mla_attn",
    )(qt, k, vt)


def _win_attn_kernel(LT, sink_ref, q_ref, kc_ref, kp_ref, kk_ref, kn_ref, vc_ref, vp_ref, vk_ref, vn_ref, o_ref):
    i, nT = pl.program_id(1), pl.num_programs(1)
    tq = q_ref.shape[1]
    tk = kc_ref.shape[1]
    r = lax.broadcasted_iota(jnp.int32, (2 * tq, tk), 0)
    c = lax.broadcasted_iota(jnp.int32, (2 * tq, tk), 1)
    d = c - jnp.where(r >= tq, r - tq, r)
    is_x = i >= LT
    far = 4 * tk
    ok_p = d >= jnp.where(jnp.logical_and(is_x, i - 1 >= LT), tk - WINDOW, far)
    ok_k = jnp.abs(d) <= jnp.where(is_x, WINDOW, -1)
    ok_n = d <= jnp.where(jnp.logical_and(is_x, i + 1 <= nT - 1), WINDOW - tk, -far)
    k_tiles = (kc_ref[0], kp_ref[0], kk_ref[0], kn_ref[0])
    masks = (None, ok_p, ok_k, ok_n)
    vals = jnp.concatenate([vc_ref[0], vp_ref[0], vk_ref[0], vn_ref[0]], axis=0)
    lane = lax.broadcasted_iota(jnp.int32, (tq, LANES), 1)
    lo = lane < B_DH
    rows = lax.broadcasted_iota(jnp.int32, (2 * tq, 1), 0)
    group = B_HEADS // B_KV_HEADS
    for g in range(B_HEADS // 2):
        kv = (2 * g) // group
        qg = q_ref[0, :, g * LANES:(g + 1) * LANES].astype(F32)
        qr = pltpu.roll(qg, B_DH, 1)
        if kv == 0:
            qa, qb = jnp.where(lo, qg, 0.0), jnp.where(lo, qr, 0.0)
        else:
            qa, qb = jnp.where(lo, 0.0, qr), jnp.where(lo, 0.0, qg)
        qs = jnp.concatenate([qa, qb], axis=0).astype(MXU_DTYPE)
        s = jnp.concatenate([_nt(qs, kt) if ok is None else jnp.where(ok, _nt(qs, kt), NEG)
                             for kt, ok in zip(k_tiles, masks)], axis=1)
        sk = jnp.where(rows < tq, sink_ref[2 * g], sink_ref[2 * g + 1])
        m = jnp.maximum(jnp.max(s, axis=-1, keepdims=True), sk)
        p = jnp.exp(s - m)
        l = jnp.sum(p, axis=-1, keepdims=True) + jnp.exp(sk - m)
        acc = _mm(p.astype(MXU_DTYPE), vals) / l
        if kv == 0:
            out = jnp.where(lo, acc[:tq], pltpu.roll(acc[tq:], B_DH, 1))
        else:
            out = jnp.where(lo, pltpu.roll(acc[:tq], B_DH, 1), acc[tq:])
        o_ref[0, :, g * LANES:(g + 1) * LANES] = out.astype(o_ref.dtype)


def _win_attn(q, k, v, sink, L):
    B, T, _ = q.shape
    t = ROW_TILE
    assert L == t and WINDOW <= t
    nT, LT = T // t, L // t
    kspec = lambda f: pl.BlockSpec((1, t, LANES), f)
    maps = [lambda b, i: (b, 0, 0),
            lambda b, i: (b, jnp.maximum(i - 1, LT), 0),
            lambda b, i: (b, i, 0),
            lambda b, i: (b, jnp.minimum(i + 1, nT - 1), 0)]
    return pl.pallas_call(
        functools.partial(_win_attn_kernel, LT),
        grid=(B, nT),
        in_specs=[pl.BlockSpec(memory_space=pltpu.SMEM),
                  pl.BlockSpec((1, t, B_HEADS * B_DH), lambda b, i: (b, i, 0))]
                 + [kspec(f) for f in maps] + [kspec(f) for f in maps],
        out_specs=pl.BlockSpec((1, t, B_HEADS * B_DH), lambda b, i: (b, i, 0)),
        out_shape=jax.ShapeDtypeStruct((B, T, B_HEADS * B_DH), MXU_DTYPE),
        compiler_params=_cparams("parallel", "parallel"),
        name="window_attn",
    )(sink, q, k, k, k, k, v, v, v, v)


def _gla_kernel(reverse, q_ref, k_ref, v_ref, g_ref, o_ref, st_ref):
    @pl.when(pl.program_id(1) == 0)
    def _():
        st_ref[...] = jnp.zeros(st_ref.shape, F32)

    C = D_CHUNK
    tb = q_ref.shape[1]
    r = lax.broadcasted_iota(jnp.int32, (C, C), 0)
    c = lax.broadcasted_iota(jnp.int32, (C, C), 1)
    keep = (c >= r) if reverse else (c <= r)
    tri = jnp.where(keep, 1.0, 0.0).astype(F32)
    chunks = range(tb // C)
    for ch in (reversed(chunks) if reverse else chunks):
        rows = slice(ch * C, (ch + 1) * C)
        b = jnp.dot(tri, g_ref[0, rows, :], precision=lax.Precision.HIGHEST, preferred_element_type=F32)
        b_last = b[0:1] if reverse else b[C - 1:C]
        q_t = q_ref[0, rows, :] * jnp.exp(b)
        kk = k_ref[0, rows, :]
        k_in = kk * jnp.exp(-b)
        k_st = kk * jnp.exp(b_last - b)
        decay = jnp.exp(b_last)
        for h in range(D_HEADS):
            sl = slice(h * D_DK, (h + 1) * D_DK)
            vsl = slice(h * D_DV, (h + 1) * D_DV)
            qh = q_t[:, sl].astype(MXU_DTYPE)
            a = jnp.where(keep, _nt(qh, k_in[:, sl].astype(MXU_DTYPE)), 0.0)
            vh = v_ref[0, rows, vsl]
            st = st_ref[h]
            o_ref[0, rows, vsl] = _mm(a.astype(MXU_DTYPE), vh) + _nt(qh, st.astype(MXU_DTYPE))
            st_ref[h] = st * decay[:, sl] + _tn(vh, k_st[:, sl].astype(MXU_DTYPE))


def _gla(q, k, v, lg, direction, L):
    B, T, _ = q.shape
    tb = ROW_TILE
    nB, LB = T // tb, L // tb
    reverse = direction == 1
    if reverse:
        blk = lambda s: jnp.where(s < LB, LB - 1 - s, LB + nB - 1 - s)
    else:
        blk = lambda s: s
    return pl.pallas_call(
        functools.partial(_gla_kernel, reverse),
        grid=(B, nB),
        in_specs=[pl.BlockSpec((1, tb, D_HEADS * D_DK), lambda b, s: (b, blk(s), 0)),
                  pl.BlockSpec((1, tb, D_HEADS * D_DK), lambda b, s: (b, blk(s), 0)),
                  pl.BlockSpec((1, tb, D_HEADS * D_DV), lambda b, s: (b, blk(s), 0)),
                  pl.BlockSpec((1, tb, D_HEADS * D_DK), lambda b, s: (b, blk(s), direction))],
        out_specs=pl.BlockSpec((1, tb, D_HEADS * D_DV), lambda b, s: (b, blk(s), 0)),
        out_shape=jax.ShapeDtypeStruct((B, T, D_HEADS * D_DV), F32),
        scratch_shapes=[pltpu.VMEM((D_HEADS, D_DV, D_DK), F32)],
        compiler_params=_cparams("parallel", "arbitrary"),
        name="gla_bwd" if reverse else "gla_fwd",
    )(q, k, v, lg)


def _outproj_kernel(gla, *refs):
    it = iter(refs)
    xs_ref, o1_ref = next(it), next(it)
    if gla:
        of_ref, ob_ref, r_ref, dgn_ref = (next(it) for _ in range(4))
        og = of_ref[0] + ob_ref[0]
        rr = r_ref[0]
        parts = []
        for h in range(D_HEADS):
            sl = slice(h * D_DV, (h + 1) * D_DV)
            rh = rr[:, sl]
            parts.append((_rms(og[:, sl], dgn_ref[...]) * (rh * jax.nn.sigmoid(rh))).astype(MXU_DTYPE))
        o2 = jnp.concatenate(parts, axis=1)
    else:
        o2 = next(it)[0]
    w_ref, gate_ref, n2_ref, sh_ref, sc_ref, xo_ref, ht_ref = (next(it) for _ in range(7))
    half = o1_ref.shape[1]
    x = xs_ref[0] + gate_ref[...] * (_tn(o1_ref[0], w_ref[:half, :]) + _mm(o2, w_ref[half:, :]))
    xo_ref[0] = x
    h2 = _rms(x, n2_ref[...]) * (1.0 + sc_ref[...]) + sh_ref[...]
    ht_ref[...] = h2.T.astype(ht_ref.dtype)


def _outproj(xs, o1, o2s, w, gate, norm_g, shift, scale, LT, gla):
    B, T, D = xs.shape
    tm = ROW_TILE
    nT = T // tm
    row = lambda b, i: (b, i, 0)
    cls = lambda b, i: (b, jnp.where(i >= LT, 1, 0), 0, 0)
    full2 = lambda b, i: (0, 0)
    args = [xs, o1]
    specs = [pl.BlockSpec((1, tm, D), row), pl.BlockSpec((1, o1.shape[1], tm), lambda b, i: (b, 0, i))]
    for a in o2s:
        args.append(a)
        specs.append(pl.BlockSpec(a.shape, full2) if a.ndim == 2 else pl.BlockSpec((1, tm, a.shape[2]), row))
    args += [w, gate, norm_g, shift, scale]
    specs += [pl.BlockSpec(w.shape, full2), pl.BlockSpec((None, None, 1, D), cls), pl.BlockSpec((1, D), full2),
              pl.BlockSpec((None, None, 1, D), cls), pl.BlockSpec((None, None, 1, D), cls)]
    return pl.pallas_call(
        functools.partial(_outproj_kernel, gla),
        grid=(B, nT), in_specs=specs,
        out_specs=[pl.BlockSpec((1, tm, D), row), pl.BlockSpec((D, tm), lambda b, i: (0, b * nT + i))],
        out_shape=[jax.ShapeDtypeStruct((B, T, D), F32), jax.ShapeDtypeStruct((D, B * T), MXU_DTYPE)],
        compiler_params=_cparams("parallel", "parallel"),
        name="outproj_gla" if gla else "outproj",
    )(*args)


def _top_values(s, k, with_rank=False):
    vals = []
    rank = jnp.full(s.shape, float(k), F32) if with_rank else None
    for a in range(k):
        m = jnp.max(s, axis=0, keepdims=True)
        vals.append(m)
        hit = s == m
        if with_rank:
            rank = jnp.where(hit, float(a), rank)
        if a + 1 < k:
            s = jnp.where(hit, -jnp.inf, s)
    vals = jnp.concatenate(vals, axis=0)
    return (vals, rank) if with_rank else vals


def _peer_gate_kernel(ht_ref, wqt_ref, keys_ref, r2_ref, e2_ref, cnt_ref, cc_ref):
    qt = _mm(wqt_ref[...], ht_ref[...])
    n, k = P_NKEYS, P_TOPK
    s1 = _mm(keys_ref[0], qt[:n].astype(MXU_DTYPE))
    s2 = _mm(keys_ref[1], qt[n:].astype(MXU_DTYPE))
    sv1 = _top_values(s1, k)
    sv2, rank2 = _top_values(s2, k, with_rank=True)
    cand = jnp.concatenate([sv1[a:a + 1] + sv2[:k // (a + 1)] for a in range(k)], axis=0)
    tau = _top_values(cand, k)[k - 1:k]
    top = sv1[0:1] + sv2[0:1]
    z = jnp.sum(jnp.where(cand >= tau, jnp.exp(cand - top), 0.0), axis=0, keepdims=True)
    cnt = jnp.zeros(s1.shape, F32)
    for b in range(k):
        cnt = cnt + jnp.where(s1 + sv2[b:b + 1] >= tau, 1.0, 0.0)
    r2_ref[0] = rank2.astype(r2_ref.dtype)
    e2_ref[0] = jnp.exp(s2 - sv2[0:1]).astype(e2_ref.dtype)
    cnt_ref[0] = cnt
    cc_ref[0] = jnp.exp(s1 - sv1[0:1]) / z


def _peer_token_tile(n_tok):
    return 512 if n_tok % 512 == 0 else 256


def _peer_gate(ht, wqt, keys):
    D, N = ht.shape
    tm = _peer_token_tile(N)
    shape = lambda dt: jax.ShapeDtypeStruct((P_HEADS, P_NKEYS, N), dt)
    ospec = pl.BlockSpec((1, P_NKEYS, tm), lambda i, h: (h, 0, i))
    return pl.pallas_call(
        _peer_gate_kernel,
        grid=(N // tm, P_HEADS),
        in_specs=[pl.BlockSpec((D, tm), lambda i, h: (0, i)),
                  pl.BlockSpec((P_DKEY, D), lambda i, h: (h, 0)),
                  pl.BlockSpec((2, P_NKEYS, P_DKEY // 2), lambda i, h: (h, 0, 0))],
        out_specs=[ospec] * 4,
        out_shape=[shape(GATE_DTYPE), shape(GATE_DTYPE), shape(F32), shape(F32)],
        compiler_params=_cparams("parallel", "arbitrary"),
        name="peer_gate",
    )(ht, wqt, keys)


def _rows_packed(row, n):
    tile = jnp.broadcast_to(row, (PACKED_ROWS, row.shape[1])).astype(GATE_DTYPE)
    return jnp.concatenate([tile] * (n // PACKED_ROWS), axis=0)


def _peer_dense_kernel(ht_ref, r2_ref, e2_ref, cnt_ref, cc_ref, u_ref, vt_ref, o_ref, acc_ref):
    j = pl.program_id(1)

    @pl.when(j == 0)
    def _():
        acc_ref[...] = jnp.zeros(acc_ref.shape, F32)

    n = P_NKEYS
    zero = jnp.zeros((n, ht_ref.shape[1]), GATE_DTYPE)
    ht = ht_ref[...]
    total = None
    nsub = u_ref.shape[0] // PEER_SUB
    scores = lambda k: _mm(u_ref[k * PEER_SUB:(k + 1) * PEER_SUB, :], ht)
    zt_next = scores(0)
    for k in range(nsub):
        zt = zt_next
        if k + 1 < nsub:
            zt_next = scores(k + 1)
        rows = []
        for ii in range(PEER_SUB // n):
            i1 = k * (PEER_SUB // n) + ii
            g = None
            for h in range(P_HEADS):
                sel = r2_ref[h] < _rows_packed(cnt_ref[h, i1:i1 + 1, :], n)
                t = jnp.where(sel, e2_ref[h], zero) * _rows_packed(cc_ref[h, i1:i1 + 1, :], n)
                g = t if g is None else g + t
            z = zt[ii * n:(ii + 1) * n]
            gelu = 0.5 * z * (1.0 + lax.erf(z * math.sqrt(0.5)))
            rows.append((gelu.astype(GATE_DTYPE) * g).astype(MXU_DTYPE))
        part = _mm(vt_ref[:, k * PEER_SUB:(k + 1) * PEER_SUB], jnp.concatenate(rows, axis=0))
        total = part if total is None else total + part
    acc_ref[...] += total

    @pl.when(j == pl.num_programs(1) - 1)
    def _():
        o_ref[...] = acc_ref[...].T


def _peer_dense(ht, r2, e2, cnt, cc, u, vt):
    D, N = ht.shape
    E = u.shape[0]
    tm = _peer_token_tile(N)
    te = 8 * P_NKEYS
    gspec = pl.BlockSpec((P_HEADS, P_NKEYS, tm), lambda i, j: (0, 0, i))
    rspec = pl.BlockSpec((P_HEADS, te // P_NKEYS, tm), lambda i, j: (0, j, i))
    return pl.pallas_call(
        _peer_dense_kernel,
        grid=(N // tm, E // te),
        in_specs=[pl.BlockSpec((D, tm), lambda i, j: (0, i)), gspec, gspec, rspec, rspec,
                  pl.BlockSpec((te, D), lambda i, j: (j, 0)),
                  pl.BlockSpec((D, te), lambda i, j: (0, j))],
        out_specs=pl.BlockSpec((tm, D), lambda i, j: (i, 0)),
        out_shape=jax.ShapeDtypeStruct((N, D), F32),
        scratch_shapes=[pltpu.VMEM((D, tm), F32)],
        compiler_params=_cparams("parallel", "arbitrary"),
        name="peer_dense",
    )(ht, r2, e2, cnt, cc, u, vt)


def _final_kernel(xs_ref, pp_ref, g_ref, o_ref):
    o_ref[0] = xs_ref[0] + g_ref[...] * pp_ref[0]


def _final(xs, pp, gate, L):
    B, T, D = xs.shape
    tm = ROW_TILE
    LT = L // tm
    src = lambda b, i: (b, i + LT, 0)
    return pl.pallas_call(
        _final_kernel,
        grid=(B, (T - L) // tm),
        in_specs=[pl.BlockSpec((1, tm, D), src), pl.BlockSpec((1, tm, D), src),
                  pl.BlockSpec((None, None, 1, D), lambda b, i: (b, 1, 0, 0))],
        out_specs=pl.BlockSpec((1, tm, D), lambda b, i: (b, i, 0)),
        out_shape=jax.ShapeDtypeStruct((B, T - L, D), F32),
        compiler_params=_cparams("parallel", "parallel"),
        name="final_residual",
    )(xs, pp, gate)


def _even_odd(n):
    return np.concatenate([np.arange(0, n, 2), np.arange(1, n, 2)])


def _rope_angles(S, L, rot_dim):
    pos = jnp.arange(S)
    r = (pos // GRID_W).astype(F32)
    cc = (pos % GRID_W).astype(F32)
    axis_dim = rot_dim // 2
    inv = 1.0 / (ROPE_THETA ** (jnp.arange(0, axis_dim, 2, dtype=F32) / axis_dim))
    ang = jnp.concatenate([r[:, None] * inv, cc[:, None] * inv], axis=-1)
    cos = jnp.concatenate([jnp.ones((L, rot_dim // 2), F32), jnp.cos(ang)], axis=0)
    sin = jnp.concatenate([jnp.zeros((L, rot_dim // 2), F32), jnp.sin(ang)], axis=0)
    return cos, sin


def _rope_tables_64(S, L):
    cos, sin = _rope_angles(S, L, 64)
    return jnp.tile(jnp.concatenate([cos, cos], -1), (1, 2)), jnp.tile(jnp.concatenate([-sin, sin], -1), (1, 2))


def _rope_tables_mla(S, L):
    cos, sin = _rope_angles(S, L, C_ROPE)
    T = cos.shape[0]
    one, zero = jnp.ones((T, C_NOPE), F32), jnp.zeros((T, C_NOPE), F32)
    pad1, pad0 = jnp.ones((T, LANES - C_NOPE - C_ROPE), F32), jnp.zeros((T, LANES - C_NOPE - C_ROPE), F32)
    return (jnp.concatenate([one, cos, cos, pad1], -1), jnp.concatenate([zero, -sin, sin, pad0], -1))


def _prep_even(w_in, a_qn, a_kn, b_qn, b_kn):
    sizes = (512, 512, 512, 512, 128, 128)
    off = np.concatenate([[0], np.cumsum(sizes)])
    p64 = _even_odd(64)

    def seg(k, permute):
        idx = np.arange(off[k], off[k + 1])
        if permute:
            idx = idx.reshape(-1, 64)[:, p64].reshape(-1)
        return idx

    cols = np.concatenate([seg(0, True), seg(1, True), seg(3, True), seg(4, True), seg(2, False), seg(5, False)])
    w = w_in[:, cols].astype(MXU_DTYPE)
    q_scale = A_DQK ** -0.5 * LOG2_E
    gain = jnp.concatenate([jnp.tile(a_qn[p64], 8) * q_scale, jnp.tile(a_kn[p64], 8),
                            jnp.tile(b_qn[p64], 8) * (B_DH ** -0.5), jnp.tile(b_kn[p64], 2)])[None, :]
    return w, gain


def _pad_cols(a, n):
    return jnp.concatenate([a, jnp.zeros(a.shape[:-1] + (n - a.shape[-1],), a.dtype)], axis=-1)


def _prep_odd(w_in, c_wq_up, c_wkv_up, c_qn, c_kn, d_wg_f, d_bg_f, d_wg_b, d_bg_b):
    D = w_in.shape[0]
    p32 = _even_odd(C_ROPE)
    o = np.concatenate([[0], np.cumsum((C_Q_RANK, C_KV_RANK, C_ROPE, 256, 256, 512, D_GATE_RANK, D_GATE_RANK, 512))])
    seg = lambda k: w_in[:, o[k]:o[k + 1]]
    w = jnp.concatenate([seg(0), seg(1), _pad_cols(seg(2)[:, p32], LANES), seg(3), seg(4), seg(5),
                         _pad_cols(jnp.concatenate([seg(6), seg(7)], -1), LANES), seg(8)], axis=-1).astype(MXU_DTYPE)
    hd = C_NOPE + C_ROPE
    head_perm = np.concatenate([np.arange(C_NOPE), C_NOPE + p32])
    wq = c_wq_up.reshape(C_Q_RANK, C_HEADS, hd)[:, :, head_perm]
    wq = _pad_cols(wq, LANES).reshape(C_Q_RANK, C_HEADS * LANES).astype(MXU_DTYPE)
    wkv = c_wkv_up.reshape(C_KV_RANK, C_HEADS, C_NOPE + C_DV)
    wk = _pad_cols(wkv[:, :, :C_NOPE], LANES).reshape(C_KV_RANK, C_HEADS * LANES)
    wv = wkv[:, :, C_NOPE:].reshape(C_KV_RANK, C_HEADS * C_DV)
    wkv = jnp.concatenate([wk, wv], axis=-1).astype(MXU_DTYPE)
    qn = jnp.tile(_pad_cols(c_qn[head_perm] * (hd ** -0.5 * LOG2_E), LANES), C_HEADS)[None, :]
    kn = jnp.tile(_pad_cols(c_kn[head_perm], LANES), C_HEADS)[None, :]
    n = D_HEADS * D_DK
    wg = jnp.zeros((LANES, 2 * n), F32)
    wg = wg.at[:D_GATE_RANK, :n].set(d_wg_f).at[D_GATE_RANK:2 * D_GATE_RANK, n:].set(d_wg_b).astype(MXU_DTYPE)
    bg = jnp.concatenate([d_bg_f, d_bg_b])[None, :]
    return w, wq, wkv, qn, kn, wg, bg


def kernel(x, c, ctx, c_ctx, w_mod, b_mod, norm1, norm2, e_w_in, e_w_out, a_qn, a_kn, a_lq1, a_lk1, a_lq2, a_lk2, a_gn, b_qn, b_kn, b_sink, o_w_in, o_w_out, c_gq, c_gkv, c_wq_up, c_wkv_up, c_qn, c_kn, d_wg_f, d_bg_f, d_wg_b, d_bg_b, d_gn, p_wq, p_keys, p_u, p_v):
    B, S, D = x.shape
    L = ctx.shape[1]
    depth = w_mod.shape[0]
    assert L == ROW_TILE and S % ROW_TILE == 0 and S % GRID_W == 0
    LT = L // ROW_TILE
    bf = MXU_DTYPE

    cv = jnp.concatenate([c, c_ctx[None, :], jnp.zeros((8 - B - 1, D), F32)], axis=0)
    mods = _modulation(cv, w_mod, b_mod).reshape(depth, 8, 6, D)
    mod_tab = jnp.stack([jnp.broadcast_to(mods[:, B][:, None], (depth, B, 6, D)), mods[:, :B]], axis=2)
    mod = lambda i, k: mod_tab[i, :, :, k][:, :, None, :]

    cos64, sin64 = _rope_tables_64(S, L)
    cos_c, sin_c = _rope_tables_mla(S, L)

    xs = jnp.concatenate([ctx, x], axis=1)
    prev = None
    for i in range(depth):
        j = i // 2
        if i % 2 == 0:
            lam_init = 0.8 - 0.6 * math.exp(-0.3 * i)
            w, gain = _prep_even(e_w_in[j], a_qn[j], a_kn[j], b_qn[j], b_kn[j])
            xs, (aq, ak, bq, bk, av, bv) = _inproj(
                "even", xs, prev, norm1[i][None], mod(i, 0), mod(i, 1), w,
                [(gain, "full"), (cos64, "rows"), (sin64, "rows")],
                [(512, bf, True), (512, bf, False), (512, bf, False), (128, bf, False), (512, bf, True),
                 (128, bf, False)], LT)
            lqk = jnp.stack([a_lq1[j], a_lk1[j], a_lq2[j], a_lk2[j]])
            o1 = _diff_attn(aq, ak, av, lqk, a_gn[j][None], L, lam_init)
            o2 = _win_attn(bq, bk, bv, b_sink[j], L)
            xs, ht = _outproj(xs, o1, [o2], e_w_out[j].astype(bf), mod(i, 2), norm2[i][None], mod(i, 3), mod(i, 4),
                              LT, gla=False)
        else:
            w, wq, wkv, qn, kn, wg, bg = _prep_odd(o_w_in[j], c_wq_up[j], c_wkv_up[j], c_qn[j], c_kn[j],
                                                   d_wg_f[j], d_bg_f[j], d_wg_b[j], d_bg_b[j])
            xs, (cq, ck, cvv, dq, dk, dv, lg, r) = _inproj(
                "odd", xs, prev, norm1[i][None], mod(i, 0), mod(i, 1), w,
                [(c_gq[j][None], "full"), (c_gkv[j][None], "full"), (wq, "full"), (wkv, "full"), (qn, "full"),
                 (kn, "full"), (wg, "full"), (bg, "full"), (cos_c, "rows"), (sin_c, "rows")],
                [(1024, bf, True), (1024, bf, False), (512, bf, True), (256, F32, False), (256, F32, False),
                 (512, bf, False), (512, F32, False), (512, F32, False)], LT)
            o1 = _mla_attn(cq, ck, cvv, L)
            of = _gla(dq, dk, dv, lg, 0, L)
            ob = _gla(dq, dk, dv, lg, 1, L)
            xs, ht = _outproj(xs, o1, [of, ob, r, d_gn[j][None]], o_w_out[j].astype(bf), mod(i, 2), norm2[i][None],
                              mod(i, 3), mod(i, 4), LT, gla=True)
        wqt = p_wq[i].T.astype(bf)
        keys = p_keys[i].reshape(2 * P_HEADS, P_NKEYS, P_DKEY // 2).astype(bf)
        r2, e2, cnt, cc = _peer_gate(ht, wqt, keys)
        pp = _peer_dense(ht, r2, e2, cnt, cc, p_u[i].astype(bf), p_v[i].T.astype(bf))
        prev = (pp.reshape(B, L + S, D), mod(i, 5))
    return _final(xs, prev[0], prev[1], L)
```

```python
import functools
import math

import numpy as np
import jax
import jax.numpy as jnp
from jax import lax
from jax.experimental import pallas as pl
from jax.experimental.pallas import tpu as pltpu

F32 = jnp.float32
MXU_DTYPE = jnp.bfloat16
GATE_DTYPE = jnp.bfloat16
PACKED_ROWS = 16
PEER_SUB = 256

GRID_W = 64
ROPE_THETA = 10000.0
EPS = 1e-6
NEG = -1e30
A_HEADS, A_DQK, A_DV = 4, 64, 128
B_HEADS, B_KV_HEADS, B_DH, WINDOW = 8, 2, 64, 128
C_HEADS, C_Q_RANK, C_KV_RANK, C_NOPE, C_ROPE, C_DV = 8, 256, 128, 64, 32, 64
D_HEADS, D_DK, D_DV, D_GATE_RANK, D_GATE_TAU, D_CHUNK = 4, 64, 128, 16, 16.0, 64
P_HEADS, P_NKEYS, P_DKEY, P_TOPK = 8, 128, 256, 16

LANES = 128
ROW_TILE = 256
KEY_CHUNK = 256
KEY_CHUNK_BIG = 768
LOG2_E = math.log2(math.e)
VMEM_LIMIT = 56 * 1024 * 1024


def _cparams(*sem):
    return pltpu.CompilerParams(dimension_semantics=sem, vmem_limit_bytes=VMEM_LIMIT)


def _rms(x, g):
    return x * lax.rsqrt(jnp.mean(x * x, axis=-1, keepdims=True) + EPS) * g


def _nt(a, b):
    return lax.dot_general(a, b, (((1,), (1,)), ((), ())), preferred_element_type=F32)


def _tn(a, b):
    return lax.dot_general(a, b, (((0,), (0,)), ((), ())), preferred_element_type=F32)


def _mm(a, b):
    return jnp.dot(a, b, preferred_element_type=F32)


def _mod_kernel(c_ref, w_ref, b_ref, o_ref):
    cv = c_ref[...]
    s = cv * jax.nn.sigmoid(cv)
    o_ref[0] = jnp.dot(s, w_ref[0], precision=lax.Precision.HIGHEST, preferred_element_type=F32) + b_ref[0]


def _modulation(cvecs, w_mod, b_mod):
    depth, d, n = w_mod.shape
    rows = cvecs.shape[0]
    tn = 1536
    return pl.pallas_call(
        _mod_kernel,
        grid=(depth, n // tn),
        in_specs=[pl.BlockSpec((rows, d), lambda l, j: (0, 0)),
                  pl.BlockSpec((1, d, tn), lambda l, j: (l, 0, j)),
                  pl.BlockSpec((1, 1, tn), lambda l, j: (l, 0, j))],
        out_specs=pl.BlockSpec((1, rows, tn), lambda l, j: (l, 0, j)),
        out_shape=jax.ShapeDtypeStruct((depth, rows, n), F32),
        compiler_params=_cparams("parallel", "parallel"),
        name="modulation",
    )(cvecs, w_mod, b_mod.reshape(depth, 1, n))


def _stream_prologue(has_prev, it):
    xs_ref = next(it)
    x = xs_ref[0]
    if has_prev:
        pp_ref, pg_ref = next(it), next(it)
        x = x + pg_ref[...] * pp_ref[0]
    g1_ref, sh_ref, sc_ref, w_ref = next(it), next(it), next(it), next(it)
    h = _rms(x, g1_ref[...]) * (1.0 + sc_ref[...]) + sh_ref[...]
    z = _mm(h.astype(MXU_DTYPE), w_ref[...])
    return x, z


def _norm_rope_pair64(zg, gain, cosf, sinf):
    lane = lax.broadcasted_iota(jnp.int32, zg.shape, 1)
    lo = lane < 64
    sq = zg * zg
    s_lo = jnp.sum(jnp.where(lo, sq, 0.0), axis=-1, keepdims=True)
    s_hi = jnp.sum(jnp.where(lo, 0.0, sq), axis=-1, keepdims=True)
    ms = jnp.where(lo, s_lo, s_hi) * (1.0 / 64.0)
    y = zg * lax.rsqrt(ms + EPS) * gain
    partner = jnp.where((lane & 63) < 32, pltpu.roll(y, 96, 1), pltpu.roll(y, 32, 1))
    return y * cosf + partner * sinf


def _inproj_even_kernel(has_prev, *refs):
    it = iter(refs)
    x, z = _stream_prologue(has_prev, it)
    gain_ref, cos_ref, sin_ref = next(it), next(it), next(it)
    if has_prev:
        xo_ref = next(it)
        xo_ref[0] = x
    aq_ref, ak_ref, bq_ref, bk_ref, av_ref, bv_ref = (next(it) for _ in range(6))
    cosf, sinf = cos_ref[...], sin_ref[...]
    g = 0
    for ref, ngroups, transposed in ((aq_ref, 4, True), (ak_ref, 4, False), (bq_ref, 4, False), (bk_ref, 1, False)):
        for k in range(ngroups):
            sl = slice(g * LANES, (g + 1) * LANES)
            y = _norm_rope_pair64(z[:, sl], gain_ref[:, sl], cosf, sinf)
            if transposed:
                ref[0, k * LANES:(k + 1) * LANES, :] = y.T.astype(ref.dtype)
            else:
                ref[0, :, k * LANES:(k + 1) * LANES] = y.astype(ref.dtype)
            g += 1
    o = g * LANES
    av_ref[0] = z[:, o:o + 512].T.astype(av_ref.dtype)
    bv_ref[0] = z[:, o + 512:o + 640].astype(bv_ref.dtype)


def _norm_rope_mla(y, gain, cosf, sinf):
    lane = lax.broadcasted_iota(jnp.int32, y.shape, 1)
    ms = jnp.sum(y * y, axis=-1, keepdims=True) * (1.0 / (C_NOPE + C_ROPE))
    y = y * lax.rsqrt(ms + EPS) * gain
    partner = jnp.where(lane < C_NOPE + C_ROPE // 2, pltpu.roll(y, LANES - C_ROPE // 2, 1),
                        pltpu.roll(y, C_ROPE // 2, 1))
    return y * cosf + partner * sinf


def _log_sigmoid(x):
    return jnp.minimum(x, 0.0) - jnp.log1p(jnp.exp(-jnp.abs(x)))


def _inproj_odd_kernel(has_prev, *refs):
    it = iter(refs)
    x, z = _stream_prologue(has_prev, it)
    (gq_ref, gkv_ref, wq_ref, wkv_ref, qn_ref, kn_ref, wg_ref, bg_ref, cos_ref, sin_ref) = (next(it) for _ in range(10))
    if has_prev:
        xo_ref = next(it)
        xo_ref[0] = x
    cq_ref, ck_ref, cv_ref, dq_ref, dk_ref, dv_ref, lg_ref, r_ref = (next(it) for _ in range(8))
    cosf, sinf = cos_ref[...], sin_ref[...]
    qn = _rms(z[:, 0:256], gq_ref[...])
    q_up = _mm(qn.astype(MXU_DTYPE), wq_ref[...])
    kvn = _rms(z[:, 256:384], gkv_ref[...])
    kv_up = _mm(kvn.astype(MXU_DTYPE), wkv_ref[...])
    kpe = pltpu.roll(z[:, 384:512], C_NOPE, 1)
    for h in range(C_HEADS):
        sl = slice(h * LANES, (h + 1) * LANES)
        cq_ref[0, sl, :] = _norm_rope_mla(q_up[:, sl], qn_ref[:, sl], cosf, sinf).T.astype(cq_ref.dtype)
        ck_ref[0, :, sl] = _norm_rope_mla(kv_up[:, sl] + kpe, kn_ref[:, sl], cosf, sinf).astype(ck_ref.dtype)
    cv_ref[0] = kv_up[:, C_HEADS * LANES:].T.astype(cv_ref.dtype)
    dq_ref[0] = z[:, 512:768] * (D_DK ** -0.5)
    dk_ref[0] = z[:, 768:1024]
    dv_ref[0] = z[:, 1024:1536].astype(dv_ref.dtype)
    gl = _mm(z[:, 1536:1664].astype(MXU_DTYPE), wg_ref[...]) + bg_ref[...]
    lg_ref[0] = _log_sigmoid(gl) * (1.0 / D_GATE_TAU)
    r_ref[0] = z[:, 1664:2176]


def _inproj(kind, xs, prev, norm_g, shift, scale, w, extra, out_cols, LT):
    B, T, D = xs.shape
    tm = ROW_TILE
    nT = T // tm
    has_prev = prev is not None
    row = lambda b, i: (b, i, 0)
    cls = lambda b, i: (b, jnp.where(i >= LT, 1, 0), 0, 0)
    full2 = lambda b, i: (0, 0)
    args, specs = [xs], [pl.BlockSpec((1, tm, D), row)]
    if has_prev:
        args += [prev[0], prev[1]]
        specs += [pl.BlockSpec((1, tm, D), row), pl.BlockSpec((None, None, 1, D), cls)]
    args += [norm_g, shift, scale, w]
    specs += [pl.BlockSpec((1, D), full2), pl.BlockSpec((None, None, 1, D), cls),
              pl.BlockSpec((None, None, 1, D), cls), pl.BlockSpec(w.shape, full2)]
    for arr, k in extra:
        args.append(arr)
        if k == "full":
            specs.append(pl.BlockSpec(arr.shape, full2))
        else:
            specs.append(pl.BlockSpec((tm, arr.shape[1]), lambda b, i: (i, 0)))
    out_shape, out_specs = [], []
    if has_prev:
        out_shape.append(jax.ShapeDtypeStruct((B, T, D), F32))
        out_specs.append(pl.BlockSpec((1, tm, D), row))
    for n, dt, transposed in out_cols:
        if transposed:
            out_shape.append(jax.ShapeDtypeStruct((B, n, T), dt))
            out_specs.append(pl.BlockSpec((1, n, tm), lambda b, i: (b, 0, i)))
        else:
            out_shape.append(jax.ShapeDtypeStruct((B, T, n), dt))
            out_specs.append(pl.BlockSpec((1, tm, n), row))
    body = _inproj_even_kernel if kind == "even" else _inproj_odd_kernel
    outs = pl.pallas_call(
        functools.partial(body, has_prev),
        grid=(B, nT), in_specs=specs, out_specs=out_specs, out_shape=out_shape,
        compiler_params=_cparams("parallel", "parallel"),
        name="inproj_" + kind,
    )(*args)
    if has_prev:
        return outs[0], outs[1:]
    return xs, outs


def _flash_streams(L, q_streams, k_cols, v_rows, k_ref, vt_ref, ml_ref, acc_ref, sc_ref, p_ref):
    T = k_ref.shape[1]
    tq = q_streams[0].shape[1]
    acc_ref[...] = jnp.zeros(acc_ref.shape, F32)

    def chunk(start, size, ml):
        out = []
        scs = [_mm(k_ref[0, pl.ds(start, size), k_cols[s]], q_streams[s]) for s in range(2)]
        for s in range(2):
            m_prev, l_prev = ml[2 * s], ml[2 * s + 1]
            m_new = jnp.maximum(m_prev, jnp.max(scs[s], axis=0, keepdims=True))
            alpha = jnp.exp2(m_prev - m_new)
            p = jnp.exp2(scs[s] - m_new)
            acc_ref[s] = alpha * acc_ref[s] + _mm(vt_ref[0, v_rows[s], pl.ds(start, size)], p.astype(MXU_DTYPE))
            out += [m_new, alpha * l_prev + jnp.sum(p, axis=0, keepdims=True)]
        return tuple(out)

    init = (jnp.full((1, tq), -jnp.inf, F32), jnp.zeros((1, tq), F32)) * 2
    is_ctx = pl.program_id(2) < L // tq
    big = sc_ref.shape[1]
    piece = KEY_CHUNK
    pieces = big // piece
    n_chunks = T // big

    def piece_scores(c, j, s):
        start = pl.multiple_of(c * big + j * piece, piece)
        return _mm(k_ref[0, pl.ds(start, piece), k_cols[s]], q_streams[s])

    def col_max(parts):
        return functools.reduce(jnp.maximum, [jnp.max(x, axis=0, keepdims=True) for x in parts])

    def value_product(c, s):
        start = pl.multiple_of(c * big, big)
        return _mm(vt_ref[0, v_rows[s], pl.ds(start, big)], p_ref[s])

    def pipelined(t, carry):
        ml, mx, alpha_prev = carry
        nxt = jnp.minimum(t + 1, n_chunks - 1)
        pv = [value_product(jnp.maximum(t - 1, 0), s) for s in range(2)]
        m_new = [jnp.maximum(ml[2 * s], mx[s]) for s in range(2)]
        alpha = [jnp.exp2(ml[2 * s] - m_new[s]) for s in range(2)]
        ahead, sums = ([], []), ([], [])
        for j in range(pieces):
            rows = slice(j * piece, (j + 1) * piece)
            for s in range(2):
                ahead[s].append(piece_scores(nxt, j, s))
            for s in range(2):
                p = jnp.exp2(sc_ref[s, rows, :] - m_new[s])
                sums[s].append(jnp.sum(p, axis=0, keepdims=True))
                p_ref[s, rows, :] = p.astype(p_ref.dtype)
            for s in range(2):
                sc_ref[s, rows, :] = ahead[s][j]
        new_ml = []
        for s in range(2):
            acc_ref[s] = alpha_prev[s] * acc_ref[s] + pv[s]
            new_ml += [m_new[s], alpha[s] * ml[2 * s + 1] + functools.reduce(jnp.add, sums[s])]
        return tuple(new_ml), tuple(col_max(ahead[s]) for s in range(2)), tuple(alpha)

    first = []
    for s in range(2):
        parts = [piece_scores(0, j, s) for j in range(pieces)]
        for j in range(pieces):
            sc_ref[s, j * piece:(j + 1) * piece, :] = parts[j]
        first.append(col_max(parts))
    p_ref[...] = jnp.zeros(p_ref.shape, p_ref.dtype)
    one = jnp.ones((1, tq), F32)
    n_big = jnp.where(is_ctx, 0, n_chunks)
    ml, _, alpha_last = lax.fori_loop(0, n_big, pipelined, (init, tuple(first), (one, one)))

    @pl.when(jnp.logical_not(is_ctx))
    def _():
        for s in range(2):
            acc_ref[s] = alpha_last[s] * acc_ref[s] + value_product(n_chunks - 1, s)

    n_small = jnp.where(is_ctx, L // KEY_CHUNK, 0)
    ml = lax.fori_loop(0, n_small, lambda it, ml: chunk(pl.multiple_of(it * KEY_CHUNK, KEY_CHUNK), KEY_CHUNK, ml), ml)
    ml_ref[...] = jnp.concatenate(ml, axis=0)


def _big_chunk(T):
    return KEY_CHUNK_BIG if T % KEY_CHUNK_BIG == 0 else KEY_CHUNK


def _chunk_scratch(T, tq):
    return pltpu.VMEM((2, _big_chunk(T), tq), F32), pltpu.VMEM((2, _big_chunk(T), tq), MXU_DTYPE)


def _diff_attn_kernel(L, lam_init, q_ref, k_ref, vt_ref, lqk_ref, gn_ref, o_ref, ml_ref, acc_ref, sc_ref, p_ref):
    q = q_ref[0]
    row = lax.broadcasted_iota(jnp.int32, q.shape, 0)
    zero = jnp.zeros_like(q)
    streams = (jnp.where(row < A_DQK, q, zero), jnp.where(row < A_DQK, zero, q))
    every = slice(None)
    _flash_streams(L, streams, (every, every), (every, every), k_ref, vt_ref, ml_ref, acc_ref, sc_ref, p_ref)
    lqk = lqk_ref[...]
    lam = (jnp.exp(jnp.sum(lqk[0:1] * lqk[1:2], axis=-1, keepdims=True))
           - jnp.exp(jnp.sum(lqk[2:3] * lqk[3:4], axis=-1, keepdims=True)) + lam_init)
    o = acc_ref[0] / ml_ref[1:2, :] - lam * (acc_ref[1] / ml_ref[3:4, :])
    o = o * lax.rsqrt(jnp.mean(o * o, axis=0, keepdims=True) + EPS) * gn_ref[...]
    o_ref[0] = (o * (1.0 - lam_init)).astype(o_ref.dtype)


def _diff_attn(qt, k, vt, lqk, gn, L, lam_init):
    B, T, _ = k.shape
    tq = ROW_TILE
    assert L % tq == 0 and L % KEY_CHUNK == 0 and T % KEY_CHUNK == 0
    gn_b = jnp.broadcast_to(gn.reshape(A_DV, 1), (A_DV, tq))
    return pl.pallas_call(
        functools.partial(_diff_attn_kernel, L, lam_init),
        grid=(B, A_HEADS, T // tq),
        in_specs=[pl.BlockSpec((1, LANES, tq), lambda b, h, i: (b, h, i)),
                  pl.BlockSpec((1, T, LANES), lambda b, h, i: (b, 0, h)),
                  pl.BlockSpec((1, A_DV, T), lambda b, h, i: (b, h, 0)),
                  pl.BlockSpec(lqk.shape, lambda b, h, i: (0, 0)),
                  pl.BlockSpec(gn_b.shape, lambda b, h, i: (0, 0))],
        out_specs=pl.BlockSpec((1, A_DV, tq), lambda b, h, i: (b, h, i)),
        out_shape=jax.ShapeDtypeStruct((B, A_HEADS * A_DV, T), MXU_DTYPE),
        scratch_shapes=[pltpu.VMEM((4, tq), F32), pltpu.VMEM((2, A_DV, tq), F32), *_chunk_scratch(T, tq)],
        compiler_params=_cparams("parallel", "parallel", "parallel"),
        name="diff_attn",
    )(qt, k, vt, lqk, gn_b)


def _mla_attn_kernel(L, q_ref, k_ref, vt_ref, o_ref, ml_ref, acc_ref, sc_ref, p_ref):
    streams = (q_ref[0, :LANES, :], q_ref[0, LANES:, :])
    k_cols = (slice(0, LANES), slice(LANES, 2 * LANES))
    v_rows = (slice(0, C_DV), slice(C_DV, 2 * C_DV))
    _flash_streams(L, streams, k_cols, v_rows, k_ref, vt_ref, ml_ref, acc_ref, sc_ref, p_ref)
    o_ref[0] = jnp.concatenate([acc_ref[0] / ml_ref[1:2, :], acc_ref[1] / ml_ref[3:4, :]], axis=0).astype(o_ref.dtype)


def _mla_attn(qt, k, vt, L):
    B, T, _ = k.shape
    tq = ROW_TILE
    assert L % tq == 0 and L % KEY_CHUNK == 0 and T % KEY_CHUNK == 0
    return pl.pallas_call(
        functools.partial(_mla_attn_kernel, L),
        grid=(B, C_HEADS // 2, T // tq),
        in_specs=[pl.BlockSpec((1, 2 * LANES, tq), lambda b, h, i: (b, h, i)),
                  pl.BlockSpec((1, T, 2 * LANES), lambda b, h, i: (b, 0, h)),
                  pl.BlockSpec((1, 2 * C_DV, T), lambda b, h, i: (b, h, 0))],
        out_specs=pl.BlockSpec((1, 2 * C_DV, tq), lambda b, h, i: (b, h, i)),
        out_shape=jax.ShapeDtypeStruct((B, C_HEADS * C_DV, T), MXU_DTYPE),
        scratch_shapes=[pltpu.VMEM((4, tq), F32), pltpu.VMEM((2, C_DV, tq), F32), *_chunk_scratch(T, tq)],
        compiler_params=_cparams("parallel", "parallel", "parallel"),
        name="mla_attn",
    )(qt, k, vt)


def _win_attn_kernel(LT, sink_ref, q_ref, kc_ref, kp_ref, kk_ref, kn_ref, vc_ref, vp_ref, vk_ref, vn_ref, o_ref):
    i, nT = pl.program_id(1), pl.num_programs(1)
    tq = q_ref.shape[1]
    tk = kc_ref.shape[1]
    r = lax.broadcasted_iota(jnp.int32, (2 * tq, tk), 0)
    c = lax.broadcasted_iota(jnp.int32, (2 * tq, tk), 1)
    d = c - jnp.where(r >= tq, r - tq, r)
    is_x = i >= LT
    far = 4 * tk
    ok_p = d >= jnp.where(jnp.logical_and(is_x, i - 1 >= LT), tk - WINDOW, far)
    ok_k = jnp.abs(d) <= jnp.where(is_x, WINDOW, -1)
    ok_n = d <= jnp.where(jnp.logical_and(is_x, i + 1 <= nT - 1), WINDOW - tk, -far)
    k_tiles = (kc_ref[0], kp_ref[0], kk_ref[0], kn_ref[0])
    masks = (None, ok_p, ok_k, ok_n)
    vals = jnp.concatenate([vc_ref[0], vp_ref[0], vk_ref[0], vn_ref[0]], axis=0)
    lane = lax.broadcasted_iota(jnp.int32, (tq, LANES), 1)
    lo = lane < B_DH
    rows = lax.broadcasted_iota(jnp.int32, (2 * tq, 1), 0)
    group = B_HEADS // B_KV_HEADS
    for g in range(B_HEADS // 2):
        kv = (2 * g) // group
        qg = q_ref[0, :, g * LANES:(g + 1) * LANES].astype(F32)
        qr = pltpu.roll(qg, B_DH, 1)
        if kv == 0:
            qa, qb = jnp.where(lo, qg, 0.0), jnp.where(lo, qr, 0.0)
        else:
            qa, qb = jnp.where(lo, 0.0, qr), jnp.where(lo, 0.0, qg)
        qs = jnp.concatenate([qa, qb], axis=0).astype(MXU_DTYPE)
        s = jnp.concatenate([_nt(qs, kt) if ok is None else jnp.where(ok, _nt(qs, kt), NEG)
                             for kt, ok in zip(k_tiles, masks)], axis=1)
        sk = jnp.where(rows < tq, sink_ref[2 * g], sink_ref[2 * g + 1])
        m = jnp.maximum(jnp.max(s, axis=-1, keepdims=True), sk)
        p = jnp.exp(s - m)
        l = jnp.sum(p, axis=-1, keepdims=True) + jnp.exp(sk - m)
        acc = _mm(p.astype(MXU_DTYPE), vals) / l
        if kv == 0:
            out = jnp.where(lo, acc[:tq], pltpu.roll(acc[tq:], B_DH, 1))
        else:
            out = jnp.where(lo, pltpu.roll(acc[:tq], B_DH, 1), acc[tq:])
        o_ref[0, :, g * LANES:(g + 1) * LANES] = out.astype(o_ref.dtype)


def _win_attn(q, k, v, sink, L):
    B, T, _ = q.shape
    t = ROW_TILE
    assert L == t and WINDOW <= t
    nT, LT = T // t, L // t
    kspec = lambda f: pl.BlockSpec((1, t, LANES), f)
    maps = [lambda b, i: (b, 0, 0),
            lambda b, i: (b, jnp.maximum(i - 1, LT), 0),
            lambda b, i: (b, i, 0),
            lambda b, i: (b, jnp.minimum(i + 1, nT - 1), 0)]
    return pl.pallas_call(
        functools.partial(_win_attn_kernel, LT),
        grid=(B, nT),
        in_specs=[pl.BlockSpec(memory_space=pltpu.SMEM),
                  pl.BlockSpec((1, t, B_HEADS * B_DH), lambda b, i: (b, i, 0))]
                 + [kspec(f) for f in maps] + [kspec(f) for f in maps],
        out_specs=pl.BlockSpec((1, t, B_HEADS * B_DH), lambda b, i: (b, i, 0)),
        out_shape=jax.ShapeDtypeStruct((B, T, B_HEADS * B_DH), MXU_DTYPE),
        compiler_params=_cparams("parallel", "parallel"),
        name="window_attn",
    )(sink, q, k, k, k, k, v, v, v, v)


def _gla_kernel(reverse, q_ref, k_ref, v_ref, g_ref, o_ref, st_ref):
    @pl.when(pl.program_id(1) == 0)
    def _():
        st_ref[...] = jnp.zeros(st_ref.shape, F32)

    C = D_CHUNK
    tb = q_ref.shape[1]
    r = lax.broadcasted_iota(jnp.int32, (C, C), 0)
    c = lax.broadcasted_iota(jnp.int32, (C, C), 1)
    keep = (c >= r) if reverse else (c <= r)
    tri = jnp.where(keep, 1.0, 0.0).astype(F32)
    chunks = range(tb // C)
    for ch in (reversed(chunks) if reverse else chunks):
        rows = slice(ch * C, (ch + 1) * C)
        b = jnp.dot(tri, g_ref[0, rows, :], precision=lax.Precision.HIGHEST, preferred_element_type=F32)
        b_last = b[0:1] if reverse else b[C - 1:C]
        q_t = q_ref[0, rows, :] * jnp.exp(b)
        kk = k_ref[0, rows, :]
        k_in = kk * jnp.exp(-b)
        k_st = kk * jnp.exp(b_last - b)
        decay = jnp.exp(b_last)
        for h in range(D_HEADS):
            sl = slice(h * D_DK, (h + 1) * D_DK)
            vsl = slice(h * D_DV, (h + 1) * D_DV)
            qh = q_t[:, sl].astype(MXU_DTYPE)
            a = jnp.where(keep, _nt(qh, k_in[:, sl].astype(MXU_DTYPE)), 0.0)
            vh = v_ref[0, rows, vsl]
            st = st_ref[h]
            o_ref[0, rows, vsl] = _mm(a.astype(MXU_DTYPE), vh) + _nt(qh, st.astype(MXU_DTYPE))
            st_ref[h] = st * decay[:, sl] + _tn(vh, k_st[:, sl].astype(MXU_DTYPE))


def _gla(q, k, v, lg, direction, L):
    B, T, _ = q.shape
    tb = ROW_TILE
    nB, LB = T // tb, L // tb
    reverse = direction == 1
    if reverse:
        blk = lambda s: jnp.where(s < LB, LB - 1 - s, LB + nB - 1 - s)
    else:
        blk = lambda s: s
    return pl.pallas_call(
        functools.partial(_gla_kernel, reverse),
        grid=(B, nB),
        in_specs=[pl.BlockSpec((1, tb, D_HEADS * D_DK), lambda b, s: (b, blk(s), 0)),
                  pl.BlockSpec((1, tb, D_HEADS * D_DK), lambda b, s: (b, blk(s), 0)),
                  pl.BlockSpec((1, tb, D_HEADS * D_DV), lambda b, s: (b, blk(s), 0)),
                  pl.BlockSpec((1, tb, D_HEADS * D_DK), lambda b, s: (b, blk(s), direction))],
        out_specs=pl.BlockSpec((1, tb, D_HEADS * D_DV), lambda b, s: (b, blk(s), 0)),
        out_shape=jax.ShapeDtypeStruct((B, T, D_HEADS * D_DV), F32),
        scratch_shapes=[pltpu.VMEM((D_HEADS, D_DV, D_DK), F32)],
        compiler_params=_cparams("parallel", "arbitrary"),
        name="gla_bwd" if reverse else "gla_fwd",
    )(q, k, v, lg)


def _outproj_kernel(gla, *refs):
    it = iter(refs)
    xs_ref, o1_ref = next(it), next(it)
    if gla:
        of_ref, ob_ref, r_ref, dgn_ref = (next(it) for _ in range(4))
        og = of_ref[0] + ob_ref[0]
        rr = r_ref[0]
        parts = []
        for h in range(D_HEADS):
            sl = slice(h * D_DV, (h + 1) * D_DV)
            rh = rr[:, sl]
            parts.append((_rms(og[:, sl], dgn_ref[...]) * (rh * jax.nn.sigmoid(rh))).astype(MXU_DTYPE))
        o2 = jnp.concatenate(parts, axis=1)
    else:
        o2 = next(it)[0]
    w_ref, gate_ref, n2_ref, sh_ref, sc_ref, xo_ref, ht_ref = (next(it) for _ in range(7))
    half = o1_ref.shape[1]
    x = xs_ref[0] + gate_ref[...] * (_tn(o1_ref[0], w_ref[:half, :]) + _mm(o2, w_ref[half:, :]))
    xo_ref[0] = x
    h2 = _rms(x, n2_ref[...]) * (1.0 + sc_ref[...]) + sh_ref[...]
    ht_ref[...] = h2.T.astype(ht_ref.dtype)


def _outproj(xs, o1, o2s, w, gate, norm_g, shift, scale, LT, gla):
    B, T, D = xs.shape
    tm = ROW_TILE
    nT = T // tm
    row = lambda b, i: (b, i, 0)
    cls = lambda b, i: (b, jnp.where(i >= LT, 1, 0), 0, 0)
    full2 = lambda b, i: (0, 0)
    args = [xs, o1]
    specs = [pl.BlockSpec((1, tm, D), row), pl.BlockSpec((1, o1.shape[1], tm), lambda b, i: (b, 0, i))]
    for a in o2s:
        args.append(a)
        specs.append(pl.BlockSpec(a.shape, full2) if a.ndim == 2 else pl.BlockSpec((1, tm, a.shape[2]), row))
    args += [w, gate, norm_g, shift, scale]
    specs += [pl.BlockSpec(w.shape, full2), pl.BlockSpec((None, None, 1, D), cls), pl.BlockSpec((1, D), full2),
              pl.BlockSpec((None, None, 1, D), cls), pl.BlockSpec((None, None, 1, D), cls)]
    return pl.pallas_call(
        functools.partial(_outproj_kernel, gla),
        grid=(B, nT), in_specs=specs,
        out_specs=[pl.BlockSpec((1, tm, D), row), pl.BlockSpec((D, tm), lambda b, i: (0, b * nT + i))],
        out_shape=[jax.ShapeDtypeStruct((B, T, D), F32), jax.ShapeDtypeStruct((D, B * T), MXU_DTYPE)],
        compiler_params=_cparams("parallel", "parallel"),
        name="outproj_gla" if gla else "outproj",
    )(*args)


def _top_values(s, k, with_rank=False):
    vals = []
    rank = jnp.full(s.shape, float(k), F32) if with_rank else None
    for a in range(k):
        m = jnp.max(s, axis=0, keepdims=True)
        vals.append(m)
        hit = s == m
        if with_rank:
            rank = jnp.where(hit, float(a), rank)
        if a + 1 < k:
            s = jnp.where(hit, -jnp.inf, s)
    vals = jnp.concatenate(vals, axis=0)
    return (vals, rank) if with_rank else vals


def _peer_gate_kernel(ht_ref, wqt_ref, keys_ref, r2_ref, e2_ref, cnt_ref, cc_ref):
    qt = _mm(wqt_ref[...], ht_ref[...])
    n, k = P_NKEYS, P_TOPK
    s1 = _mm(keys_ref[0], qt[:n].astype(MXU_DTYPE))
    s2 = _mm(keys_ref[1], qt[n:].astype(MXU_DTYPE))
    sv1 = _top_values(s1, k)
    sv2, rank2 = _top_values(s2, k, with_rank=True)
    cand = jnp.concatenate([sv1[a:a + 1] + sv2[:k // (a + 1)] for a in range(k)], axis=0)
    tau = _top_values(cand, k)[k - 1:k]
    top = sv1[0:1] + sv2[0:1]
    z = jnp.sum(jnp.where(cand >= tau, jnp.exp(cand - top), 0.0), axis=0, keepdims=True)
    cnt = jnp.zeros(s1.shape, F32)
    for b in range(k):
        cnt = cnt + jnp.where(s1 + sv2[b:b + 1] >= tau, 1.0, 0.0)
    r2_ref[0] = rank2.astype(r2_ref.dtype)
    e2_ref[0] = jnp.exp(s2 - sv2[0:1]).astype(e2_ref.dtype)
    cnt_ref[0] = cnt
    cc_ref[0] = jnp.exp(s1 - sv1[0:1]) / z


def _peer_token_tile(n_tok):
    return 512 if n_tok % 512 == 0 else 256


def _peer_gate(ht, wqt, keys):
    D, N = ht.shape
    tm = _peer_token_tile(N)
    shape = lambda dt: jax.ShapeDtypeStruct((P_HEADS, P_NKEYS, N), dt)
    ospec = pl.BlockSpec((1, P_NKEYS, tm), lambda i, h: (h, 0, i))
    return pl.pallas_call(
        _peer_gate_kernel,
        grid=(N // tm, P_HEADS),
        in_specs=[pl.BlockSpec((D, tm), lambda i, h: (0, i)),
                  pl.BlockSpec((P_DKEY, D), lambda i, h: (h, 0)),
                  pl.BlockSpec((2, P_NKEYS, P_DKEY // 2), lambda i, h: (h, 0, 0))],
        out_specs=[ospec] * 4,
        out_shape=[shape(GATE_DTYPE), shape(GATE_DTYPE), shape(F32), shape(F32)],
        compiler_params=_cparams("parallel", "arbitrary"),
        name="peer_gate",
    )(ht, wqt, keys)


def _rows_packed(row, n):
    tile = jnp.broadcast_to(row, (PACKED_ROWS, row.shape[1])).astype(GATE_DTYPE)
    return jnp.concatenate([tile] * (n // PACKED_ROWS), axis=0)


def _peer_dense_kernel(ht_ref, r2_ref, e2_ref, cnt_ref, cc_ref, u_ref, vt_ref, o_ref, acc_ref):
    j = pl.program_id(1)

    @pl.when(j == 0)
    def _():
        acc_ref[...] = jnp.zeros(acc_ref.shape, F32)

    n = P_NKEYS
    zero = jnp.zeros((n, ht_ref.shape[1]), GATE_DTYPE)
    ht = ht_ref[...]
    total = None
    nsub = u_ref.shape[0] // PEER_SUB
    scores = lambda k: _mm(u_ref[k * PEER_SUB:(k + 1) * PEER_SUB, :], ht)
    zt_next = scores(0)
    for k in range(nsub):
        zt = zt_next
        if k + 1 < nsub:
            zt_next = scores(k + 1)
        rows = []
        for ii in range(PEER_SUB // n):
            i1 = k * (PEER_SUB // n) + ii
            g = None
            for h in range(P_HEADS):
                sel = r2_ref[h] < _rows_packed(cnt_ref[h, i1:i1 + 1, :], n)
                t = jnp.where(sel, e2_ref[h], zero) * _rows_packed(cc_ref[h, i1:i1 + 1, :], n)
                g = t if g is None else g + t
            z = zt[ii * n:(ii + 1) * n]
            gelu = 0.5 * z * (1.0 + lax.erf(z * math.sqrt(0.5)))
            rows.append((gelu.astype(GATE_DTYPE) * g).astype(MXU_DTYPE))
        part = _mm(vt_ref[:, k * PEER_SUB:(k + 1) * PEER_SUB], jnp.concatenate(rows, axis=0))
        total = part if total is None else total + part
    acc_ref[...] += total

    @pl.when(j == pl.num_programs(1) - 1)
    def _():
        o_ref[...] = acc_ref[...].T


def _peer_dense(ht, r2, e2, cnt, cc, u, vt):
    D, N = ht.shape
    E = u.shape[0]
    tm = _peer_token_tile(N)
    te = 8 * P_NKEYS
    gspec = pl.BlockSpec((P_HEADS, P_NKEYS, tm), lambda i, j: (0, 0, i))
    rspec = pl.BlockSpec((P_HEADS, te // P_NKEYS, tm), lambda i, j: (0, j, i))
    return pl.pallas_call(
        _peer_dense_kernel,
        grid=(N // tm, E // te),
        in_specs=[pl.BlockSpec((D, tm), lambda i, j: (0, i)), gspec, gspec, rspec, rspec,
                  pl.BlockSpec((te, D), lambda i, j: (j, 0)),
                  pl.BlockSpec((D, te), lambda i, j: (0, j))],
        out_specs=pl.BlockSpec((tm, D), lambda i, j: (i, 0)),
        out_shape=jax.ShapeDtypeStruct((N, D), F32),
        scratch_shapes=[pltpu.VMEM((D, tm), F32)],
        compiler_params=_cparams("parallel", "arbitrary"),
        name="peer_dense",
    )(ht, r2, e2, cnt, cc, u, vt)


def _final_kernel(xs_ref, pp_ref, g_ref, o_ref):
    o_ref[0] = xs_ref[0] + g_ref[...] * pp_ref[0]


def _final(xs, pp, gate, L):
    B, T, D = xs.shape
    tm = ROW_TILE
    LT = L // tm
    src = lambda b, i: (b, i + LT, 0)
    return pl.pallas_call(
        _final_kernel,
        grid=(B, (T - L) // tm),
        in_specs=[pl.BlockSpec((1, tm, D), src), pl.BlockSpec((1, tm, D), src),
                  pl.BlockSpec((None, None, 1, D), lambda b, i: (b, 1, 0, 0))],
        out_specs=pl.BlockSpec((1, tm, D), lambda b, i: (b, i, 0)),
        out_shape=jax.ShapeDtypeStruct((B, T - L, D), F32),
        compiler_params=_cparams("parallel", "parallel"),
        name="final_residual",
    )(xs, pp, gate)


def _even_odd(n):
    return np.concatenate([np.arange(0, n, 2), np.arange(1, n, 2)])


def _rope_angles(S, L, rot_dim):
    pos = jnp.arange(S)
    r = (pos // GRID_W).astype(F32)
    cc = (pos % GRID_W).astype(F32)
    axis_dim = rot_dim // 2
    inv = 1.0 / (ROPE_THETA ** (jnp.arange(0, axis_dim, 2, dtype=F32) / axis_dim))
    ang = jnp.concatenate([r[:, None] * inv, cc[:, None] * inv], axis=-1)
    cos = jnp.concatenate([jnp.ones((L, rot_dim // 2), F32), jnp.cos(ang)], axis=0)
    sin = jnp.concatenate([jnp.zeros((L, rot_dim // 2), F32), jnp.sin(ang)], axis=0)
    return cos, sin


def _rope_tables_64(S, L):
    cos, sin = _rope_angles(S, L, 64)
    return jnp.tile(jnp.concatenate([cos, cos], -1), (1, 2)), jnp.tile(jnp.concatenate([-sin, sin], -1), (1, 2))


def _rope_tables_mla(S, L):
    cos, sin = _rope_angles(S, L, C_ROPE)
    T = cos.shape[0]
    one, zero = jnp.ones((T, C_NOPE), F32), jnp.zeros((T, C_NOPE), F32)
    pad1, pad0 = jnp.ones((T, LANES - C_NOPE - C_ROPE), F32), jnp.zeros((T, LANES - C_NOPE - C_ROPE), F32)
    return (jnp.concatenate([one, cos, cos, pad1], -1), jnp.concatenate([zero, -sin, sin, pad0], -1))


def _prep_even(w_in, a_qn, a_kn, b_qn, b_kn):
    sizes = (512, 512, 512, 512, 128, 128)
    off = np.concatenate([[0], np.cumsum(sizes)])
    p64 = _even_odd(64)

    def seg(k, permute):
        idx = np.arange(off[k], off[k + 1])
        if permute:
            idx = idx.reshape(-1, 64)[:, p64].reshape(-1)
        return idx

    cols = np.concatenate([seg(0, True), seg(1, True), seg(3, True), seg(4, True), seg(2, False), seg(5, False)])
    w = w_in[:, cols].astype(MXU_DTYPE)
    q_scale = A_DQK ** -0.5 * LOG2_E
    gain = jnp.concatenate([jnp.tile(a_qn[p64], 8) * q_scale, jnp.tile(a_kn[p64], 8),
                            jnp.tile(b_qn[p64], 8) * (B_DH ** -0.5), jnp.tile(b_kn[p64], 2)])[None, :]
    return w, gain


def _pad_cols(a, n):
    return jnp.concatenate([a, jnp.zeros(a.shape[:-1] + (n - a.shape[-1],), a.dtype)], axis=-1)


def _prep_odd(w_in, c_wq_up, c_wkv_up, c_qn, c_kn, d_wg_f, d_bg_f, d_wg_b, d_bg_b):
    D = w_in.shape[0]
    p32 = _even_odd(C_ROPE)
    o = np.concatenate([[0], np.cumsum((C_Q_RANK, C_KV_RANK, C_ROPE, 256, 256, 512, D_GATE_RANK, D_GATE_RANK, 512))])
    seg = lambda k: w_in[:, o[k]:o[k + 1]]
    w = jnp.concatenate([seg(0), seg(1), _pad_cols(seg(2)[:, p32], LANES), seg(3), seg(4), seg(5),
                         _pad_cols(jnp.concatenate([seg(6), seg(7)], -1), LANES), seg(8)], axis=-1).astype(MXU_DTYPE)
    hd = C_NOPE + C_ROPE
    head_perm = np.concatenate([np.arange(C_NOPE), C_NOPE + p32])
    wq = c_wq_up.reshape(C_Q_RANK, C_HEADS, hd)[:, :, head_perm]
    wq = _pad_cols(wq, LANES).reshape(C_Q_RANK, C_HEADS * LANES).astype(MXU_DTYPE)
    wkv = c_wkv_up.reshape(C_KV_RANK, C_HEADS, C_NOPE + C_DV)
    wk = _pad_cols(wkv[:, :, :C_NOPE], LANES).reshape(C_KV_RANK, C_HEADS * LANES)
    wv = wkv[:, :, C_NOPE:].reshape(C_KV_RANK, C_HEADS * C_DV)
    wkv = jnp.concatenate([wk, wv], axis=-1).astype(MXU_DTYPE)
    qn = jnp.tile(_pad_cols(c_qn[head_perm] * (hd ** -0.5 * LOG2_E), LANES), C_HEADS)[None, :]
    kn = jnp.tile(_pad_cols(c_kn[head_perm], LANES), C_HEADS)[None, :]
    n = D_HEADS * D_DK
    wg = jnp.zeros((LANES, 2 * n), F32)
    wg = wg.at[:D_GATE_RANK, :n].set(d_wg_f).at[D_GATE_RANK:2 * D_GATE_RANK, n:].set(d_wg_b).astype(MXU_DTYPE)
    bg = jnp.concatenate([d_bg_f, d_bg_b])[None, :]
    return w, wq, wkv, qn, kn, wg, bg


def kernel(x, c, ctx, c_ctx, w_mod, b_mod, norm1, norm2, e_w_in, e_w_out, a_qn, a_kn, a_lq1, a_lk1, a_lq2, a_lk2, a_gn, b_qn, b_kn, b_sink, o_w_in, o_w_out, c_gq, c_gkv, c_wq_up, c_wkv_up, c_qn, c_kn, d_wg_f, d_bg_f, d_wg_b, d_bg_b, d_gn, p_wq, p_keys, p_u, p_v):
    B, S, D = x.shape
    L = ctx.shape[1]
    depth = w_mod.shape[0]
    assert L == ROW_TILE and S % ROW_TILE == 0 and S % GRID_W == 0
    LT = L // ROW_TILE
    bf = MXU_DTYPE

    cv = jnp.concatenate([c, c_ctx[None, :], jnp.zeros((8 - B - 1, D), F32)], axis=0)
    mods = _modulation(cv, w_mod, b_mod).reshape(depth, 8, 6, D)
    mod_tab = jnp.stack([jnp.broadcast_to(mods[:, B][:, None], (depth, B, 6, D)), mods[:, :B]], axis=2)
    mod = lambda i, k: mod_tab[i, :, :, k][:, :, None, :]

    cos64, sin64 = _rope_tables_64(S, L)
    cos_c, sin_c = _rope_tables_mla(S, L)

    xs = jnp.concatenate([ctx, x], axis=1)
    prev = None
    for i in range(depth):
        j = i // 2
        if i % 2 == 0:
            lam_init = 0.8 - 0.6 * math.exp(-0.3 * i)
            w, gain = _prep_even(e_w_in[j], a_qn[j], a_kn[j], b_qn[j], b_kn[j])
            xs, (aq, ak, bq, bk, av, bv) = _inproj(
                "even", xs, prev, norm1[i][None], mod(i, 0), mod(i, 1), w,
                [(gain, "full"), (cos64, "rows"), (sin64, "rows")],
                [(512, bf, True), (512, bf, False), (512, bf, False), (128, bf, False), (512, bf, True),
                 (128, bf, False)], LT)
            lqk = jnp.stack([a_lq1[j], a_lk1[j], a_lq2[j], a_lk2[j]])
            o1 = _diff_attn(aq, ak, av, lqk, a_gn[j][None], L, lam_init)
            o2 = _win_attn(bq, bk, bv, b_sink[j], L)
            xs, ht = _outproj(xs, o1, [o2], e_w_out[j].astype(bf), mod(i, 2), norm2[i][None], mod(i, 3), mod(i, 4),
                              LT, gla=False)
        else:
            w, wq, wkv, qn, kn, wg, bg = _prep_odd(o_w_in[j], c_wq_up[j], c_wkv_up[j], c_qn[j], c_kn[j],
                                                   d_wg_f[j], d_bg_f[j], d_wg_b[j], d_bg_b[j])
            xs, (cq, ck, cvv, dq, dk, dv, lg, r) = _inproj(
                "odd", xs, prev, norm1[i][None], mod(i, 0), mod(i, 1), w,
                [(c_gq[j][None], "full"), (c_gkv[j][None], "full"), (wq, "full"), (wkv, "full"), (qn, "full"),
                 (kn, "full"), (wg, "full"), (bg, "full"), (cos_c, "rows"), (sin_c, "rows")],
                [(1024, bf, True), (1024, bf, False), (512, bf, True), (256, F32, False), (256, F32, False),
                 (512, bf, False), (512, F32, False), (512, F32, False)], LT)
            o1 = _mla_attn(cq, ck, cvv, L)
            of = _gla(dq, dk, dv, lg, 0, L)
            ob = _gla(dq, dk, dv, lg, 1, L)
            xs, ht = _outproj(xs, o1, [of, ob, r, d_gn[j][None]], o_w_out[j].astype(bf), mod(i, 2), norm2[i][None],
                              mod(i, 3), mod(i, 4), LT, gla=True)
        wqt = p_wq[i].T.astype(bf)
        keys = p_keys[i].reshape(2 * P_HEADS, P_NKEYS, P_DKEY // 2).astype(bf)
        r2, e2, cnt, cc = _peer_gate(ht, wqt, keys)
        pp = _peer_dense(ht, r2, e2, cnt, cc, p_u[i].astype(bf), p_v[i].T.astype(bf))
        prev = (pp.reshape(B, L + S, D), mod(i, 5))
    return _final(xs, prev[0], prev[1], L)
```

```python
import functools
import math

import numpy as np
import jax
import jax.numpy as jnp
from jax import lax
from jax.experimental import pallas as pl
from jax.experimental.pallas import tpu as pltpu

F32 = jnp.float32
MXU_DTYPE = jnp.bfloat16
GATE_DTYPE = jnp.bfloat16
PACKED_ROWS = 16
PEER_SUB = 256

GRID_W = 64
ROPE_THETA = 10000.0
EPS = 1e-6
NEG = -1e30
A_HEADS, A_DQK, A_DV = 4, 64, 128
B_HEADS, B_KV_HEADS, B_DH, WINDOW = 8, 2, 64, 128
C_HEADS, C_Q_RANK, C_KV_RANK, C_NOPE, C_ROPE, C_DV = 8, 256, 128, 64, 32, 64
D_HEADS, D_DK, D_DV, D_GATE_RANK, D_GATE_TAU, D_CHUNK = 4, 64, 128, 16, 16.0, 64
P_HEADS, P_NKEYS, P_DKEY, P_TOPK = 8, 128, 256, 16

LANES = 128
ROW_TILE = 256
KEY_CHUNK = 256
KEY_CHUNK_BIG = 768
QUERY_TILE = 512
LOG2_E = math.log2(math.e)
VMEM_LIMIT = 56 * 1024 * 1024


def _cparams(*sem):
    return pltpu.CompilerParams(dimension_semantics=sem, vmem_limit_bytes=VMEM_LIMIT)


def _rms(x, g):
    return x * lax.rsqrt(jnp.mean(x * x, axis=-1, keepdims=True) + EPS) * g


def _nt(a, b):
    return lax.dot_general(a, b, (((1,), (1,)), ((), ())), preferred_element_type=F32)


def _tn(a, b):
    return lax.dot_general(a, b, (((0,), (0,)), ((), ())), preferred_element_type=F32)


def _mm(a, b):
    return jnp.dot(a, b, preferred_element_type=F32)


def _mod_kernel(c_ref, w_ref, b_ref, o_ref):
    cv = c_ref[...]
    s = cv * jax.nn.sigmoid(cv)
    o_ref[0] = jnp.dot(s, w_ref[0], precision=lax.Precision.HIGHEST, preferred_element_type=F32) + b_ref[0]


def _modulation(cvecs, w_mod, b_mod):
    depth, d, n = w_mod.shape
    rows = cvecs.shape[0]
    tn = 1536
    return pl.pallas_call(
        _mod_kernel,
        grid=(depth, n // tn),
        in_specs=[pl.BlockSpec((rows, d), lambda l, j: (0, 0)),
                  pl.BlockSpec((1, d, tn), lambda l, j: (l, 0, j)),
                  pl.BlockSpec((1, 1, tn), lambda l, j: (l, 0, j))],
        out_specs=pl.BlockSpec((1, rows, tn), lambda l, j: (l, 0, j)),
        out_shape=jax.ShapeDtypeStruct((depth, rows, n), F32),
        compiler_params=_cparams("parallel", "parallel"),
        name="modulation",
    )(cvecs, w_mod, b_mod.reshape(depth, 1, n))


def _stream_prologue(has_prev, it):
    xs_ref = next(it)
    x = xs_ref[0]
    if has_prev:
        pp_ref, pg_ref = next(it), next(it)
        x = x + pg_ref[...] * pp_ref[0]
    g1_ref, sh_ref, sc_ref, w_ref = next(it), next(it), next(it), next(it)
    h = _rms(x, g1_ref[...]) * (1.0 + sc_ref[...]) + sh_ref[...]
    z = _mm(h.astype(MXU_DTYPE), w_ref[...])
    return x, z


def _norm_rope_pair64(zg, gain, cosf, sinf):
    lane = lax.broadcasted_iota(jnp.int32, zg.shape, 1)
    lo = lane < 64
    sq = zg * zg
    s_lo = jnp.sum(jnp.where(lo, sq, 0.0), axis=-1, keepdims=True)
    s_hi = jnp.sum(jnp.where(lo, 0.0, sq), axis=-1, keepdims=True)
    ms = jnp.where(lo, s_lo, s_hi) * (1.0 / 64.0)
    y = zg * lax.rsqrt(ms + EPS) * gain
    partner = jnp.where((lane & 63) < 32, pltpu.roll(y, 96, 1), pltpu.roll(y, 32, 1))
    return y * cosf + partner * sinf


def _inproj_even_kernel(has_prev, *refs):
    it = iter(refs)
    x, z = _stream_prologue(has_prev, it)
    gain_ref, cos_ref, sin_ref = next(it), next(it), next(it)
    if has_prev:
        xo_ref = next(it)
        xo_ref[0] = x
    aq_ref, ak_ref, bq_ref, bk_ref, av_ref, bv_ref = (next(it) for _ in range(6))
    cosf, sinf = cos_ref[...], sin_ref[...]
    g = 0
    for ref, ngroups, transposed in ((aq_ref, 4, True), (ak_ref, 4, False), (bq_ref, 4, False), (bk_ref, 1, False)):
        for k in range(ngroups):
            sl = slice(g * LANES, (g + 1) * LANES)
            y = _norm_rope_pair64(z[:, sl], gain_ref[:, sl], cosf, sinf)
            if transposed:
                ref[0, k * LANES:(k + 1) * LANES, :] = y.T.astype(ref.dtype)
            else:
                ref[0, :, k * LANES:(k + 1) * LANES] = y.astype(ref.dtype)
            g += 1
    o = g * LANES
    av_ref[0] = z[:, o:o + 512].T.astype(av_ref.dtype)
    bv_ref[0] = z[:, o + 512:o + 640].astype(bv_ref.dtype)


def _norm_rope_mla(y, gain, cosf, sinf):
    lane = lax.broadcasted_iota(jnp.int32, y.shape, 1)
    ms = jnp.sum(y * y, axis=-1, keepdims=True) * (1.0 / (C_NOPE + C_ROPE))
    y = y * lax.rsqrt(ms + EPS) * gain
    partner = jnp.where(lane < C_NOPE + C_ROPE // 2, pltpu.roll(y, LANES - C_ROPE // 2, 1),
                        pltpu.roll(y, C_ROPE // 2, 1))
    return y * cosf + partner * sinf


def _log_sigmoid(x):
    return jnp.minimum(x, 0.0) - jnp.log1p(jnp.exp(-jnp.abs(x)))


def _inproj_odd_kernel(has_prev, *refs):
    it = iter(refs)
    x, z = _stream_prologue(has_prev, it)
    (gq_ref, gkv_ref, wq_ref, wkv_ref, qn_ref, kn_ref, wg_ref, bg_ref, cos_ref, sin_ref) = (next(it) for _ in range(10))
    if has_prev:
        xo_ref = next(it)
        xo_ref[0] = x
    cq_ref, ck_ref, cv_ref, dq_ref, dk_ref, dv_ref, lg_ref, r_ref = (next(it) for _ in range(8))
    cosf, sinf = cos_ref[...], sin_ref[...]
    qn = _rms(z[:, 0:256], gq_ref[...])
    q_up = _mm(qn.astype(MXU_DTYPE), wq_ref[...])
    kvn = _rms(z[:, 256:384], gkv_ref[...])
    kv_up = _mm(kvn.astype(MXU_DTYPE), wkv_ref[...])
    kpe = pltpu.roll(z[:, 384:512], C_NOPE, 1)
    for h in range(C_HEADS):
        sl = slice(h * LANES, (h + 1) * LANES)
        cq_ref[0, sl, :] = _norm_rope_mla(q_up[:, sl], qn_ref[:, sl], cosf, sinf).T.astype(cq_ref.dtype)
        ck_ref[0, :, sl] = _norm_rope_mla(kv_up[:, sl] + kpe, kn_ref[:, sl], cosf, sinf).astype(ck_ref.dtype)
    cv_ref[0] = kv_up[:, C_HEADS * LANES:].T.astype(cv_ref.dtype)
    dq_ref[0] = z[:, 512:768] * (D_DK ** -0.5)
    dk_ref[0] = z[:, 768:1024]
    dv_ref[0] = z[:, 1024:1536].astype(dv_ref.dtype)
    gl = _mm(z[:, 1536:1664].astype(MXU_DTYPE), wg_ref[...]) + bg_ref[...]
    lg_ref[0] = _log_sigmoid(gl) * (1.0 / D_GATE_TAU)
    r_ref[0] = z[:, 1664:2176]


def _inproj(kind, xs, prev, norm_g, shift, scale, w, extra, out_cols, LT):
    B, T, D = xs.shape
    tm = ROW_TILE
    nT = T // tm
    has_prev = prev is not None
    row = lambda b, i: (b, i, 0)
    cls = lambda b, i: (b, jnp.where(i >= LT, 1, 0), 0, 0)
    full2 = lambda b, i: (0, 0)
    args, specs = [xs], [pl.BlockSpec((1, tm, D), row)]
    if has_prev:
        args += [prev[0], prev[1]]
        specs += [pl.BlockSpec((1, tm, D), row), pl.BlockSpec((None, None, 1, D), cls)]
    args += [norm_g, shift, scale, w]
    specs += [pl.BlockSpec((1, D), full2), pl.BlockSpec((None, None, 1, D), cls),
              pl.BlockSpec((None, None, 1, D), cls), pl.BlockSpec(w.shape, full2)]
    for arr, k in extra:
        args.append(arr)
        if k == "full":
            specs.append(pl.BlockSpec(arr.shape, full2))
        else:
            specs.append(pl.BlockSpec((tm, arr.shape[1]), lambda b, i: (i, 0)))
    out_shape, out_specs = [], []
    if has_prev:
        out_shape.append(jax.ShapeDtypeStruct((B, T, D), F32))
        out_specs.append(pl.BlockSpec((1, tm, D), row))
    for n, dt, transposed in out_cols:
        if transposed:
            out_shape.append(jax.ShapeDtypeStruct((B, n, T), dt))
            out_specs.append(pl.BlockSpec((1, n, tm), lambda b, i: (b, 0, i)))
        else:
            out_shape.append(jax.ShapeDtypeStruct((B, T, n), dt))
            out_specs.append(pl.BlockSpec((1, tm, n), row))
    body = _inproj_even_kernel if kind == "even" else _inproj_odd_kernel
    outs = pl.pallas_call(
        functools.partial(body, has_prev),
        grid=(B, nT), in_specs=specs, out_specs=out_specs, out_shape=out_shape,
        compiler_params=_cparams("parallel", "parallel"),
        name="inproj_" + kind,
    )(*args)
    if has_prev:
        return outs[0], outs[1:]
    return xs, outs


def _flash_streams(ctx, q_streams, k_cols, v_rows, k_ref, vt_ref, ml_ref, acc_ref, sc_ref=None, p_ref=None):
    T = k_ref.shape[1]
    tq = q_streams[0].shape[1]
    acc_ref[...] = jnp.zeros(acc_ref.shape, F32)
    init = (jnp.full((1, tq), -jnp.inf, F32), jnp.zeros((1, tq), F32)) * 2

    def chunk(start, size, ml):
        out = []
        scs = [_mm(k_ref[0, start:start + size, k_cols[s]], q_streams[s]) for s in range(2)]
        for s in range(2):
            m_prev, l_prev = ml[2 * s], ml[2 * s + 1]
            m_new = jnp.maximum(m_prev, jnp.max(scs[s], axis=0, keepdims=True))
            alpha = jnp.exp2(m_prev - m_new)
            p = jnp.exp2(scs[s] - m_new)
            acc_ref[s] = alpha * acc_ref[s] + _mm(vt_ref[0, v_rows[s], start:start + size], p.astype(MXU_DTYPE))
            out += [m_new, alpha * l_prev + jnp.sum(p, axis=0, keepdims=True)]
        return tuple(out)

    if ctx:
        ml = init
        for c in range(0, T, KEY_CHUNK):
            ml = chunk(c, KEY_CHUNK, ml)
        ml_ref[...] = jnp.concatenate(ml, axis=0)
        return

    big = sc_ref.shape[1]
    piece = KEY_CHUNK
    pieces = big // piece
    n_chunks = T // big

    def piece_scores(c, j, s):
        start = pl.multiple_of(c * big + j * piece, piece)
        return _mm(k_ref[0, pl.ds(start, piece), k_cols[s]], q_streams[s])

    def col_max(parts):
        return functools.reduce(jnp.maximum, [jnp.max(x, axis=0, keepdims=True) for x in parts])

    def value_product(c, s):
        start = pl.multiple_of(c * big, big)
        return _mm(vt_ref[0, v_rows[s], pl.ds(start, big)], p_ref[s])

    def pipelined(t, carry, look_ahead=True):
        ml, mx, alpha_prev = carry
        pv = [value_product(jnp.maximum(t - 1, 0), s) for s in range(2)]
        m_new = [jnp.maximum(ml[2 * s], mx[s]) for s in range(2)]
        alpha = [jnp.exp2(ml[2 * s] - m_new[s]) for s in range(2)]
        ahead, sums = ([], []), ([], [])
        for j in range(pieces):
            rows = slice(j * piece, (j + 1) * piece)
            if look_ahead:
                for s in range(2):
                    ahead[s].append(piece_scores(t + 1, j, s))
            for s in range(2):
                p = jnp.exp2(sc_ref[s, rows, :] - m_new[s])
                sums[s].append(jnp.sum(p, axis=0, keepdims=True))
                p_ref[s, rows, :] = p.astype(p_ref.dtype)
            if look_ahead:
                for s in range(2):
                    sc_ref[s, rows, :] = ahead[s][j]
        new_ml = []
        for s in range(2):
            acc_ref[s] = alpha_prev[s] * acc_ref[s] + pv[s]
            new_ml += [m_new[s], alpha[s] * ml[2 * s + 1] + functools.reduce(jnp.add, sums[s])]
        mx_next = tuple(col_max(ahead[s]) for s in range(2)) if look_ahead else mx
        return tuple(new_ml), mx_next, tuple(alpha)

    first = []
    for s in range(2):
        parts = [piece_scores(0, j, s) for j in range(pieces)]
        for j in range(pieces):
            sc_ref[s, j * piece:(j + 1) * piece, :] = parts[j]
        first.append(col_max(parts))
    p_ref[...] = jnp.zeros(p_ref.shape, p_ref.dtype)
    one = jnp.ones((1, tq), F32)
    carry = lax.fori_loop(0, n_chunks - 1, pipelined, (init, tuple(first), (one, one)))
    ml, _, alpha_last = pipelined(n_chunks - 1, carry, look_ahead=False)
    for s in range(2):
        acc_ref[s] = alpha_last[s] * acc_ref[s] + value_product(n_chunks - 1, s)
    ml_ref[...] = jnp.concatenate(ml, axis=0)


def _query_tile(n_q):
    return QUERY_TILE if n_q % QUERY_TILE == 0 else ROW_TILE


def _big_chunk(T):
    return KEY_CHUNK_BIG if T % KEY_CHUNK_BIG == 0 else KEY_CHUNK


def _flash_scratch(ctx, T, dv, tq):
    scratch = [pltpu.VMEM((4, tq), F32), pltpu.VMEM((2, dv, tq), F32)]
    if not ctx:
        scratch += [pltpu.VMEM((2, _big_chunk(T), tq), F32), pltpu.VMEM((2, _big_chunk(T), tq), MXU_DTYPE)]
    return scratch


def _split_queries(qt, L):
    return qt[:, :, :L], qt[:, :, L:]


def _diff_attn_kernel(ctx, lam_init, q_ref, k_ref, vt_ref, lqk_ref, gn_ref, o_ref, ml_ref, acc_ref, *pipe):
    q = q_ref[0]
    row = lax.broadcasted_iota(jnp.int32, q.shape, 0)
    zero = jnp.zeros_like(q)
    streams = (jnp.where(row < A_DQK, q, zero), jnp.where(row < A_DQK, zero, q))
    every = slice(None)
    _flash_streams(ctx, streams, (every, every), (every, every), k_ref, vt_ref, ml_ref, acc_ref, *pipe)
    lqk = lqk_ref[...]
    lam = (jnp.exp(jnp.sum(lqk[0:1] * lqk[1:2], axis=-1, keepdims=True))
           - jnp.exp(jnp.sum(lqk[2:3] * lqk[3:4], axis=-1, keepdims=True)) + lam_init)
    o = acc_ref[0] / ml_ref[1:2, :] - lam * (acc_ref[1] / ml_ref[3:4, :])
    o = o * lax.rsqrt(jnp.mean(o * o, axis=0, keepdims=True) + EPS) * gn_ref[...]
    o_ref[0] = (o * (1.0 - lam_init)).astype(o_ref.dtype)


def _diff_attn(qt, k, vt, lqk, gn, L, lam_init):
    B, T, _ = k.shape
    assert L % KEY_CHUNK == 0 and T % KEY_CHUNK == 0

    def call(ctx, q, keys, tq):
        n_q = q.shape[2]
        gn_b = jnp.broadcast_to(gn.reshape(A_DV, 1), (A_DV, tq))
        return pl.pallas_call(
            functools.partial(_diff_attn_kernel, ctx, lam_init),
            grid=(B, A_HEADS, n_q // tq),
            in_specs=[pl.BlockSpec((1, LANES, tq), lambda b, h, i: (b, h, i)),
                      pl.BlockSpec((1, keys, LANES), lambda b, h, i: (b, 0, h)),
                      pl.BlockSpec((1, A_DV, keys), lambda b, h, i: (b, h, 0)),
                      pl.BlockSpec(lqk.shape, lambda b, h, i: (0, 0)),
                      pl.BlockSpec(gn_b.shape, lambda b, h, i: (0, 0))],
            out_specs=pl.BlockSpec((1, A_DV, tq), lambda b, h, i: (b, h, i)),
            out_shape=jax.ShapeDtypeStruct((B, A_HEADS * A_DV, n_q), MXU_DTYPE),
            scratch_shapes=_flash_scratch(ctx, keys, A_DV, tq),
            compiler_params=_cparams("parallel", "parallel", "parallel"),
            name="diff_attn_ctx" if ctx else "diff_attn",
        )(q, k, vt, lqk, gn_b)

    q_ctx, q_lat = _split_queries(qt, L)
    return jnp.concatenate([call(True, q_ctx, L, ROW_TILE), call(False, q_lat, T, _query_tile(T - L))], axis=2)


def _mla_attn_kernel(ctx, q_ref, k_ref, vt_ref, o_ref, ml_ref, acc_ref, *pipe):
    streams = (q_ref[0, :LANES, :], q_ref[0, LANES:, :])
    k_cols = (slice(0, LANES), slice(LANES, 2 * LANES))
    v_rows = (slice(0, C_DV), slice(C_DV, 2 * C_DV))
    _flash_streams(ctx, streams, k_cols, v_rows, k_ref, vt_ref, ml_ref, acc_ref, *pipe)
    o_ref[0] = jnp.concatenate([acc_ref[0] / ml_ref[1:2, :], acc_ref[1] / ml_ref[3:4, :]], axis=0).astype(o_ref.dtype)


def _mla_attn(qt, k, vt, L):
    B, T, _ = k.shape
    assert L % KEY_CHUNK == 0 and T % KEY_CHUNK == 0

    def call(ctx, q, keys, tq):
        n_q = q.shape[2]
        return pl.pallas_call(
            functools.partial(_mla_attn_kernel, ctx),
            grid=(B, C_HEADS // 2, n_q // tq),
            in_specs=[pl.BlockSpec((1, 2 * LANES, tq), lambda b, h, i: (b, h, i)),
                      pl.BlockSpec((1, keys, 2 * LANES), lambda b, h, i: (b, 0, h)),
                      pl.BlockSpec((1, 2 * C_DV, keys), lambda b, h, i: (b, h, 0))],
            out_specs=pl.BlockSpec((1, 2 * C_DV, tq), lambda b, h, i: (b, h, i)),
            out_shape=jax.ShapeDtypeStruct((B, C_HEADS * C_DV, n_q), MXU_DTYPE),
            scratch_shapes=_flash_scratch(ctx, keys, C_DV, tq),
            compiler_params=_cparams("parallel", "parallel", "parallel"),
            name="mla_attn_ctx" if ctx else "mla_attn",
        )(q, k, vt)

    q_ctx, q_lat = _split_queries(qt, L)
    return jnp.concatenate([call(True, q_ctx, L, ROW_TILE), call(False, q_lat, T, _query_tile(T - L))], axis=2)


def _win_attn_kernel(LT, sink_ref, q_ref, kc_ref, kp_ref, kk_ref, kn_ref, vc_ref, vp_ref, vk_ref, vn_ref, o_ref):
    i, nT = pl.program_id(1), pl.num_programs(1)
    tq = q_ref.shape[1]
    tk = kc_ref.shape[1]
    r = lax.broadcasted_iota(jnp.int32, (2 * tq, tk), 0)
    c = lax.broadcasted_iota(jnp.int32, (2 * tq, tk), 1)
    d = c - jnp.where(r >= tq, r - tq, r)
    is_x = i >= LT
    far = 4 * tk
    ok_p = d >= jnp.where(jnp.logical_and(is_x, i - 1 >= LT), tk - WINDOW, far)
    ok_k = jnp.abs(d) <= jnp.where(is_x, WINDOW, -1)
    ok_n = d <= jnp.where(jnp.logical_and(is_x, i + 1 <= nT - 1), WINDOW - tk, -far)
    k_tiles = (kc_ref[0], kp_ref[0], kk_ref[0], kn_ref[0])
    masks = (None, ok_p, ok_k, ok_n)
    vals = jnp.concatenate([vc_ref[0], vp_ref[0], vk_ref[0], vn_ref[0]], axis=0)
    lane = lax.broadcasted_iota(jnp.int32, (tq, LANES), 1)
    lo = lane < B_DH
    rows = lax.broadcasted_iota(jnp.int32, (2 * tq, 1), 0)
    group = B_HEADS // B_KV_HEADS
    for g in range(B_HEADS // 2):
        kv = (2 * g) // group
        qg = q_ref[0, :, g * LANES:(g + 1) * LANES].astype(F32)
        qr = pltpu.roll(qg, B_DH, 1)
        if kv == 0:
            qa, qb = jnp.where(lo, qg, 0.0), jnp.where(lo, qr, 0.0)
        else:
            qa, qb = jnp.where(lo, 0.0, qr), jnp.where(lo, 0.0, qg)
        qs = jnp.concatenate([qa, qb], axis=0).astype(MXU_DTYPE)
        s = jnp.concatenate([_nt(qs, kt) if ok is None else jnp.where(ok, _nt(qs, kt), NEG)
                             for kt, ok in zip(k_tiles, masks)], axis=1)
        sk = jnp.where(rows < tq, sink_ref[2 * g], sink_ref[2 * g + 1])
        m = jnp.maximum(jnp.max(s, axis=-1, keepdims=True), sk)
        p = jnp.exp(s - m)
        l = jnp.sum(p, axis=-1, keepdims=True) + jnp.exp(sk - m)
        acc = _mm(p.astype(MXU_DTYPE), vals) / l
        if kv == 0:
            out = jnp.where(lo, acc[:tq], pltpu.roll(acc[tq:], B_DH, 1))
        else:
            out = jnp.where(lo, pltpu.roll(acc[:tq], B_DH, 1), acc[tq:])
        o_ref[0, :, g * LANES:(g + 1) * LANES] = out.astype(o_ref.dtype)


def _win_attn(q, k, v, sink, L):
    B, T, _ = q.shape
    t = ROW_TILE
    assert L == t and WINDOW <= t
    nT, LT = T // t, L // t
    kspec = lambda f: pl.BlockSpec((1, t, LANES), f)
    maps = [lambda b, i: (b, 0, 0),
            lambda b, i: (b, jnp.maximum(i - 1, LT), 0),
            lambda b, i: (b, i, 0),
            lambda b, i: (b, jnp.minimum(i + 1, nT - 1), 0)]
    return pl.pallas_call(
        functools.partial(_win_attn_kernel, LT),
        grid=(B, nT),
        in_specs=[pl.BlockSpec(memory_space=pltpu.SMEM),
                  pl.BlockSpec((1, t, B_HEADS * B_DH), lambda b, i: (b, i, 0))]
                 + [kspec(f) for f in maps] + [kspec(f) for f in maps],
        out_specs=pl.BlockSpec((1, t, B_HEADS * B_DH), lambda b, i: (b, i, 0)),
        out_shape=jax.ShapeDtypeStruct((B, T, B_HEADS * B_DH), MXU_DTYPE),
        compiler_params=_cparams("parallel", "parallel"),
        name="window_attn",
    )(sink, q, k, k, k, k, v, v, v, v)


def _gla_kernel(reverse, q_ref, k_ref, v_ref, g_ref, o_ref, st_ref):
    @pl.when(pl.program_id(1) == 0)
    def _():
        st_ref[...] = jnp.zeros(st_ref.shape, F32)

    C = D_CHUNK
    tb = q_ref.shape[1]
    r = lax.broadcasted_iota(jnp.int32, (C, C), 0)
    c = lax.broadcasted_iota(jnp.int32, (C, C), 1)
    keep = (c >= r) if reverse else (c <= r)
    tri = jnp.where(keep, 1.0, 0.0).astype(F32)
    chunks = range(tb // C)
    for ch in (reversed(chunks) if reverse else chunks):
        rows = slice(ch * C, (ch + 1) * C)
        b = jnp.dot(tri, g_ref[0, rows, :], precision=lax.Precision.HIGHEST, preferred_element_type=F32)
        b_last = b[0:1] if reverse else b[C - 1:C]
        q_t = q_ref[0, rows, :] * jnp.exp(b)
        kk = k_ref[0, rows, :]
        k_in = kk * jnp.exp(-b)
        k_st = kk * jnp.exp(b_last - b)
        decay = jnp.exp(b_last)
        for h in range(D_HEADS):
            sl = slice(h * D_DK, (h + 1) * D_DK)
            vsl = slice(h * D_DV, (h + 1) * D_DV)
            qh = q_t[:, sl].astype(MXU_DTYPE)
            a = jnp.where(keep, _nt(qh, k_in[:, sl].astype(MXU_DTYPE)), 0.0)
            vh = v_ref[0, rows, vsl]
            st = st_ref[h]
            o_ref[0, rows, vsl] = _mm(a.astype(MXU_DTYPE), vh) + _nt(qh, st.astype(MXU_DTYPE))
            st_ref[h] = st * decay[:, sl] + _tn(vh, k_st[:, sl].astype(MXU_DTYPE))


def _gla(q, k, v, lg, direction, L):
    B, T, _ = q.shape
    tb = ROW_TILE
    nB, LB = T // tb, L // tb
    reverse = direction == 1
    if reverse:
        blk = lambda s: jnp.where(s < LB, LB - 1 - s, LB + nB - 1 - s)
    else:
        blk = lambda s: s
    return pl.pallas_call(
        functools.partial(_gla_kernel, reverse),
        grid=(B, nB),
        in_specs=[pl.BlockSpec((1, tb, D_HEADS * D_DK), lambda b, s: (b, blk(s), 0)),
                  pl.BlockSpec((1, tb, D_HEADS * D_DK), lambda b, s: (b, blk(s), 0)),
                  pl.BlockSpec((1, tb, D_HEADS * D_DV), lambda b, s: (b, blk(s), 0)),
                  pl.BlockSpec((1, tb, D_HEADS * D_DK), lambda b, s: (b, blk(s), direction))],
        out_specs=pl.BlockSpec((1, tb, D_HEADS * D_DV), lambda b, s: (b, blk(s), 0)),
        out_shape=jax.ShapeDtypeStruct((B, T, D_HEADS * D_DV), F32),
        scratch_shapes=[pltpu.VMEM((D_HEADS, D_DV, D_DK), F32)],
        compiler_params=_cparams("parallel", "arbitrary"),
        name="gla_bwd" if reverse else "gla_fwd",
    )(q, k, v, lg)


def _outproj_kernel(gla, *refs):
    it = iter(refs)
    xs_ref, o1_ref = next(it), next(it)
    if gla:
        of_ref, ob_ref, r_ref, dgn_ref = (next(it) for _ in range(4))
        og = of_ref[0] + ob_ref[0]
        rr = r_ref[0]
        parts = []
        for h in range(D_HEADS):
            sl = slice(h * D_DV, (h + 1) * D_DV)
            rh = rr[:, sl]
            parts.append((_rms(og[:, sl], dgn_ref[...]) * (rh * jax.nn.sigmoid(rh))).astype(MXU_DTYPE))
        o2 = jnp.concatenate(parts, axis=1)
    else:
        o2 = next(it)[0]
    w_ref, gate_ref, n2_ref, sh_ref, sc_ref, xo_ref, ht_ref = (next(it) for _ in range(7))
    half = o1_ref.shape[1]
    x = xs_ref[0] + gate_ref[...] * (_tn(o1_ref[0], w_ref[:half, :]) + _mm(o2, w_ref[half:, :]))
    xo_ref[0] = x
    h2 = _rms(x, n2_ref[...]) * (1.0 + sc_ref[...]) + sh_ref[...]
    ht_ref[...] = h2.T.astype(ht_ref.dtype)


def _outproj(xs, o1, o2s, w, gate, norm_g, shift, scale, LT, gla):
    B, T, D = xs.shape
    tm = ROW_TILE
    nT = T // tm
    row = lambda b, i: (b, i, 0)
    cls = lambda b, i: (b, jnp.where(i >= LT, 1, 0), 0, 0)
    full2 = lambda b, i: (0, 0)
    args = [xs, o1]
    specs = [pl.BlockSpec((1, tm, D), row), pl.BlockSpec((1, o1.shape[1], tm), lambda b, i: (b, 0, i))]
    for a in o2s:
        args.append(a)
        specs.append(pl.BlockSpec(a.shape, full2) if a.ndim == 2 else pl.BlockSpec((1, tm, a.shape[2]), row))
    args += [w, gate, norm_g, shift, scale]
    specs += [pl.BlockSpec(w.shape, full2), pl.BlockSpec((None, None, 1, D), cls), pl.BlockSpec((1, D), full2),
              pl.BlockSpec((None, None, 1, D), cls), pl.BlockSpec((None, None, 1, D), cls)]
    return pl.pallas_call(
        functools.partial(_outproj_kernel, gla),
        grid=(B, nT), in_specs=specs,
        out_specs=[pl.BlockSpec((1, tm, D), row), pl.BlockSpec((D, tm), lambda b, i: (0, b * nT + i))],
        out_shape=[jax.ShapeDtypeStruct((B, T, D), F32), jax.ShapeDtypeStruct((D, B * T), MXU_DTYPE)],
        compiler_params=_cparams("parallel", "parallel"),
        name="outproj_gla" if gla else "outproj",
    )(*args)


def _top_values(s, k, with_rank=False):
    vals = []
    rank = jnp.full(s.shape, float(k), F32) if with_rank else None
    for a in range(k):
        m = jnp.max(s, axis=0, keepdims=True)
        vals.append(m)
        hit = s == m
        if with_rank:
            rank = jnp.where(hit, float(a), rank)
        if a + 1 < k:
            s = jnp.where(hit, -jnp.inf, s)
    vals = jnp.concatenate(vals, axis=0)
    return (vals, rank) if with_rank else vals


def _peer_gate_kernel(ht_ref, wqt_ref, keys_ref, r2_ref, e2_ref, cnt_ref, cc_ref):
    qt = _mm(wqt_ref[...], ht_ref[...])
    n, k = P_NKEYS, P_TOPK
    s1 = _mm(keys_ref[0], qt[:n].astype(MXU_DTYPE))
    s2 = _mm(keys_ref[1], qt[n:].astype(MXU_DTYPE))
    sv1 = _top_values(s1, k)
    sv2, rank2 = _top_values(s2, k, with_rank=True)
    cand = jnp.concatenate([sv1[a:a + 1] + sv2[:k // (a + 1)] for a in range(k)], axis=0)
    tau = _top_values(cand, k)[k - 1:k]
    top = sv1[0:1] + sv2[0:1]
    z = jnp.sum(jnp.where(cand >= tau, jnp.exp(cand - top), 0.0), axis=0, keepdims=True)
    cnt = jnp.zeros(s1.shape, F32)
    for b in range(k):
        cnt = cnt + jnp.where(s1 + sv2[b:b + 1] >= tau, 1.0, 0.0)
    r2_ref[0] = rank2.astype(r2_ref.dtype)
    e2_ref[0] = jnp.exp(s2 - sv2[0:1]).astype(e2_ref.dtype)
    cnt_ref[0] = cnt
    cc_ref[0] = jnp.exp(s1 - sv1[0:1]) / z


def _peer_token_tile(n_tok):
    return 512 if n_tok % 512 == 0 else 256


def _peer_gate(ht, wqt, keys):
    D, N = ht.shape
    tm = _peer_token_tile(N)
    shape = lambda dt: jax.ShapeDtypeStruct((P_HEADS, P_NKEYS, N), dt)
    ospec = pl.BlockSpec((1, P_NKEYS, tm), lambda i, h: (h, 0, i))
    return pl.pallas_call(
        _peer_gate_kernel,
        grid=(N // tm, P_HEADS),
        in_specs=[pl.BlockSpec((D, tm), lambda i, h: (0, i)),
                  pl.BlockSpec((P_DKEY, D), lambda i, h: (h, 0)),
                  pl.BlockSpec((2, P_NKEYS, P_DKEY // 2), lambda i, h: (h, 0, 0))],
        out_specs=[ospec] * 4,
        out_shape=[shape(GATE_DTYPE), shape(GATE_DTYPE), shape(F32), shape(F32)],
        compiler_params=_cparams("parallel", "arbitrary"),
        name="peer_gate",
    )(ht, wqt, keys)


def _rows_packed(row, n):
    tile = jnp.broadcast_to(row, (PACKED_ROWS, row.shape[1])).astype(GATE_DTYPE)
    return jnp.concatenate([tile] * (n // PACKED_ROWS), axis=0)


def _peer_dense_kernel(ht_ref, r2_ref, e2_ref, cnt_ref, cc_ref, u_ref, vt_ref, o_ref, acc_ref):
    j = pl.program_id(1)

    @pl.when(j == 0)
    def _():
        acc_ref[...] = jnp.zeros(acc_ref.shape, F32)

    n = P_NKEYS
    zero = jnp.zeros((n, ht_ref.shape[1]), GATE_DTYPE)
    ht = ht_ref[...]
    total = None
    nsub = u_ref.shape[0] // PEER_SUB
    scores = lambda k: _mm(u_ref[k * PEER_SUB:(k + 1) * PEER_SUB, :], ht)
    zt_next = scores(0)
    for k in range(nsub):
        zt = zt_next
        if k + 1 < nsub:
            zt_next = scores(k + 1)
        rows = []
        for ii in range(PEER_SUB // n):
            i1 = k * (PEER_SUB // n) + ii
            g = None
            for h in range(P_HEADS):
                sel = r2_ref[h] < _rows_packed(cnt_ref[h, i1:i1 + 1, :], n)
                t = jnp.where(sel, e2_ref[h], zero) * _rows_packed(cc_ref[h, i1:i1 + 1, :], n)
                g = t if g is None else g + t
            z = zt[ii * n:(ii + 1) * n]
            gelu = 0.5 * z * (1.0 + lax.erf(z * math.sqrt(0.5)))
            rows.append((gelu.astype(GATE_DTYPE) * g).astype(MXU_DTYPE))
        part = _mm(vt_ref[:, k * PEER_SUB:(k + 1) * PEER_SUB], jnp.concatenate(rows, axis=0))
        total = part if total is None else total + part
    acc_ref[...] += total

    @pl.when(j == pl.num_programs(1) - 1)
    def _():
        o_ref[...] = acc_ref[...].T


def _peer_dense(ht, r2, e2, cnt, cc, u, vt):
    D, N = ht.shape
    E = u.shape[0]
    tm = _peer_token_tile(N)
    te = 8 * P_NKEYS
    gspec = pl.BlockSpec((P_HEADS, P_NKEYS, tm), lambda i, j: (0, 0, i))
    rspec = pl.BlockSpec((P_HEADS, te // P_NKEYS, tm), lambda i, j: (0, j, i))
    return pl.pallas_call(
        _peer_dense_kernel,
        grid=(N // tm, E // te),
        in_specs=[pl.BlockSpec((D, tm), lambda i, j: (0, i)), gspec, gspec, rspec, rspec,
                  pl.BlockSpec((te, D), lambda i, j: (j, 0)),
                  pl.BlockSpec((D, te), lambda i, j: (0, j))],
        out_specs=pl.BlockSpec((tm, D), lambda i, j: (i, 0)),
        out_shape=jax.ShapeDtypeStruct((N, D), F32),
        scratch_shapes=[pltpu.VMEM((D, tm), F32)],
        compiler_params=_cparams("parallel", "arbitrary"),
        name="peer_dense",
    )(ht, r2, e2, cnt, cc, u, vt)


def _final_kernel(xs_ref, pp_ref, g_ref, o_ref):
    o_ref[0] = xs_ref[0] + g_ref[...] * pp_ref[0]


def _final(xs, pp, gate, L):
    B, T, D = xs.shape
    tm = ROW_TILE
    LT = L // tm
    src = lambda b, i: (b, i + LT, 0)
    return pl.pallas_call(
        _final_kernel,
        grid=(B, (T - L) // tm),
        in_specs=[pl.BlockSpec((1, tm, D), src), pl.BlockSpec((1, tm, D), src),
                  pl.BlockSpec((None, None, 1, D), lambda b, i: (b, 1, 0, 0))],
        out_specs=pl.BlockSpec((1, tm, D), lambda b, i: (b, i, 0)),
        out_shape=jax.ShapeDtypeStruct((B, T - L, D), F32),
        compiler_params=_cparams("parallel", "parallel"),
        name="final_residual",
    )(xs, pp, gate)


def _even_odd(n):
    return np.concatenate([np.arange(0, n, 2), np.arange(1, n, 2)])


def _rope_angles(S, L, rot_dim):
    pos = jnp.arange(S)
    r = (pos // GRID_W).astype(F32)
    cc = (pos % GRID_W).astype(F32)
    axis_dim = rot_dim // 2
    inv = 1.0 / (ROPE_THETA ** (jnp.arange(0, axis_dim, 2, dtype=F32) / axis_dim))
    ang = jnp.concatenate([r[:, None] * inv, cc[:, None] * inv], axis=-1)
    cos = jnp.concatenate([jnp.ones((L, rot_dim // 2), F32), jnp.cos(ang)], axis=0)
    sin = jnp.concatenate([jnp.zeros((L, rot_dim // 2), F32), jnp.sin(ang)], axis=0)
    return cos, sin


def _rope_tables_64(S, L):
    cos, sin = _rope_angles(S, L, 64)
    return jnp.tile(jnp.concatenate([cos, cos], -1), (1, 2)), jnp.tile(jnp.concatenate([-sin, sin], -1), (1, 2))


def _rope_tables_mla(S, L):
    cos, sin = _rope_angles(S, L, C_ROPE)
    T = cos.shape[0]
    one, zero = jnp.ones((T, C_NOPE), F32), jnp.zeros((T, C_NOPE), F32)
    pad1, pad0 = jnp.ones((T, LANES - C_NOPE - C_ROPE), F32), jnp.zeros((T, LANES - C_NOPE - C_ROPE), F32)
    return (jnp.concatenate([one, cos, cos, pad1], -1), jnp.concatenate([zero, -sin, sin, pad0], -1))


def _prep_even(w_in, a_qn, a_kn, b_qn, b_kn):
    sizes = (512, 512, 512, 512, 128, 128)
    off = np.concatenate([[0], np.cumsum(sizes)])
    p64 = _even_odd(64)

    def seg(k, permute):
        idx = np.arange(off[k], off[k + 1])
        if permute:
            idx = idx.reshape(-1, 64)[:, p64].reshape(-1)
        return idx

    cols = np.concatenate([seg(0, True), seg(1, True), seg(3, True), seg(4, True), seg(2, False), seg(5, False)])
    w = w_in[:, cols].astype(MXU_DTYPE)
    q_scale = A_DQK ** -0.5 * LOG2_E
    gain = jnp.concatenate([jnp.tile(a_qn[p64], 8) * q_scale, jnp.tile(a_kn[p64], 8),
                            jnp.tile(b_qn[p64], 8) * (B_DH ** -0.5), jnp.tile(b_kn[p64], 2)])[None, :]
    return w, gain


def _pad_cols(a, n):
    return jnp.concatenate([a, jnp.zeros(a.shape[:-1] + (n - a.shape[-1],), a.dtype)], axis=-1)


def _prep_odd(w_in, c_wq_up, c_wkv_up, c_qn, c_kn, d_wg_f, d_bg_f, d_wg_b, d_bg_b):
    D = w_in.shape[0]
    p32 = _even_odd(C_ROPE)
    o = np.concatenate([[0], np.cumsum((C_Q_RANK, C_KV_RANK, C_ROPE, 256, 256, 512, D_GATE_RANK, D_GATE_RANK, 512))])
    seg = lambda k: w_in[:, o[k]:o[k + 1]]
    w = jnp.concatenate([seg(0), seg(1), _pad_cols(seg(2)[:, p32], LANES), seg(3), seg(4), seg(5),
                         _pad_cols(jnp.concatenate([seg(6), seg(7)], -1), LANES), seg(8)], axis=-1).astype(MXU_DTYPE)
    hd = C_NOPE + C_ROPE
    head_perm = np.concatenate([np.arange(C_NOPE), C_NOPE + p32])
    wq = c_wq_up.reshape(C_Q_RANK, C_HEADS, hd)[:, :, head_perm]
    wq = _pad_cols(wq, LANES).reshape(C_Q_RANK, C_HEADS * LANES).astype(MXU_DTYPE)
    wkv = c_wkv_up.reshape(C_KV_RANK, C_HEADS, C_NOPE + C_DV)
    wk = _pad_cols(wkv[:, :, :C_NOPE], LANES).reshape(C_KV_RANK, C_HEADS * LANES)
    wv = wkv[:, :, C_NOPE:].reshape(C_KV_RANK, C_HEADS * C_DV)
    wkv = jnp.concatenate([wk, wv], axis=-1).astype(MXU_DTYPE)
    qn = jnp.tile(_pad_cols(c_qn[head_perm] * (hd ** -0.5 * LOG2_E), LANES), C_HEADS)[None, :]
    kn = jnp.tile(_pad_cols(c_kn[head_perm], LANES), C_HEADS)[None, :]
    n = D_HEADS * D_DK
    wg = jnp.zeros((LANES, 2 * n), F32)
    wg = wg.at[:D_GATE_RANK, :n].set(d_wg_f).at[D_GATE_RANK:2 * D_GATE_RANK, n:].set(d_wg_b).astype(MXU_DTYPE)
    bg = jnp.concatenate([d_bg_f, d_bg_b])[None, :]
    return w, wq, wkv, qn, kn, wg, bg


def kernel(x, c, ctx, c_ctx, w_mod, b_mod, norm1, norm2, e_w_in, e_w_out, a_qn, a_kn, a_lq1, a_lk1, a_lq2, a_lk2, a_gn, b_qn, b_kn, b_sink, o_w_in, o_w_out, c_gq, c_gkv, c_wq_up, c_wkv_up, c_qn, c_kn, d_wg_f, d_bg_f, d_wg_b, d_bg_b, d_gn, p_wq, p_keys, p_u, p_v):
    B, S, D = x.shape
    L = ctx.shape[1]
    depth = w_mod.shape[0]
    assert L == ROW_TILE and S % ROW_TILE == 0 and S % GRID_W == 0
    LT = L // ROW_TILE
    bf = MXU_DTYPE

    cv = jnp.concatenate([c, c_ctx[None, :], jnp.zeros((8 - B - 1, D), F32)], axis=0)
    mods = _modulation(cv, w_mod, b_mod).reshape(depth, 8, 6, D)
    mod_tab = jnp.stack([jnp.broadcast_to(mods[:, B][:, None], (depth, B, 6, D)), mods[:, :B]], axis=2)
    mod = lambda i, k: mod_tab[i, :, :, k][:, :, None, :]

    cos64, sin64 = _rope_tables_64(S, L)
    cos_c, sin_c = _rope_tables_mla(S, L)

    xs = jnp.concatenate([ctx, x], axis=1)
    prev = None
    for i in range(depth):
        j = i // 2
        if i % 2 == 0:
            lam_init = 0.8 - 0.6 * math.exp(-0.3 * i)
            w, gain = _prep_even(e_w_in[j], a_qn[j], a_kn[j], b_qn[j], b_kn[j])
            xs, (aq, ak, bq, bk, av, bv) = _inproj(
                "even", xs, prev, norm1[i][None], mod(i, 0), mod(i, 1), w,
                [(gain, "full"), (cos64, "rows"), (sin64, "rows")],
                [(512, bf, True), (512, bf, False), (512, bf, False), (128, bf, False), (512, bf, True),
                 (128, bf, False)], LT)
            lqk = jnp.stack([a_lq1[j], a_lk1[j], a_lq2[j], a_lk2[j]])
            o1 = _diff_attn(aq, ak, av, lqk, a_gn[j][None], L, lam_init)
            o2 = _win_attn(bq, bk, bv, b_sink[j], L)
            xs, ht = _outproj(xs, o1, [o2], e_w_out[j].astype(bf), mod(i, 2), norm2[i][None], mod(i, 3), mod(i, 4),
                              LT, gla=False)
        else:
            w, wq, wkv, qn, kn, wg, bg = _prep_odd(o_w_in[j], c_wq_up[j], c_wkv_up[j], c_qn[j], c_kn[j],
                                                   d_wg_f[j], d_bg_f[j], d_wg_b[j], d_bg_b[j])
            xs, (cq, ck, cvv, dq, dk, dv, lg, r) = _inproj(
                "odd", xs, prev, norm1[i][None], mod(i, 0), mod(i, 1), w,
                [(c_gq[j][None], "full"), (c_gkv[j][None], "full"), (wq, "full"), (wkv, "full"), (qn, "full"),
                 (kn, "full"), (wg, "full"), (bg, "full"), (cos_c, "rows"), (sin_c, "rows")],
                [(1024, bf, True), (1024, bf, False), (512, bf, True), (256, F32, False), (256, F32, False),
                 (512, bf, False), (512, F32, False), (512, F32, False)], LT)
            o1 = _mla_attn(cq, ck, cvv, L)
            of = _gla(dq, dk, dv, lg, 0, L)
            ob = _gla(dq, dk, dv, lg, 1, L)
            xs, ht = _outproj(xs, o1, [of, ob, r, d_gn[j][None]], o_w_out[j].astype(bf), mod(i, 2), norm2[i][None],
                              mod(i, 3), mod(i, 4), LT, gla=True)
        wqt = p_wq[i].T.astype(bf)
        keys = p_keys[i].reshape(2 * P_HEADS, P_NKEYS, P_DKEY // 2).astype(bf)
        r2, e2, cnt, cc = _peer_gate(ht, wqt, keys)
        pp = _peer_dense(ht, r2, e2, cnt, cc, p_u[i].astype(bf), p_v[i].T.astype(bf))
        prev = (pp.reshape(B, L + S, D), mod(i, 5))
    return _final(xs, prev[0], prev[1], L)
```

```python
import functools
import math

import numpy as np
import jax
import jax.numpy as jnp
from jax import lax
from jax.experimental import pallas as pl
from jax.experimental.pallas import tpu as pltpu

F32 = jnp.float32
MXU_DTYPE = jnp.bfloat16
GATE_DTYPE = jnp.bfloat16
PACKED_ROWS = 16
PEER_SUB = 256

GRID_W = 64
ROPE_THETA = 10000.0
EPS = 1e-6
NEG = -1e30
A_HEADS, A_DQK, A_DV = 4, 64, 128
B_HEADS, B_KV_HEADS, B_DH, WINDOW = 8, 2, 64, 128
C_HEADS, C_Q_RANK, C_KV_RANK, C_NOPE, C_ROPE, C_DV = 8, 256, 128, 64, 32, 64
D_HEADS, D_DK, D_DV, D_GATE_RANK, D_GATE_TAU, D_CHUNK = 4, 64, 128, 16, 16.0, 64
P_HEADS, P_NKEYS, P_DKEY, P_TOPK = 8, 128, 256, 16

LANES = 128
ROW_TILE = 256
KEY_CHUNK = 256
KEY_CHUNK_BIG = 768
QUERY_TILE = 512
LOG2_E = math.log2(math.e)
VMEM_LIMIT = 56 * 1024 * 1024


def _cparams(*sem):
    return pltpu.CompilerParams(dimension_semantics=sem, vmem_limit_bytes=VMEM_LIMIT)


def _rms(x, g):
    return x * lax.rsqrt(jnp.mean(x * x, axis=-1, keepdims=True) + EPS) * g


def _nt(a, b):
    return lax.dot_general(a, b, (((1,), (1,)), ((), ())), preferred_element_type=F32)


def _tn(a, b):
    return lax.dot_general(a, b, (((0,), (0,)), ((), ())), preferred_element_type=F32)


def _mm(a, b):
    return jnp.dot(a, b, preferred_element_type=F32)


def _mod_kernel(c_ref, w_ref, b_ref, o_ref):
    cv = c_ref[...]
    s = cv * jax.nn.sigmoid(cv)
    o_ref[0] = jnp.dot(s, w_ref[0], precision=lax.Precision.HIGHEST, preferred_element_type=F32) + b_ref[0]


def _modulation(cvecs, w_mod, b_mod):
    depth, d, n = w_mod.shape
    rows = cvecs.shape[0]
    tn = 1536
    return pl.pallas_call(
        _mod_kernel,
        grid=(depth, n // tn),
        in_specs=[pl.BlockSpec((rows, d), lambda l, j: (0, 0)),
                  pl.BlockSpec((1, d, tn), lambda l, j: (l, 0, j)),
                  pl.BlockSpec((1, 1, tn), lambda l, j: (l, 0, j))],
        out_specs=pl.BlockSpec((1, rows, tn), lambda l, j: (l, 0, j)),
        out_shape=jax.ShapeDtypeStruct((depth, rows, n), F32),
        compiler_params=_cparams("parallel", "parallel"),
        name="modulation",
    )(cvecs, w_mod, b_mod.reshape(depth, 1, n))


def _stream_prologue(has_prev, it):
    xs_ref = next(it)
    x = xs_ref[0]
    if has_prev:
        pp_ref, pg_ref = next(it), next(it)
        x = x + pg_ref[...] * pp_ref[0]
    g1_ref, sh_ref, sc_ref, w_ref = next(it), next(it), next(it), next(it)
    h = _rms(x, g1_ref[...]) * (1.0 + sc_ref[...]) + sh_ref[...]
    z = _mm(h.astype(MXU_DTYPE), w_ref[...])
    return x, z


def _norm_rope_pair64(zg, gain, cosf, sinf):
    lane = lax.broadcasted_iota(jnp.int32, zg.shape, 1)
    lo = lane < 64
    sq = zg * zg
    s_lo = jnp.sum(jnp.where(lo, sq, 0.0), axis=-1, keepdims=True)
    s_hi = jnp.sum(jnp.where(lo, 0.0, sq), axis=-1, keepdims=True)
    ms = jnp.where(lo, s_lo, s_hi) * (1.0 / 64.0)
    y = zg * lax.rsqrt(ms + EPS) * gain
    partner = jnp.where((lane & 63) < 32, pltpu.roll(y, 96, 1), pltpu.roll(y, 32, 1))
    return y * cosf + partner * sinf


def _inproj_even_kernel(has_prev, *refs):
    it = iter(refs)
    x, z = _stream_prologue(has_prev, it)
    gain_ref, cos_ref, sin_ref = next(it), next(it), next(it)
    if has_prev:
        xo_ref = next(it)
        xo_ref[0] = x
    aq_ref, ak_ref, bq_ref, bk_ref, av_ref, bv_ref = (next(it) for _ in range(6))
    cosf, sinf = cos_ref[...], sin_ref[...]
    g = 0
    for ref, ngroups, transposed in ((aq_ref, 4, True), (ak_ref, 4, False), (bq_ref, 4, False), (bk_ref, 1, False)):
        for k in range(ngroups):
            sl = slice(g * LANES, (g + 1) * LANES)
            y = _norm_rope_pair64(z[:, sl], gain_ref[:, sl], cosf, sinf)
            if transposed:
                ref[0, k * LANES:(k + 1) * LANES, :] = y.T.astype(ref.dtype)
            else:
                ref[0, :, k * LANES:(k + 1) * LANES] = y.astype(ref.dtype)
            g += 1
    o = g * LANES
    av_ref[0] = z[:, o:o + 512].T.astype(av_ref.dtype)
    bv_ref[0] = z[:, o + 512:o + 640].astype(bv_ref.dtype)


def _norm_rope_mla(y, gain, cosf, sinf):
    lane = lax.broadcasted_iota(jnp.int32, y.shape, 1)
    ms = jnp.sum(y * y, axis=-1, keepdims=True) * (1.0 / (C_NOPE + C_ROPE))
    y = y * lax.rsqrt(ms + EPS) * gain
    partner = jnp.where(lane < C_NOPE + C_ROPE // 2, pltpu.roll(y, LANES - C_ROPE // 2, 1),
                        pltpu.roll(y, C_ROPE // 2, 1))
    return y * cosf + partner * sinf


def _log_sigmoid(x):
    return jnp.minimum(x, 0.0) - jnp.log1p(jnp.exp(-jnp.abs(x)))


def _inproj_odd_kernel(has_prev, *refs):
    it = iter(refs)
    x, z = _stream_prologue(has_prev, it)
    (gq_ref, gkv_ref, wq_ref, wkv_ref, qn_ref, kn_ref, wg_ref, bg_ref, cos_ref, sin_ref) = (next(it) for _ in range(10))
    if has_prev:
        xo_ref = next(it)
        xo_ref[0] = x
    cq_ref, ck_ref, cv_ref, dq_ref, dk_ref, dv_ref, lg_ref, r_ref = (next(it) for _ in range(8))
    cosf, sinf = cos_ref[...], sin_ref[...]
    qn = _rms(z[:, 0:256], gq_ref[...])
    q_up = _mm(qn.astype(MXU_DTYPE), wq_ref[...])
    kvn = _rms(z[:, 256:384], gkv_ref[...])
    kv_up = _mm(kvn.astype(MXU_DTYPE), wkv_ref[...])
    kpe = pltpu.roll(z[:, 384:512], C_NOPE, 1)
    for h in range(C_HEADS):
        sl = slice(h * LANES, (h + 1) * LANES)
        cq_ref[0, sl, :] = _norm_rope_mla(q_up[:, sl], qn_ref[:, sl], cosf, sinf).T.astype(cq_ref.dtype)
        ck_ref[0, :, sl] = _norm_rope_mla(kv_up[:, sl] + kpe, kn_ref[:, sl], cosf, sinf).astype(ck_ref.dtype)
    cv_ref[0] = kv_up[:, C_HEADS * LANES:].T.astype(cv_ref.dtype)
    dq_ref[0] = z[:, 512:768] * (D_DK ** -0.5)
    dk_ref[0] = z[:, 768:1024]
    dv_ref[0] = z[:, 1024:1536].astype(dv_ref.dtype)
    gl = _mm(z[:, 1536:1664].astype(MXU_DTYPE), wg_ref[...]) + bg_ref[...]
    lg_ref[0] = _log_sigmoid(gl) * (1.0 / D_GATE_TAU)
    r_ref[0] = z[:, 1664:2176]


def _inproj(kind, xs, prev, norm_g, shift, scale, w, extra, out_cols, LT):
    B, T, D = xs.shape
    tm = ROW_TILE
    nT = T // tm
    has_prev = prev is not None
    row = lambda b, i: (b, i, 0)
    cls = lambda b, i: (b, jnp.where(i >= LT, 1, 0), 0, 0)
    full2 = lambda b, i: (0, 0)
    args, specs = [xs], [pl.BlockSpec((1, tm, D), row)]
    if has_prev:
        args += [prev[0], prev[1]]
        specs += [pl.BlockSpec((1, tm, D), row), pl.BlockSpec((None, None, 1, D), cls)]
    args += [norm_g, shift, scale, w]
    specs += [pl.BlockSpec((1, D), full2), pl.BlockSpec((None, None, 1, D), cls),
              pl.BlockSpec((None, None, 1, D), cls), pl.BlockSpec(w.shape, full2)]
    for arr, k in extra:
        args.append(arr)
        if k == "full":
            specs.append(pl.BlockSpec(arr.shape, full2))
        else:
            specs.append(pl.BlockSpec((tm, arr.shape[1]), lambda b, i: (i, 0)))
    out_shape, out_specs = [], []
    if has_prev:
        out_shape.append(jax.ShapeDtypeStruct((B, T, D), F32))
        out_specs.append(pl.BlockSpec((1, tm, D), row))
    for n, dt, transposed in out_cols:
        if transposed:
            out_shape.append(jax.ShapeDtypeStruct((B, n, T), dt))
            out_specs.append(pl.BlockSpec((1, n, tm), lambda b, i: (b, 0, i)))
        else:
            out_shape.append(jax.ShapeDtypeStruct((B, T, n), dt))
            out_specs.append(pl.BlockSpec((1, tm, n), row))
    body = _inproj_even_kernel if kind == "even" else _inproj_odd_kernel
    outs = pl.pallas_call(
        functools.partial(body, has_prev),
        grid=(B, nT), in_specs=specs, out_specs=out_specs, out_shape=out_shape,
        compiler_params=_cparams("parallel", "parallel"),
        name="inproj_" + kind,
    )(*args)
    if has_prev:
        return outs[0], outs[1:]
    return xs, outs


def _flash_streams(ctx, q_streams, k_cols, v_rows, k_ref, vt_ref, ml_ref, acc_ref, sc_ref=None, p_ref=None):
    T = k_ref.shape[1]
    tq = q_streams[0].shape[1]
    acc_ref[...] = jnp.zeros(acc_ref.shape, F32)
    init = (jnp.full((1, tq), -jnp.inf, F32), jnp.zeros((1, tq), F32)) * 2

    def chunk(start, size, ml):
        out = []
        scs = [_mm(k_ref[0, start:start + size, k_cols[s]], q_streams[s]) for s in range(2)]
        for s in range(2):
            m_prev, l_prev = ml[2 * s], ml[2 * s + 1]
            m_new = jnp.maximum(m_prev, jnp.max(scs[s], axis=0, keepdims=True))
            alpha = jnp.exp2(m_prev - m_new)
            p = jnp.exp2(scs[s] - m_new)
            acc_ref[s] = alpha * acc_ref[s] + _mm(vt_ref[0, v_rows[s], start:start + size], p.astype(MXU_DTYPE))
            out += [m_new, alpha * l_prev + jnp.sum(p, axis=0, keepdims=True)]
        return tuple(out)

    if ctx:
        ml = init
        for c in range(0, T, KEY_CHUNK):
            ml = chunk(c, KEY_CHUNK, ml)
        ml_ref[...] = jnp.concatenate(ml, axis=0)
        return

    big = sc_ref.shape[1]
    piece = KEY_CHUNK
    pieces = big // piece
    n_chunks = T // big

    def piece_scores(c, j, s):
        start = pl.multiple_of(c * big + j * piece, piece)
        return _mm(k_ref[0, pl.ds(start, piece), k_cols[s]], q_streams[s])

    def col_max(parts):
        return functools.reduce(jnp.maximum, [jnp.max(x, axis=0, keepdims=True) for x in parts])

    def value_product(c, s):
        start = pl.multiple_of(c * big, big)
        return _mm(vt_ref[0, v_rows[s], pl.ds(start, big)], p_ref[s])

    def pipelined(t, carry, look_ahead=True):
        ml, mx, alpha_prev = carry
        pv = [value_product(jnp.maximum(t - 1, 0), s) for s in range(2)]
        m_new = [jnp.maximum(ml[2 * s], mx[s]) for s in range(2)]
        alpha = [jnp.exp2(ml[2 * s] - m_new[s]) for s in range(2)]
        ahead, sums = ([], []), ([], [])
        for j in range(pieces):
            rows = slice(j * piece, (j + 1) * piece)
            if look_ahead:
                for s in range(2):
                    ahead[s].append(piece_scores(t + 1, j, s))
            for s in range(2):
                p = jnp.exp2(sc_ref[s, rows, :] - m_new[s])
                sums[s].append(jnp.sum(p, axis=0, keepdims=True))
                p_ref[s, rows, :] = p.astype(p_ref.dtype)
            if look_ahead:
                for s in range(2):
                    sc_ref[s, rows, :] = ahead[s][j]
        new_ml = []
        for s in range(2):
            acc_ref[s] = alpha_prev[s] * acc_ref[s] + pv[s]
            new_ml += [m_new[s], alpha[s] * ml[2 * s + 1] + functools.reduce(jnp.add, sums[s])]
        mx_next = tuple(col_max(ahead[s]) for s in range(2)) if look_ahead else mx
        return tuple(new_ml), mx_next, tuple(alpha)

    first = []
    for s in range(2):
        parts = [piece_scores(0, j, s) for j in range(pieces)]
        for j in range(pieces):
            sc_ref[s, j * piece:(j + 1) * piece, :] = parts[j]
        first.append(col_max(parts))
    p_ref[...] = jnp.zeros(p_ref.shape, p_ref.dtype)
    one = jnp.ones((1, tq), F32)
    carry = lax.fori_loop(0, n_chunks - 1, pipelined, (init, tuple(first), (one, one)))
    ml, _, alpha_last = pipelined(n_chunks - 1, carry, look_ahead=False)
    for s in range(2):
        acc_ref[s] = alpha_last[s] * acc_ref[s] + value_product(n_chunks - 1, s)
    ml_ref[...] = jnp.concatenate(ml, axis=0)


def _query_tile(n_q):
    return QUERY_TILE if n_q % QUERY_TILE == 0 else ROW_TILE


def _big_chunk(T):
    return KEY_CHUNK_BIG if T % KEY_CHUNK_BIG == 0 else KEY_CHUNK


def _flash_scratch(ctx, T, dv, tq):
    scratch = [pltpu.VMEM((4, tq), F32), pltpu.VMEM((2, dv, tq), F32)]
    if not ctx:
        scratch += [pltpu.VMEM((2, _big_chunk(T), tq), F32), pltpu.VMEM((2, _big_chunk(T), tq), MXU_DTYPE)]
    return scratch


def _split_queries(qt, L):
    return qt[:, :, :L], qt[:, :, L:]


def _diff_attn_kernel(ctx, lam_init, q_ref, k_ref, vt_ref, lqk_ref, gn_ref, o_ref, ml_ref, acc_ref, *pipe):
    q = q_ref[0]
    row = lax.broadcasted_iota(jnp.int32, q.shape, 0)
    zero = jnp.zeros_like(q)
    streams = (jnp.where(row < A_DQK, q, zero), jnp.where(row < A_DQK, zero, q))
    every = slice(None)
    _flash_streams(ctx, streams, (every, every), (every, every), k_ref, vt_ref, ml_ref, acc_ref, *pipe)
    lqk = lqk_ref[...]
    lam = (jnp.exp(jnp.sum(lqk[0:1] * lqk[1:2], axis=-1, keepdims=True))
           - jnp.exp(jnp.sum(lqk[2:3] * lqk[3:4], axis=-1, keepdims=True)) + lam_init)
    o = acc_ref[0] / ml_ref[1:2, :] - lam * (acc_ref[1] / ml_ref[3:4, :])
    o = o * lax.rsqrt(jnp.mean(o * o, axis=0, keepdims=True) + EPS) * gn_ref[...]
    o_ref[0] = (o * (1.0 - lam_init)).astype(o_ref.dtype)


def _diff_attn(qt, k, vt, lqk, gn, L, lam_init):
    B, T, _ = k.shape
    assert L % KEY_CHUNK == 0 and T % KEY_CHUNK == 0

    def call(ctx, q, keys, tq):
        n_q = q.shape[2]
        gn_b = jnp.broadcast_to(gn.reshape(A_DV, 1), (A_DV, tq))
        return pl.pallas_call(
            functools.partial(_diff_attn_kernel, ctx, lam_init),
            grid=(B, A_HEADS, n_q // tq),
            in_specs=[pl.BlockSpec((1, LANES, tq), lambda b, h, i: (b, h, i)),
                      pl.BlockSpec((1, keys, LANES), lambda b, h, i: (b, 0, h)),
                      pl.BlockSpec((1, A_DV, keys), lambda b, h, i: (b, h, 0)),
                      pl.BlockSpec(lqk.shape, lambda b, h, i: (0, 0)),
                      pl.BlockSpec(gn_b.shape, lambda b, h, i: (0, 0))],
            out_specs=pl.BlockSpec((1, A_DV, tq), lambda b, h, i: (b, h, i)),
            out_shape=jax.ShapeDtypeStruct((B, A_HEADS * A_DV, n_q), MXU_DTYPE),
            scratch_shapes=_flash_scratch(ctx, keys, A_DV, tq),
            compiler_params=_cparams("parallel", "parallel", "parallel"),
            name="diff_attn_ctx" if ctx else "diff_attn",
        )(q, k, vt, lqk, gn_b)

    q_ctx, q_lat = _split_queries(qt, L)
    return jnp.concatenate([call(True, q_ctx, L, ROW_TILE), call(False, q_lat, T, _query_tile(T - L))], axis=2)


def _mla_attn_kernel(ctx, q_ref, k_ref, vt_ref, o_ref, ml_ref, acc_ref, *pipe):
    streams = (q_ref[0, :LANES, :], q_ref[0, LANES:, :])
    k_cols = (slice(0, LANES), slice(LANES, 2 * LANES))
    v_rows = (slice(0, C_DV), slice(C_DV, 2 * C_DV))
    _flash_streams(ctx, streams, k_cols, v_rows, k_ref, vt_ref, ml_ref, acc_ref, *pipe)
    o_ref[0] = jnp.concatenate([acc_ref[0] / ml_ref[1:2, :], acc_ref[1] / ml_ref[3:4, :]], axis=0).astype(o_ref.dtype)


def _mla_attn(qt, k, vt, L):
    B, T, _ = k.shape
    assert L % KEY_CHUNK == 0 and T % KEY_CHUNK == 0

    def call(ctx, q, keys, tq):
        n_q = q.shape[2]
        return pl.pallas_call(
            functools.partial(_mla_attn_kernel, ctx),
            grid=(B, C_HEADS // 2, n_q // tq),
            in_specs=[pl.BlockSpec((1, 2 * LANES, tq), lambda b, h, i: (b, h, i)),
                      pl.BlockSpec((1, keys, 2 * LANES), lambda b, h, i: (b, 0, h)),
                      pl.BlockSpec((1, 2 * C_DV, keys), lambda b, h, i: (b, h, 0))],
            out_specs=pl.BlockSpec((1, 2 * C_DV, tq), lambda b, h, i: (b, h, i)),
            out_shape=jax.ShapeDtypeStruct((B, C_HEADS * C_DV, n_q), MXU_DTYPE),
            scratch_shapes=_flash_scratch(ctx, keys, C_DV, tq),
            compiler_params=_cparams("parallel", "parallel", "parallel"),
            name="mla_attn_ctx" if ctx else "mla_attn",
        )(q, k, vt)

    q_ctx, q_lat = _split_queries(qt, L)
    return jnp.concatenate([call(True, q_ctx, L, ROW_TILE), call(False, q_lat, T, _query_tile(T - L))], axis=2)


def _win_attn_kernel(LT, sink_ref, q_ref, kc_ref, kp_ref, kk_ref, kn_ref, vc_ref, vp_ref, vk_ref, vn_ref, o_ref):
    i, nT = pl.program_id(1), pl.num_programs(1)
    tq = q_ref.shape[1]
    tk = kc_ref.shape[1]
    r = lax.broadcasted_iota(jnp.int32, (2 * tq, tk), 0)
    c = lax.broadcasted_iota(jnp.int32, (2 * tq, tk), 1)
    d = c - jnp.where(r >= tq, r - tq, r)
    is_x = i >= LT
    far = 4 * tk
    ok_p = d >= jnp.where(jnp.logical_and(is_x, i - 1 >= LT), tk - WINDOW, far)
    ok_k = jnp.abs(d) <= jnp.where(is_x, WINDOW, -1)
    ok_n = d <= jnp.where(jnp.logical_and(is_x, i + 1 <= nT - 1), WINDOW - tk, -far)
    k_tiles = (kc_ref[0], kp_ref[0], kk_ref[0], kn_ref[0])
    masks = (None, ok_p, ok_k, ok_n)
    vals = jnp.concatenate([vc_ref[0], vp_ref[0], vk_ref[0], vn_ref[0]], axis=0)
    lane = lax.broadcasted_iota(jnp.int32, (tq, LANES), 1)
    lo = lane < B_DH
    rows = lax.broadcasted_iota(jnp.int32, (2 * tq, 1), 0)
    group = B_HEADS // B_KV_HEADS
    for g in range(B_HEADS // 2):
        kv = (2 * g) // group
        qg = q_ref[0, :, g * LANES:(g + 1) * LANES].astype(F32)
        qr = pltpu.roll(qg, B_DH, 1)
        if kv == 0:
            qa, qb = jnp.where(lo, qg, 0.0), jnp.where(lo, qr, 0.0)
        else:
            qa, qb = jnp.where(lo, 0.0, qr), jnp.where(lo, 0.0, qg)
        qs = jnp.concatenate([qa, qb], axis=0).astype(MXU_DTYPE)
        s = jnp.concatenate([_nt(qs, kt) if ok is None else jnp.where(ok, _nt(qs, kt), NEG)
                             for kt, ok in zip(k_tiles, masks)], axis=1)
        sk = jnp.where(rows < tq, sink_ref[2 * g], sink_ref[2 * g + 1])
        m = jnp.maximum(jnp.max(s, axis=-1, keepdims=True), sk)
        p = jnp.exp(s - m)
        l = jnp.sum(p, axis=-1, keepdims=True) + jnp.exp(sk - m)
        acc = _mm(p.astype(MXU_DTYPE), vals) / l
        if kv == 0:
            out = jnp.where(lo, acc[:tq], pltpu.roll(acc[tq:], B_DH, 1))
        else:
            out = jnp.where(lo, pltpu.roll(acc[:tq], B_DH, 1), acc[tq:])
        o_ref[0, :, g * LANES:(g + 1) * LANES] = out.astype(o_ref.dtype)


def _win_attn(q, k, v, sink, L):
    B, T, _ = q.shape
    t = ROW_TILE
    assert L == t and WINDOW <= t
    nT, LT = T // t, L // t
    kspec = lambda f: pl.BlockSpec((1, t, LANES), f)
    maps = [lambda b, i: (b, 0, 0),
            lambda b, i: (b, jnp.maximum(i - 1, LT), 0),
            lambda b, i: (b, i, 0),
            lambda b, i: (b, jnp.minimum(i + 1, nT - 1), 0)]
    return pl.pallas_call(
        functools.partial(_win_attn_kernel, LT),
        grid=(B, nT),
        in_specs=[pl.BlockSpec(memory_space=pltpu.SMEM),
                  pl.BlockSpec((1, t, B_HEADS * B_DH), lambda b, i: (b, i, 0))]
                 + [kspec(f) for f in maps] + [kspec(f) for f in maps],
        out_specs=pl.BlockSpec((1, t, B_HEADS * B_DH), lambda b, i: (b, i, 0)),
        out_shape=jax.ShapeDtypeStruct((B, T, B_HEADS * B_DH), MXU_DTYPE),
        compiler_params=_cparams("parallel", "parallel"),
        name="window_attn",
    )(sink, q, k, k, k, k, v, v, v, v)


def _gla_kernel(reverse, q_ref, k_ref, v_ref, g_ref, o_ref, st_ref):
    @pl.when(pl.program_id(1) == 0)
    def _():
        st_ref[...] = jnp.zeros(st_ref.shape, F32)

    C = D_CHUNK
    tb = q_ref.shape[1]
    r = lax.broadcasted_iota(jnp.int32, (C, C), 0)
    c = lax.broadcasted_iota(jnp.int32, (C, C), 1)
    keep = (c >= r) if reverse else (c <= r)
    tri = jnp.where(keep, 1.0, 0.0).astype(F32)
    chunks = range(tb // C)
    for ch in (reversed(chunks) if reverse else chunks):
        rows = slice(ch * C, (ch + 1) * C)
        b = jnp.dot(tri, g_ref[0, rows, :], precision=lax.Precision.HIGHEST, preferred_element_type=F32)
        b_last = b[0:1] if reverse else b[C - 1:C]
        q_t = q_ref[0, rows, :] * jnp.exp(b)
        kk = k_ref[0, rows, :]
        k_in = kk * jnp.exp(-b)
        k_st = kk * jnp.exp(b_last - b)
        decay = jnp.exp(b_last)
        for h in range(D_HEADS):
            sl = slice(h * D_DK, (h + 1) * D_DK)
            vsl = slice(h * D_DV, (h + 1) * D_DV)
            qh = q_t[:, sl].astype(MXU_DTYPE)
            a = jnp.where(keep, _nt(qh, k_in[:, sl].astype(MXU_DTYPE)), 0.0)
            vh = v_ref[0, rows, vsl]
            st = st_ref[h]
            o_ref[0, rows, vsl] = _mm(a.astype(MXU_DTYPE), vh) + _nt(qh, st.astype(MXU_DTYPE))
            st_ref[h] = st * decay[:, sl] + _tn(vh, k_st[:, sl].astype(MXU_DTYPE))


def _gla(q, k, v, lg, direction, L):
    B, T, _ = q.shape
    tb = ROW_TILE
    nB, LB = T // tb, L // tb
    reverse = direction == 1
    if reverse:
        blk = lambda s: jnp.where(s < LB, LB - 1 - s, LB + nB - 1 - s)
    else:
        blk = lambda s: s
    return pl.pallas_call(
        functools.partial(_gla_kernel, reverse),
        grid=(B, nB),
        in_specs=[pl.BlockSpec((1, tb, D_HEADS * D_DK), lambda b, s: (b, blk(s), 0)),
                  pl.BlockSpec((1, tb, D_HEADS * D_DK), lambda b, s: (b, blk(s), 0)),
                  pl.BlockSpec((1, tb, D_HEADS * D_DV), lambda b, s: (b, blk(s), 0)),
                  pl.BlockSpec((1, tb, D_HEADS * D_DK), lambda b, s: (b, blk(s), direction))],
        out_specs=pl.BlockSpec((1, tb, D_HEADS * D_DV), lambda b, s: (b, blk(s), 0)),
        out_shape=jax.ShapeDtypeStruct((B, T, D_HEADS * D_DV), F32),
        scratch_shapes=[pltpu.VMEM((D_HEADS, D_DV, D_DK), F32)],
        compiler_params=_cparams("parallel", "arbitrary"),
        name="gla_bwd" if reverse else "gla_fwd",
    )(q, k, v, lg)


def _outproj_kernel(gla, *refs):
    it = iter(refs)
    xs_ref, o1_ref = next(it), next(it)
    if gla:
        of_ref, ob_ref, r_ref, dgn_ref = (next(it) for _ in range(4))
        og = of_ref[0] + ob_ref[0]
        rr = r_ref[0]
        parts = []
        for h in range(D_HEADS):
            sl = slice(h * D_DV, (h + 1) * D_DV)
            rh = rr[:, sl]
            parts.append((_rms(og[:, sl], dgn_ref[...]) * (rh * jax.nn.sigmoid(rh))).astype(MXU_DTYPE))
        o2 = jnp.concatenate(parts, axis=1)
    else:
        o2 = next(it)[0]
    w_ref, gate_ref, n2_ref, sh_ref, sc_ref, xo_ref, ht_ref = (next(it) for _ in range(7))
    half = o1_ref.shape[1]
    x = xs_ref[0] + gate_ref[...] * (_tn(o1_ref[0], w_ref[:half, :]) + _mm(o2, w_ref[half:, :]))
    xo_ref[0] = x
    h2 = _rms(x, n2_ref[...]) * (1.0 + sc_ref[...]) + sh_ref[...]
    ht_ref[...] = h2.T.astype(ht_ref.dtype)


def _outproj(xs, o1, o2s, w, gate, norm_g, shift, scale, LT, gla):
    B, T, D = xs.shape
    tm = ROW_TILE
    nT = T // tm
    row = lambda b, i: (b, i, 0)
    cls = lambda b, i: (b, jnp.where(i >= LT, 1, 0), 0, 0)
    full2 = lambda b, i: (0, 0)
    args = [xs, o1]
    specs = [pl.BlockSpec((1, tm, D), row), pl.BlockSpec((1, o1.shape[1], tm), lambda b, i: (b, 0, i))]
    for a in o2s:
        args.append(a)
        specs.append(pl.BlockSpec(a.shape, full2) if a.ndim == 2 else pl.BlockSpec((1, tm, a.shape[2]), row))
    args += [w, gate, norm_g, shift, scale]
    specs += [pl.BlockSpec(w.shape, full2), pl.BlockSpec((None, None, 1, D), cls), pl.BlockSpec((1, D), full2),
              pl.BlockSpec((None, None, 1, D), cls), pl.BlockSpec((None, None, 1, D), cls)]
    return pl.pallas_call(
        functools.partial(_outproj_kernel, gla),
        grid=(B, nT), in_specs=specs,
        out_specs=[pl.BlockSpec((1, tm, D), row), pl.BlockSpec((D, tm), lambda b, i: (0, b * nT + i))],
        out_shape=[jax.ShapeDtypeStruct((B, T, D), F32), jax.ShapeDtypeStruct((D, B * T), MXU_DTYPE)],
        compiler_params=_cparams("parallel", "parallel"),
        name="outproj_gla" if gla else "outproj",
    )(*args)


def _top_values(s, k, with_rank=False):
    vals = []
    rank = jnp.full(s.shape, float(k), F32) if with_rank else None
    for a in range(k):
        m = jnp.max(s, axis=0, keepdims=True)
        vals.append(m)
        hit = s == m
        if with_rank:
            rank = jnp.where(hit, float(a), rank)
        if a + 1 < k:
            s = jnp.where(hit, -jnp.inf, s)
    vals = jnp.concatenate(vals, axis=0)
    return (vals, rank) if with_rank else vals


def _merge_exchanges(lo, n, r):
    step = r * 2
    if step < n:
        pairs = _merge_exchanges(lo, n, step) + _merge_exchanges(lo + r, n, step)
        return pairs + [(i, i + r) for i in range(lo + r, lo + n - r, step)]
    return [(lo, lo + r)]


def _sort_exchanges(lo, n):
    if n <= 1:
        return []
    half = n // 2
    return _sort_exchanges(lo, half) + _sort_exchanges(lo + half, half) + _merge_exchanges(lo, n, 1)


def _top_values_sorted(s, k):
    slabs = [s[8 * r:8 * (r + 1)] for r in range(s.shape[0] // 8)]
    assert len(slabs) == k
    for i, j in _sort_exchanges(0, k):
        slabs[i], slabs[j] = jnp.maximum(slabs[i], slabs[j]), jnp.minimum(slabs[i], slabs[j])
    vals = []
    for a in range(k):
        m = jnp.max(slabs[0], axis=0, keepdims=True)
        vals.append(m)
        hit = slabs[0] == m
        for r in range(k - a - 1):
            slabs[r] = jnp.where(hit, slabs[r + 1], slabs[r])
    return jnp.concatenate(vals, axis=0)


def _peer_gate_kernel(ht_ref, wqt_ref, keys_ref, r2_ref, e2_ref, cnt_ref, cc_ref):
    qt = _mm(wqt_ref[...], ht_ref[...])
    n, k = P_NKEYS, P_TOPK
    s1 = _mm(keys_ref[0], qt[:n].astype(MXU_DTYPE))
    s2 = _mm(keys_ref[1], qt[n:].astype(MXU_DTYPE))
    sv1 = _top_values_sorted(s1, k)
    sv2, rank2 = _top_values(s2, k, with_rank=True)
    cand = jnp.concatenate([sv1[a:a + 1] + sv2[:k // (a + 1)] for a in range(k)], axis=0)
    tau = _top_values(cand, k)[k - 1:k]
    top = sv1[0:1] + sv2[0:1]
    z = jnp.sum(jnp.where(cand >= tau, jnp.exp(cand - top), 0.0), axis=0, keepdims=True)
    cnt = jnp.zeros(s1.shape, F32)
    for b in range(k // 2):
        cnt = cnt + jnp.where(s1 + sv2[b:b + 1] >= tau, 1.0, 0.0)
    extra = functools.reduce(jnp.add, [jnp.where(sv1[0:1] + sv2[b:b + 1] >= tau, 1.0, 0.0) for b in range(k // 2, k)])
    cnt = cnt + jnp.where(s1 == sv1[0:1], extra, 0.0)
    r2_ref[0] = rank2.astype(r2_ref.dtype)
    e2_ref[0] = jnp.exp(s2 - sv2[0:1]).astype(e2_ref.dtype)
    cnt_ref[0] = cnt
    cc_ref[0] = jnp.exp(s1 - sv1[0:1]) / z


def _peer_token_tile(n_tok):
    return 512 if n_tok % 512 == 0 else 256


def _peer_gate(ht, wqt, keys):
    D, N = ht.shape
    tm = _peer_token_tile(N)
    shape = lambda dt: jax.ShapeDtypeStruct((P_HEADS, P_NKEYS, N), dt)
    ospec = pl.BlockSpec((1, P_NKEYS, tm), lambda i, h: (h, 0, i))
    return pl.pallas_call(
        _peer_gate_kernel,
        grid=(N // tm, P_HEADS),
        in_specs=[pl.BlockSpec((D, tm), lambda i, h: (0, i)),
                  pl.BlockSpec((P_DKEY, D), lambda i, h: (h, 0)),
                  pl.BlockSpec((2, P_NKEYS, P_DKEY // 2), lambda i, h: (h, 0, 0))],
        out_specs=[ospec] * 4,
        out_shape=[shape(GATE_DTYPE), shape(GATE_DTYPE), shape(F32), shape(F32)],
        compiler_params=_cparams("parallel", "arbitrary"),
        name="peer_gate",
    )(ht, wqt, keys)


def _rows_packed(row, n):
    tile = jnp.broadcast_to(row, (PACKED_ROWS, row.shape[1])).astype(GATE_DTYPE)
    return jnp.concatenate([tile] * (n // PACKED_ROWS), axis=0)


def _peer_dense_kernel(ht_ref, r2_ref, e2_ref, cnt_ref, cc_ref, u_ref, vt_ref, o_ref, acc_ref):
    j = pl.program_id(1)

    @pl.when(j == 0)
    def _():
        acc_ref[...] = jnp.zeros(acc_ref.shape, F32)

    n = P_NKEYS
    zero = jnp.zeros((n, ht_ref.shape[1]), GATE_DTYPE)
    ht = ht_ref[...]
    total = None
    nsub = u_ref.shape[0] // PEER_SUB
    scores = lambda k: _mm(u_ref[k * PEER_SUB:(k + 1) * PEER_SUB, :], ht)
    zt_next = scores(0)
    for k in range(nsub):
        zt = zt_next
        if k + 1 < nsub:
            zt_next = scores(k + 1)
        rows = []
        for ii in range(PEER_SUB // n):
            i1 = k * (PEER_SUB // n) + ii
            g = None
            for h in range(P_HEADS):
                sel = r2_ref[h] < _rows_packed(cnt_ref[h, i1:i1 + 1, :], n)
                t = jnp.where(sel, e2_ref[h], zero) * _rows_packed(cc_ref[h, i1:i1 + 1, :], n)
                g = t if g is None else g + t
            z = zt[ii * n:(ii + 1) * n]
            gelu = 0.5 * z * (1.0 + lax.erf(z * math.sqrt(0.5)))
            rows.append((gelu.astype(GATE_DTYPE) * g).astype(MXU_DTYPE))
        part = _mm(vt_ref[:, k * PEER_SUB:(k + 1) * PEER_SUB], jnp.concatenate(rows, axis=0))
        total = part if total is None else total + part
    acc_ref[...] += total

    @pl.when(j == pl.num_programs(1) - 1)
    def _():
        o_ref[...] = acc_ref[...].T


def _peer_dense(ht, r2, e2, cnt, cc, u, vt):
    D, N = ht.shape
    E = u.shape[0]
    tm = _peer_token_tile(N)
    te = 8 * P_NKEYS
    gspec = pl.BlockSpec((P_HEADS, P_NKEYS, tm), lambda i, j: (0, 0, i))
    rspec = pl.BlockSpec((P_HEADS, te // P_NKEYS, tm), lambda i, j: (0, j, i))
    return pl.pallas_call(
        _peer_dense_kernel,
        grid=(N // tm, E // te),
        in_specs=[pl.BlockSpec((D, tm), lambda i, j: (0, i)), gspec, gspec, rspec, rspec,
                  pl.BlockSpec((te, D), lambda i, j: (j, 0)),
                  pl.BlockSpec((D, te), lambda i, j: (0, j))],
        out_specs=pl.BlockSpec((tm, D), lambda i, j: (i, 0)),
        out_shape=jax.ShapeDtypeStruct((N, D), F32),
        scratch_shapes=[pltpu.VMEM((D, tm), F32)],
        compiler_params=_cparams("parallel", "arbitrary"),
        name="peer_dense",
    )(ht, r2, e2, cnt, cc, u, vt)


def _final_kernel(xs_ref, pp_ref, g_ref, o_ref):
    o_ref[0] = xs_ref[0] + g_ref[...] * pp_ref[0]


def _final(xs, pp, gate, L):
    B, T, D = xs.shape
    tm = ROW_TILE
    LT = L // tm
    src = lambda b, i: (b, i + LT, 0)
    return pl.pallas_call(
        _final_kernel,
        grid=(B, (T - L) // tm),
        in_specs=[pl.BlockSpec((1, tm, D), src), pl.BlockSpec((1, tm, D), src),
                  pl.BlockSpec((None, None, 1, D), lambda b, i: (b, 1, 0, 0))],
        out_specs=pl.BlockSpec((1, tm, D), lambda b, i: (b, i, 0)),
        out_shape=jax.ShapeDtypeStruct((B, T - L, D), F32),
        compiler_params=_cparams("parallel", "parallel"),
        name="final_residual",
    )(xs, pp, gate)


def _even_odd(n):
    return np.concatenate([np.arange(0, n, 2), np.arange(1, n, 2)])


def _rope_angles(S, L, rot_dim):
    pos = jnp.arange(S)
    r = (pos // GRID_W).astype(F32)
    cc = (pos % GRID_W).astype(F32)
    axis_dim = rot_dim // 2
    inv = 1.0 / (ROPE_THETA ** (jnp.arange(0, axis_dim, 2, dtype=F32) / axis_dim))
    ang = jnp.concatenate([r[:, None] * inv, cc[:, None] * inv], axis=-1)
    cos = jnp.concatenate([jnp.ones((L, rot_dim // 2), F32), jnp.cos(ang)], axis=0)
    sin = jnp.concatenate([jnp.zeros((L, rot_dim // 2), F32), jnp.sin(ang)], axis=0)
    return cos, sin


def _rope_tables_64(S, L):
    cos, sin = _rope_angles(S, L, 64)
    return jnp.tile(jnp.concatenate([cos, cos], -1), (1, 2)), jnp.tile(jnp.concatenate([-sin, sin], -1), (1, 2))


def _rope_tables_mla(S, L):
    cos, sin = _rope_angles(S, L, C_ROPE)
    T = cos.shape[0]
    one, zero = jnp.ones((T, C_NOPE), F32), jnp.zeros((T, C_NOPE), F32)
    pad1, pad0 = jnp.ones((T, LANES - C_NOPE - C_ROPE), F32), jnp.zeros((T, LANES - C_NOPE - C_ROPE), F32)
    return (jnp.concatenate([one, cos, cos, pad1], -1), jnp.concatenate([zero, -sin, sin, pad0], -1))


def _prep_even(w_in, a_qn, a_kn, b_qn, b_kn):
    sizes = (512, 512, 512, 512, 128, 128)
    off = np.concatenate([[0], np.cumsum(sizes)])
    p64 = _even_odd(64)

    def seg(k, permute):
        idx = np.arange(off[k], off[k + 1])
        if permute:
            idx = idx.reshape(-1, 64)[:, p64].reshape(-1)
        return idx

    cols = np.concatenate([seg(0, True), seg(1, True), seg(3, True), seg(4, True), seg(2, False), seg(5, False)])
    w = w_in[:, cols].astype(MXU_DTYPE)
    q_scale = A_DQK ** -0.5 * LOG2_E
    gain = jnp.concatenate([jnp.tile(a_qn[p64], 8) * q_scale, jnp.tile(a_kn[p64], 8),
                            jnp.tile(b_qn[p64], 8) * (B_DH ** -0.5), jnp.tile(b_kn[p64], 2)])[None, :]
    return w, gain


def _pad_cols(a, n):
    return jnp.concatenate([a, jnp.zeros(a.shape[:-1] + (n - a.shape[-1],), a.dtype)], axis=-1)


def _prep_odd(w_in, c_wq_up, c_wkv_up, c_qn, c_kn, d_wg_f, d_bg_f, d_wg_b, d_bg_b):
    D = w_in.shape[0]
    p32 = _even_odd(C_ROPE)
    o = np.concatenate([[0], np.cumsum((C_Q_RANK, C_KV_RANK, C_ROPE, 256, 256, 512, D_GATE_RANK, D_GATE_RANK, 512))])
    seg = lambda k: w_in[:, o[k]:o[k + 1]]
    w = jnp.concatenate([seg(0), seg(1), _pad_cols(seg(2)[:, p32], LANES), seg(3), seg(4), seg(5),
                         _pad_cols(jnp.concatenate([seg(6), seg(7)], -1), LANES), seg(8)], axis=-1).astype(MXU_DTYPE)
    hd = C_NOPE + C_ROPE
    head_perm = np.concatenate([np.arange(C_NOPE), C_NOPE + p32])
    wq = c_wq_up.reshape(C_Q_RANK, C_HEADS, hd)[:, :, head_perm]
    wq = _pad_cols(wq, LANES).reshape(C_Q_RANK, C_HEADS * LANES).astype(MXU_DTYPE)
    wkv = c_wkv_up.reshape(C_KV_RANK, C_HEADS, C_NOPE + C_DV)
    wk = _pad_cols(wkv[:, :, :C_NOPE], LANES).reshape(C_KV_RANK, C_HEADS * LANES)
    wv = wkv[:, :, C_NOPE:].reshape(C_KV_RANK, C_HEADS * C_DV)
    wkv = jnp.concatenate([wk, wv], axis=-1).astype(MXU_DTYPE)
    qn = jnp.tile(_pad_cols(c_qn[head_perm] * (hd ** -0.5 * LOG2_E), LANES), C_HEADS)[None, :]
    kn = jnp.tile(_pad_cols(c_kn[head_perm], LANES), C_HEADS)[None, :]
    n = D_HEADS * D_DK
    wg = jnp.zeros((LANES, 2 * n), F32)
    wg = wg.at[:D_GATE_RANK, :n].set(d_wg_f).at[D_GATE_RANK:2 * D_GATE_RANK, n:].set(d_wg_b).astype(MXU_DTYPE)
    bg = jnp.concatenate([d_bg_f, d_bg_b])[None, :]
    return w, wq, wkv, qn, kn, wg, bg


def kernel(x, c, ctx, c_ctx, w_mod, b_mod, norm1, norm2, e_w_in, e_w_out, a_qn, a_kn, a_lq1, a_lk1, a_lq2, a_lk2, a_gn, b_qn, b_kn, b_sink, o_w_in, o_w_out, c_gq, c_gkv, c_wq_up, c_wkv_up, c_qn, c_kn, d_wg_f, d_bg_f, d_wg_b, d_bg_b, d_gn, p_wq, p_keys, p_u, p_v):
    B, S, D = x.shape
    L = ctx.shape[1]
    depth = w_mod.shape[0]
    assert L == ROW_TILE and S % ROW_TILE == 0 and S % GRID_W == 0
    LT = L // ROW_TILE
    bf = MXU_DTYPE

    cv = jnp.concatenate([c, c_ctx[None, :], jnp.zeros((8 - B - 1, D), F32)], axis=0)
    mods = _modulation(cv, w_mod, b_mod).reshape(depth, 8, 6, D)
    mod_tab = jnp.stack([jnp.broadcast_to(mods[:, B][:, None], (depth, B, 6, D)), mods[:, :B]], axis=2)
    mod = lambda i, k: mod_tab[i, :, :, k][:, :, None, :]

    cos64, sin64 = _rope_tables_64(S, L)
    cos_c, sin_c = _rope_tables_mla(S, L)

    xs = jnp.concatenate([ctx, x], axis=1)
    prev = None
    for i in range(depth):
        j = i // 2
        if i % 2 == 0:
            lam_init = 0.8 - 0.6 * math.exp(-0.3 * i)
            w, gain = _prep_even(e_w_in[j], a_qn[j], a_kn[j], b_qn[j], b_kn[j])
            xs, (aq, ak, bq, bk, av, bv) = _inproj(
                "even", xs, prev, norm1[i][None], mod(i, 0), mod(i, 1), w,
                [(gain, "full"), (cos64, "rows"), (sin64, "rows")],
                [(512, bf, True), (512, bf, False), (512, bf, False), (128, bf, False), (512, bf, True),
                 (128, bf, False)], LT)
            lqk = jnp.stack([a_lq1[j], a_lk1[j], a_lq2[j], a_lk2[j]])
            o1 = _diff_attn(aq, ak, av, lqk, a_gn[j][None], L, lam_init)
            o2 = _win_attn(bq, bk, bv, b_sink[j], L)
            xs, ht = _outproj(xs, o1, [o2], e_w_out[j].astype(bf), mod(i, 2), norm2[i][None], mod(i, 3), mod(i, 4),
                              LT, gla=False)
        else:
            w, wq, wkv, qn, kn, wg, bg = _prep_odd(o_w_in[j], c_wq_up[j], c_wkv_up[j], c_qn[j], c_kn[j],
                                                   d_wg_f[j], d_bg_f[j], d_wg_b[j], d_bg_b[j])
            xs, (cq, ck, cvv, dq, dk, dv, lg, r) = _inproj(
                "odd", xs, prev, norm1[i][None], mod(i, 0), mod(i, 1), w,
                [(c_gq[j][None], "full"), (c_gkv[j][None], "full"), (wq, "full"), (wkv, "full"), (qn, "full"),
                 (kn, "full"), (wg, "full"), (bg, "full"), (cos_c, "rows"), (sin_c, "rows")],
                [(1024, bf, True), (1024, bf, False), (512, bf, True), (256, F32, False), (256, F32, False),
                 (512, bf, False), (512, F32, False), (512, F32, False)], LT)
            o1 = _mla_attn(cq, ck, cvv, L)
            of = _gla(dq, dk, dv, lg, 0, L)
            ob = _gla(dq, dk, dv, lg, 1, L)
            xs, ht = _outproj(xs, o1, [of, ob, r, d_gn[j][None]], o_w_out[j].astype(bf), mod(i, 2), norm2[i][None],
                              mod(i, 3), mod(i, 4), LT, gla=True)
        wqt = p_wq[i].T.astype(bf)
        keys = p_keys[i].reshape(2 * P_HEADS, P_NKEYS, P_DKEY // 2).astype(bf)
        r2, e2, cnt, cc = _peer_gate(ht, wqt, keys)
        pp = _peer_dense(ht, r2, e2, cnt, cc, p_u[i].astype(bf), p_v[i].T.astype(bf))
        prev = (pp.reshape(B, L + S, D), mod(i, 5))
    return _final(xs, prev[0], prev[1], L)
```

```python
import functools
import math

import numpy as np
import jax
import jax.numpy as jnp
from jax import lax
from jax.experimental import pallas as pl
from jax.experimental.pallas import tpu as pltpu

F32 = jnp.float32
MXU_DTYPE = jnp.bfloat16
GATE_DTYPE = jnp.bfloat16
PACKED_ROWS = 16
PEER_SUB = 256
GATE_HEADS_PER_STEP = 4
PEER_EXPERT_TILE = 2048

GRID_W = 64
ROPE_THETA = 10000.0
EPS = 1e-6
NEG = -1e30
A_HEADS, A_DQK, A_DV = 4, 64, 128
B_HEADS, B_KV_HEADS, B_DH, WINDOW = 8, 2, 64, 128
C_HEADS, C_Q_RANK, C_KV_RANK, C_NOPE, C_ROPE, C_DV = 8, 256, 128, 64, 32, 64
D_HEADS, D_DK, D_DV, D_GATE_RANK, D_GATE_TAU, D_CHUNK = 4, 64, 128, 16, 16.0, 64
P_HEADS, P_NKEYS, P_DKEY, P_TOPK = 8, 128, 256, 16

LANES = 128
ROW_TILE = 256
KEY_CHUNK = 256
KEY_CHUNK_BIG = 768
QUERY_TILE = 512
LOG2_E = math.log2(math.e)
VMEM_LIMIT = 56 * 1024 * 1024


def _cparams(*sem):
    return pltpu.CompilerParams(dimension_semantics=sem, vmem_limit_bytes=VMEM_LIMIT)


def _rms(x, g):
    return x * lax.rsqrt(jnp.mean(x * x, axis=-1, keepdims=True) + EPS) * g


def _nt(a, b):
    return lax.dot_general(a, b, (((1,), (1,)), ((), ())), preferred_element_type=F32)


def _tn(a, b):
    return lax.dot_general(a, b, (((0,), (0,)), ((), ())), preferred_element_type=F32)


def _mm(a, b):
    return jnp.dot(a, b, preferred_element_type=F32)


def _mod_kernel(c_ref, w_ref, b_ref, o_ref):
    cv = c_ref[...]
    s = cv * jax.nn.sigmoid(cv)
    o_ref[0] = jnp.dot(s, w_ref[0], precision=lax.Precision.HIGHEST, preferred_element_type=F32) + b_ref[0]


def _modulation(cvecs, w_mod, b_mod):
    depth, d, n = w_mod.shape
    rows = cvecs.shape[0]
    tn = 1536
    return pl.pallas_call(
        _mod_kernel,
        grid=(depth, n // tn),
        in_specs=[pl.BlockSpec((rows, d), lambda l, j: (0, 0)),
                  pl.BlockSpec((1, d, tn), lambda l, j: (l, 0, j)),
                  pl.BlockSpec((1, 1, tn), lambda l, j: (l, 0, j))],
        out_specs=pl.BlockSpec((1, rows, tn), lambda l, j: (l, 0, j)),
        out_shape=jax.ShapeDtypeStruct((depth, rows, n), F32),
        compiler_params=_cparams("parallel", "parallel"),
        name="modulation",
    )(cvecs, w_mod, b_mod.reshape(depth, 1, n))


def _stream_prologue(has_prev, it):
    xs_ref = next(it)
    x = xs_ref[0]
    if has_prev:
        pp_ref, pg_ref = next(it), next(it)
        x = x + pg_ref[...] * pp_ref[0]
    g1_ref, sh_ref, sc_ref, w_ref = next(it), next(it), next(it), next(it)
    h = _rms(x, g1_ref[...]) * (1.0 + sc_ref[...]) + sh_ref[...]
    z = _mm(h.astype(MXU_DTYPE), w_ref[...])
    return x, z


def _norm_rope_pair64(zg, gain, cosf, sinf):
    lane = lax.broadcasted_iota(jnp.int32, zg.shape, 1)
    lo = lane < 64
    sq = zg * zg
    s_lo = jnp.sum(jnp.where(lo, sq, 0.0), axis=-1, keepdims=True)
    s_hi = jnp.sum(jnp.where(lo, 0.0, sq), axis=-1, keepdims=True)
    ms = jnp.where(lo, s_lo, s_hi) * (1.0 / 64.0)
    y = zg * lax.rsqrt(ms + EPS) * gain
    partner = jnp.where((lane & 63) < 32, pltpu.roll(y, 96, 1), pltpu.roll(y, 32, 1))
    return y * cosf + partner * sinf


def _inproj_even_kernel(has_prev, *refs):
    it = iter(refs)
    x, z = _stream_prologue(has_prev, it)
    gain_ref, cos_ref, sin_ref = next(it), next(it), next(it)
    if has_prev:
        xo_ref = next(it)
        xo_ref[0] = x
    aq_ref, ak_ref, bq_ref, bk_ref, av_ref, bv_ref = (next(it) for _ in range(6))
    cosf, sinf = cos_ref[...], sin_ref[...]
    g = 0
    for ref, ngroups, transposed in ((aq_ref, 4, True), (ak_ref, 4, False), (bq_ref, 4, False), (bk_ref, 1, False)):
        for k in range(ngroups):
            sl = slice(g * LANES, (g + 1) * LANES)
            y = _norm_rope_pair64(z[:, sl], gain_ref[:, sl], cosf, sinf)
            if transposed:
                ref[0, k * LANES:(k + 1) * LANES, :] = y.T.astype(ref.dtype)
            else:
                ref[0, :, k * LANES:(k + 1) * LANES] = y.astype(ref.dtype)
            g += 1
    o = g * LANES
    av_ref[0] = z[:, o:o + 512].T.astype(av_ref.dtype)
    bv_ref[0] = z[:, o + 512:o + 640].astype(bv_ref.dtype)


def _norm_rope_mla(y, gain, cosf, sinf):
    lane = lax.broadcasted_iota(jnp.int32, y.shape, 1)
    ms = jnp.sum(y * y, axis=-1, keepdims=True) * (1.0 / (C_NOPE + C_ROPE))
    y = y * lax.rsqrt(ms + EPS) * gain
    partner = jnp.where(lane < C_NOPE + C_ROPE // 2, pltpu.roll(y, LANES - C_ROPE // 2, 1),
                        pltpu.roll(y, C_ROPE // 2, 1))
    return y * cosf + partner * sinf


def _log_sigmoid(x):
    return jnp.minimum(x, 0.0) - jnp.log1p(jnp.exp(-jnp.abs(x)))


def _inproj_odd_kernel(has_prev, *refs):
    it = iter(refs)
    x, z = _stream_prologue(has_prev, it)
    (gq_ref, gkv_ref, wq_ref, wkv_ref, qn_ref, kn_ref, wg_ref, bg_ref, cos_ref, sin_ref) = (next(it) for _ in range(10))
    if has_prev:
        xo_ref = next(it)
        xo_ref[0] = x
    cq_ref, ck_ref, cv_ref, dq_ref, dk_ref, dv_ref, lg_ref, r_ref = (next(it) for _ in range(8))
    cosf, sinf = cos_ref[...], sin_ref[...]
    qn = _rms(z[:, 0:256], gq_ref[...])
    q_up = _mm(qn.astype(MXU_DTYPE), wq_ref[...])
    kvn = _rms(z[:, 256:384], gkv_ref[...])
    kv_up = _mm(kvn.astype(MXU_DTYPE), wkv_ref[...])
    kpe = pltpu.roll(z[:, 384:512], C_NOPE, 1)
    for h in range(C_HEADS):
        sl = slice(h * LANES, (h + 1) * LANES)
        cq_ref[0, sl, :] = _norm_rope_mla(q_up[:, sl], qn_ref[:, sl], cosf, sinf).T.astype(cq_ref.dtype)
        ck_ref[0, :, sl] = _norm_rope_mla(kv_up[:, sl] + kpe, kn_ref[:, sl], cosf, sinf).astype(ck_ref.dtype)
    cv_ref[0] = kv_up[:, C_HEADS * LANES:].T.astype(cv_ref.dtype)
    dq_ref[0] = z[:, 512:768] * (D_DK ** -0.5)
    dk_ref[0] = z[:, 768:1024]
    dv_ref[0] = z[:, 1024:1536].astype(dv_ref.dtype)
    gl = _mm(z[:, 1536:1664].astype(MXU_DTYPE), wg_ref[...]) + bg_ref[...]
    lg_ref[0] = _log_sigmoid(gl) * (1.0 / D_GATE_TAU)
    r_ref[0] = z[:, 1664:2176]


def _inproj(kind, xs, prev, norm_g, shift, scale, w, extra, out_cols, LT):
    B, T, D = xs.shape
    tm = ROW_TILE
    nT = T // tm
    has_prev = prev is not None
    row = lambda b, i: (b, i, 0)
    cls = lambda b, i: (b, jnp.where(i >= LT, 1, 0), 0, 0)
    full2 = lambda b, i: (0, 0)
    args, specs = [xs], [pl.BlockSpec((1, tm, D), row)]
    if has_prev:
        args += [prev[0], prev[1]]
        specs += [pl.BlockSpec((1, tm, D), row), pl.BlockSpec((None, None, 1, D), cls)]
    args += [norm_g, shift, scale, w]
    specs += [pl.BlockSpec((1, D), full2), pl.BlockSpec((None, None, 1, D), cls),
              pl.BlockSpec((None, None, 1, D), cls), pl.BlockSpec(w.shape, full2)]
    for arr, k in extra:
        args.append(arr)
        if k == "full":
            specs.append(pl.BlockSpec(arr.shape, full2))
        else:
            specs.append(pl.BlockSpec((tm, arr.shape[1]), lambda b, i: (i, 0)))
    out_shape, out_specs = [], []
    if has_prev:
        out_shape.append(jax.ShapeDtypeStruct((B, T, D), F32))
        out_specs.append(pl.BlockSpec((1, tm, D), row))
    for n, dt, transposed in out_cols:
        if transposed:
            out_shape.append(jax.ShapeDtypeStruct((B, n, T), dt))
            out_specs.append(pl.BlockSpec((1, n, tm), lambda b, i: (b, 0, i)))
        else:
            out_shape.append(jax.ShapeDtypeStruct((B, T, n), dt))
            out_specs.append(pl.BlockSpec((1, tm, n), row))
    body = _inproj_even_kernel if kind == "even" else _inproj_odd_kernel
    outs = pl.pallas_call(
        functools.partial(body, has_prev),
        grid=(B, nT), in_specs=specs, out_specs=out_specs, out_shape=out_shape,
        compiler_params=_cparams("parallel", "parallel"),
        name="inproj_" + kind,
    )(*args)
    if has_prev:
        return outs[0], outs[1:]
    return xs, outs


def _flash_streams(ctx, q_streams, k_cols, v_rows, k_ref, vt_ref, ml_ref, acc_ref, sc_ref=None, p_ref=None):
    T = k_ref.shape[1]
    tq = q_streams[0].shape[1]
    acc_ref[...] = jnp.zeros(acc_ref.shape, F32)
    init = (jnp.full((1, tq), -jnp.inf, F32), jnp.zeros((1, tq), F32)) * 2

    def chunk(start, size, ml):
        out = []
        scs = [_mm(k_ref[0, start:start + size, k_cols[s]], q_streams[s]) for s in range(2)]
        for s in range(2):
            m_prev, l_prev = ml[2 * s], ml[2 * s + 1]
            m_new = jnp.maximum(m_prev, jnp.max(scs[s], axis=0, keepdims=True))
            alpha = jnp.exp2(m_prev - m_new)
            p = jnp.exp2(scs[s] - m_new)
            acc_ref[s] = alpha * acc_ref[s] + _mm(vt_ref[0, v_rows[s], start:start + size], p.astype(MXU_DTYPE))
            out += [m_new, alpha * l_prev + jnp.sum(p, axis=0, keepdims=True)]
        return tuple(out)

    if ctx:
        ml = init
        for c in range(0, T, KEY_CHUNK):
            ml = chunk(c, KEY_CHUNK, ml)
        ml_ref[...] = jnp.concatenate(ml, axis=0)
        return

    big = sc_ref.shape[1]
    piece = KEY_CHUNK
    pieces = big // piece
    n_chunks = T // big

    def piece_scores(c, j, s):
        start = pl.multiple_of(c * big + j * piece, piece)
        return _mm(k_ref[0, pl.ds(start, piece), k_cols[s]], q_streams[s])

    def col_max(parts):
        return functools.reduce(jnp.maximum, [jnp.max(x, axis=0, keepdims=True) for x in parts])

    def value_product(c, s):
        start = pl.multiple_of(c * big, big)
        return _mm(vt_ref[0, v_rows[s], pl.ds(start, big)], p_ref[s])

    def pipelined(t, carry, look_ahead=True):
        ml, mx, alpha_prev = carry
        pv = [value_product(jnp.maximum(t - 1, 0), s) for s in range(2)]
        m_new = [jnp.maximum(ml[2 * s], mx[s]) for s in range(2)]
        alpha = [jnp.exp2(ml[2 * s] - m_new[s]) for s in range(2)]
        ahead, sums = ([], []), ([], [])
        for j in range(pieces):
            rows = slice(j * piece, (j + 1) * piece)
            if look_ahead:
                for s in range(2):
                    ahead[s].append(piece_scores(t + 1, j, s))
            for s in range(2):
                p = jnp.exp2(sc_ref[s, rows, :] - m_new[s])
                sums[s].append(jnp.sum(p, axis=0, keepdims=True))
                p_ref[s, rows, :] = p.astype(p_ref.dtype)
            if look_ahead:
                for s in range(2):
                    sc_ref[s, rows, :] = ahead[s][j]
        new_ml = []
        for s in range(2):
            acc_ref[s] = alpha_prev[s] * acc_ref[s] + pv[s]
            new_ml += [m_new[s], alpha[s] * ml[2 * s + 1] + functools.reduce(jnp.add, sums[s])]
        mx_next = tuple(col_max(ahead[s]) for s in range(2)) if look_ahead else mx
        return tuple(new_ml), mx_next, tuple(alpha)

    first = []
    for s in range(2):
        parts = [piece_scores(0, j, s) for j in range(pieces)]
        for j in range(pieces):
            sc_ref[s, j * piece:(j + 1) * piece, :] = parts[j]
        first.append(col_max(parts))
    p_ref[...] = jnp.zeros(p_ref.shape, p_ref.dtype)
    one = jnp.ones((1, tq), F32)
    carry = lax.fori_loop(0, n_chunks - 1, pipelined, (init, tuple(first), (one, one)))
    ml, _, alpha_last = pipelined(n_chunks - 1, carry, look_ahead=False)
    for s in range(2):
        acc_ref[s] = alpha_last[s] * acc_ref[s] + value_product(n_chunks - 1, s)
    ml_ref[...] = jnp.concatenate(ml, axis=0)


def _query_tile(n_q):
    return QUERY_TILE if n_q % QUERY_TILE == 0 else ROW_TILE


def _big_chunk(T):
    return KEY_CHUNK_BIG if T % KEY_CHUNK_BIG == 0 else KEY_CHUNK


def _flash_scratch(ctx, T, dv, tq):
    scratch = [pltpu.VMEM((4, tq), F32), pltpu.VMEM((2, dv, tq), F32)]
    if not ctx:
        scratch += [pltpu.VMEM((2, _big_chunk(T), tq), F32), pltpu.VMEM((2, _big_chunk(T), tq), MXU_DTYPE)]
    return scratch


def _split_queries(qt, L):
    return qt[:, :, :L], qt[:, :, L:]


def _diff_attn_kernel(ctx, lam_init, q_ref, k_ref, vt_ref, lqk_ref, gn_ref, o_ref, ml_ref, acc_ref, *pipe):
    q = q_ref[0]
    row = lax.broadcasted_iota(jnp.int32, q.shape, 0)
    zero = jnp.zeros_like(q)
    streams = (jnp.where(row < A_DQK, q, zero), jnp.where(row < A_DQK, zero, q))
    every = slice(None)
    _flash_streams(ctx, streams, (every, every), (every, every), k_ref, vt_ref, ml_ref, acc_ref, *pipe)
    lqk = lqk_ref[...]
    lam = (jnp.exp(jnp.sum(lqk[0:1] * lqk[1:2], axis=-1, keepdims=True))
           - jnp.exp(jnp.sum(lqk[2:3] * lqk[3:4], axis=-1, keepdims=True)) + lam_init)
    o = acc_ref[0] / ml_ref[1:2, :] - lam * (acc_ref[1] / ml_ref[3:4, :])
    o = o * lax.rsqrt(jnp.mean(o * o, axis=0, keepdims=True) + EPS) * gn_ref[...]
    o_ref[0] = (o * (1.0 - lam_init)).astype(o_ref.dtype)


def _diff_attn(qt, k, vt, lqk, gn, L, lam_init):
    B, T, _ = k.shape
    assert L % KEY_CHUNK == 0 and T % KEY_CHUNK == 0

    def call(ctx, q, keys, tq):
        n_q = q.shape[2]
        gn_b = jnp.broadcast_to(gn.reshape(A_DV, 1), (A_DV, tq))
        return pl.pallas_call(
            functools.partial(_diff_attn_kernel, ctx, lam_init),
            grid=(B, A_HEADS, n_q // tq),
            in_specs=[pl.BlockSpec((1, LANES, tq), lambda b, h, i: (b, h, i)),
                      pl.BlockSpec((1, keys, LANES), lambda b, h, i: (b, 0, h)),
                      pl.BlockSpec((1, A_DV, keys), lambda b, h, i: (b, h, 0)),
                      pl.BlockSpec(lqk.shape, lambda b, h, i: (0, 0)),
                      pl.BlockSpec(gn_b.shape, lambda b, h, i: (0, 0))],
            out_specs=pl.BlockSpec((1, A_DV, tq), lambda b, h, i: (b, h, i)),
            out_shape=jax.ShapeDtypeStruct((B, A_HEADS * A_DV, n_q), MXU_DTYPE),
            scratch_shapes=_flash_scratch(ctx, keys, A_DV, tq),
            compiler_params=_cparams("parallel", "parallel", "parallel"),
            name="diff_attn_ctx" if ctx else "diff_attn",
        )(q, k, vt, lqk, gn_b)

    q_ctx, q_lat = _split_queries(qt, L)
    return jnp.concatenate([call(True, q_ctx, L, ROW_TILE), call(False, q_lat, T, _query_tile(T - L))], axis=2)


def _mla_attn_kernel(ctx, q_ref, k_ref, vt_ref, o_ref, ml_ref, acc_ref, *pipe):
    streams = (q_ref[0, :LANES, :], q_ref[0, LANES:, :])
    k_cols = (slice(0, LANES), slice(LANES, 2 * LANES))
    v_rows = (slice(0, C_DV), slice(C_DV, 2 * C_DV))
    _flash_streams(ctx, streams, k_cols, v_rows, k_ref, vt_ref, ml_ref, acc_ref, *pipe)
    o_ref[0] = jnp.concatenate([acc_ref[0] / ml_ref[1:2, :], acc_ref[1] / ml_ref[3:4, :]], axis=0).astype(o_ref.dtype)


def _mla_attn(qt, k, vt, L):
    B, T, _ = k.shape
    assert L % KEY_CHUNK == 0 and T % KEY_CHUNK == 0

    def call(ctx, q, keys, tq):
        n_q = q.shape[2]
        return pl.pallas_call(
            functools.partial(_mla_attn_kernel, ctx),
            grid=(B, C_HEADS // 2, n_q // tq),
            in_specs=[pl.BlockSpec((1, 2 * LANES, tq), lambda b, h, i: (b, h, i)),
                      pl.BlockSpec((1, keys, 2 * LANES), lambda b, h, i: (b, 0, h)),
                      pl.BlockSpec((1, 2 * C_DV, keys), lambda b, h, i: (b, h, 0))],
            out_specs=pl.BlockSpec((1, 2 * C_DV, tq), lambda b, h, i: (b, h, i)),
            out_shape=jax.ShapeDtypeStruct((B, C_HEADS * C_DV, n_q), MXU_DTYPE),
            scratch_shapes=_flash_scratch(ctx, keys, C_DV, tq),
            compiler_params=_cparams("parallel", "parallel", "parallel"),
            name="mla_attn_ctx" if ctx else "mla_attn",
        )(q, k, vt)

    q_ctx, q_lat = _split_queries(qt, L)
    return jnp.concatenate([call(True, q_ctx, L, ROW_TILE), call(False, q_lat, T, _query_tile(T - L))], axis=2)


def _win_attn_kernel(LT, sink_ref, q_ref, kc_ref, kp_ref, kk_ref, kn_ref, vc_ref, vp_ref, vk_ref, vn_ref, o_ref):
    i, nT = pl.program_id(1), pl.num_programs(1)
    tq = q_ref.shape[1]
    tk = kc_ref.shape[1]
    r = lax.broadcasted_iota(jnp.int32, (2 * tq, tk), 0)
    c = lax.broadcasted_iota(jnp.int32, (2 * tq, tk), 1)
    d = c - jnp.where(r >= tq, r - tq, r)
    is_x = i >= LT
    far = 4 * tk
    ok_p = d >= jnp.where(jnp.logical_and(is_x, i - 1 >= LT), tk - WINDOW, far)
    ok_k = jnp.abs(d) <= jnp.where(is_x, WINDOW, -1)
    ok_n = d <= jnp.where(jnp.logical_and(is_x, i + 1 <= nT - 1), WINDOW - tk, -far)
    k_tiles = (kc_ref[0], kp_ref[0], kk_ref[0], kn_ref[0])
    masks = (None, ok_p, ok_k, ok_n)
    vals = jnp.concatenate([vc_ref[0], vp_ref[0], vk_ref[0], vn_ref[0]], axis=0)
    lane = lax.broadcasted_iota(jnp.int32, (tq, LANES), 1)
    lo = lane < B_DH
    rows = lax.broadcasted_iota(jnp.int32, (2 * tq, 1), 0)
    group = B_HEADS // B_KV_HEADS
    for g in range(B_HEADS // 2):
        kv = (2 * g) // group
        qg = q_ref[0, :, g * LANES:(g + 1) * LANES].astype(F32)
        qr = pltpu.roll(qg, B_DH, 1)
        if kv == 0:
            qa, qb = jnp.where(lo, qg, 0.0), jnp.where(lo, qr, 0.0)
        else:
            qa, qb = jnp.where(lo, 0.0, qr), jnp.where(lo, 0.0, qg)
        qs = jnp.concatenate([qa, qb], axis=0).astype(MXU_DTYPE)
        s = jnp.concatenate([_nt(qs, kt) if ok is None else jnp.where(ok, _nt(qs, kt), NEG)
                             for kt, ok in zip(k_tiles, masks)], axis=1)
        sk = jnp.where(rows < tq, sink_ref[2 * g], sink_ref[2 * g + 1])
        m = jnp.maximum(jnp.max(s, axis=-1, keepdims=True), sk)
        p = jnp.exp(s - m)
        l = jnp.sum(p, axis=-1, keepdims=True) + jnp.exp(sk - m)
        acc = _mm(p.astype(MXU_DTYPE), vals) / l
        if kv == 0:
            out = jnp.where(lo, acc[:tq], pltpu.roll(acc[tq:], B_DH, 1))
        else:
            out = jnp.where(lo, pltpu.roll(acc[:tq], B_DH, 1), acc[tq:])
        o_ref[0, :, g * LANES:(g + 1) * LANES] = out.astype(o_ref.dtype)


def _win_attn(q, k, v, sink, L):
    B, T, _ = q.shape
    t = ROW_TILE
    assert L == t and WINDOW <= t
    nT, LT = T // t, L // t
    kspec = lambda f: pl.BlockSpec((1, t, LANES), f)
    maps = [lambda b, i: (b, 0, 0),
            lambda b, i: (b, jnp.maximum(i - 1, LT), 0),
            lambda b, i: (b, i, 0),
            lambda b, i: (b, jnp.minimum(i + 1, nT - 1), 0)]
    return pl.pallas_call(
        functools.partial(_win_attn_kernel, LT),
        grid=(B, nT),
        in_specs=[pl.BlockSpec(memory_space=pltpu.SMEM),
                  pl.BlockSpec((1, t, B_HEADS * B_DH), lambda b, i: (b, i, 0))]
                 + [kspec(f) for f in maps] + [kspec(f) for f in maps],
        out_specs=pl.BlockSpec((1, t, B_HEADS * B_DH), lambda b, i: (b, i, 0)),
        out_shape=jax.ShapeDtypeStruct((B, T, B_HEADS * B_DH), MXU_DTYPE),
        compiler_params=_cparams("parallel", "parallel"),
        name="window_attn",
    )(sink, q, k, k, k, k, v, v, v, v)


def _gla_kernel(reverse, q_ref, k_ref, v_ref, g_ref, o_ref, st_ref):
    @pl.when(pl.program_id(1) == 0)
    def _():
        st_ref[...] = jnp.zeros(st_ref.shape, F32)

    C = D_CHUNK
    tb = q_ref.shape[1]
    r = lax.broadcasted_iota(jnp.int32, (C, C), 0)
    c = lax.broadcasted_iota(jnp.int32, (C, C), 1)
    keep = (c >= r) if reverse else (c <= r)
    tri = jnp.where(keep, 1.0, 0.0).astype(F32)
    chunks = range(tb // C)
    for ch in (reversed(chunks) if reverse else chunks):
        rows = slice(ch * C, (ch + 1) * C)
        b = jnp.dot(tri, g_ref[0, rows, :], precision=lax.Precision.HIGHEST, preferred_element_type=F32)
        b_last = b[0:1] if reverse else b[C - 1:C]
        q_t = q_ref[0, rows, :] * jnp.exp(b)
        kk = k_ref[0, rows, :]
        k_in = kk * jnp.exp(-b)
        k_st = kk * jnp.exp(b_last - b)
        decay = jnp.exp(b_last)
        for h in range(D_HEADS):
            sl = slice(h * D_DK, (h + 1) * D_DK)
            vsl = slice(h * D_DV, (h + 1) * D_DV)
            qh = q_t[:, sl].astype(MXU_DTYPE)
            a = jnp.where(keep, _nt(qh, k_in[:, sl].astype(MXU_DTYPE)), 0.0)
            vh = v_ref[0, rows, vsl]
            st = st_ref[h]
            o_ref[0, rows, vsl] = _mm(a.astype(MXU_DTYPE), vh) + _nt(qh, st.astype(MXU_DTYPE))
            st_ref[h] = st * decay[:, sl] + _tn(vh, k_st[:, sl].astype(MXU_DTYPE))


def _gla(q, k, v, lg, direction, L):
    B, T, _ = q.shape
    tb = ROW_TILE
    nB, LB = T // tb, L // tb
    reverse = direction == 1
    if reverse:
        blk = lambda s: jnp.where(s < LB, LB - 1 - s, LB + nB - 1 - s)
    else:
        blk = lambda s: s
    return pl.pallas_call(
        functools.partial(_gla_kernel, reverse),
        grid=(B, nB),
        in_specs=[pl.BlockSpec((1, tb, D_HEADS * D_DK), lambda b, s: (b, blk(s), 0)),
                  pl.BlockSpec((1, tb, D_HEADS * D_DK), lambda b, s: (b, blk(s), 0)),
                  pl.BlockSpec((1, tb, D_HEADS * D_DV), lambda b, s: (b, blk(s), 0)),
                  pl.BlockSpec((1, tb, D_HEADS * D_DK), lambda b, s: (b, blk(s), direction))],
        out_specs=pl.BlockSpec((1, tb, D_HEADS * D_DV), lambda b, s: (b, blk(s), 0)),
        out_shape=jax.ShapeDtypeStruct((B, T, D_HEADS * D_DV), F32),
        scratch_shapes=[pltpu.VMEM((D_HEADS, D_DV, D_DK), F32)],
        compiler_params=_cparams("parallel", "arbitrary"),
        name="gla_bwd" if reverse else "gla_fwd",
    )(q, k, v, lg)


def _outproj_kernel(gla, *refs):
    it = iter(refs)
    xs_ref, o1_ref = next(it), next(it)
    if gla:
        of_ref, ob_ref, r_ref, dgn_ref = (next(it) for _ in range(4))
        og = of_ref[0] + ob_ref[0]
        rr = r_ref[0]
        parts = []
        for h in range(D_HEADS):
            sl = slice(h * D_DV, (h + 1) * D_DV)
            rh = rr[:, sl]
            parts.append((_rms(og[:, sl], dgn_ref[...]) * (rh * jax.nn.sigmoid(rh))).astype(MXU_DTYPE))
        o2 = jnp.concatenate(parts, axis=1)
    else:
        o2 = next(it)[0]
    w_ref, gate_ref, n2_ref, sh_ref, sc_ref, xo_ref, ht_ref = (next(it) for _ in range(7))
    half = o1_ref.shape[1]
    x = xs_ref[0] + gate_ref[...] * (_tn(o1_ref[0], w_ref[:half, :]) + _mm(o2, w_ref[half:, :]))
    xo_ref[0] = x
    h2 = _rms(x, n2_ref[...]) * (1.0 + sc_ref[...]) + sh_ref[...]
    ht_ref[...] = h2.T.astype(ht_ref.dtype)


def _outproj(xs, o1, o2s, w, gate, norm_g, shift, scale, LT, gla):
    B, T, D = xs.shape
    tm = ROW_TILE
    nT = T // tm
    row = lambda b, i: (b, i, 0)
    cls = lambda b, i: (b, jnp.where(i >= LT, 1, 0), 0, 0)
    full2 = lambda b, i: (0, 0)
    args = [xs, o1]
    specs = [pl.BlockSpec((1, tm, D), row), pl.BlockSpec((1, o1.shape[1], tm), lambda b, i: (b, 0, i))]
    for a in o2s:
        args.append(a)
        specs.append(pl.BlockSpec(a.shape, full2) if a.ndim == 2 else pl.BlockSpec((1, tm, a.shape[2]), row))
    args += [w, gate, norm_g, shift, scale]
    specs += [pl.BlockSpec(w.shape, full2), pl.BlockSpec((None, None, 1, D), cls), pl.BlockSpec((1, D), full2),
              pl.BlockSpec((None, None, 1, D), cls), pl.BlockSpec((None, None, 1, D), cls)]
    return pl.pallas_call(
        functools.partial(_outproj_kernel, gla),
        grid=(B, nT), in_specs=specs,
        out_specs=[pl.BlockSpec((1, tm, D), row), pl.BlockSpec((D, tm), lambda b, i: (0, b * nT + i))],
        out_shape=[jax.ShapeDtypeStruct((B, T, D), F32), jax.ShapeDtypeStruct((D, B * T), MXU_DTYPE)],
        compiler_params=_cparams("parallel", "parallel"),
        name="outproj_gla" if gla else "outproj",
    )(*args)


def _top_values(s, k, with_rank=False):
    vals = []
    rank = jnp.full(s.shape, float(k), F32) if with_rank else None
    for a in range(k):
        m = jnp.max(s, axis=0, keepdims=True)
        vals.append(m)
        hit = s == m
        if with_rank:
            rank = jnp.where(hit, float(a), rank)
        if a + 1 < k:
            s = jnp.where(hit, -jnp.inf, s)
    vals = jnp.concatenate(vals, axis=0)
    return (vals, rank) if with_rank else vals


def _merge_exchanges(lo, n, r):
    step = r * 2
    if step < n:
        pairs = _merge_exchanges(lo, n, step) + _merge_exchanges(lo + r, n, step)
        return pairs + [(i, i + r) for i in range(lo + r, lo + n - r, step)]
    return [(lo, lo + r)]


def _sort_exchanges(lo, n):
    if n <= 1:
        return []
    half = n // 2
    return _sort_exchanges(lo, half) + _sort_exchanges(lo + half, half) + _merge_exchanges(lo, n, 1)


def _top_values_sorted(s, k):
    slabs = [s[8 * r:8 * (r + 1)] for r in range(s.shape[0] // 8)]
    assert len(slabs) == k
    for i, j in _sort_exchanges(0, k):
        slabs[i], slabs[j] = jnp.maximum(slabs[i], slabs[j]), jnp.minimum(slabs[i], slabs[j])
    vals = []
    for a in range(k):
        m = jnp.max(slabs[0], axis=0, keepdims=True)
        vals.append(m)
        hit = slabs[0] == m
        for r in range(k - a - 1):
            slabs[r] = jnp.where(hit, slabs[r + 1], slabs[r])
    return jnp.concatenate(vals, axis=0)


def _peer_gate_kernel(ht_ref, wqt_ref, keys_ref, *out_refs):
    n = P_NKEYS
    scores = []
    for g in range(wqt_ref.shape[0] // P_DKEY):
        qt = _mm(wqt_ref[g * P_DKEY:(g + 1) * P_DKEY, :], ht_ref[...])
        scores.append((_mm(keys_ref[2 * g], qt[:n].astype(MXU_DTYPE)),
                       _mm(keys_ref[2 * g + 1], qt[n:].astype(MXU_DTYPE))))
    for g, (s1, s2) in enumerate(scores):
        _peer_gate_head(s1, s2, g, *out_refs)


def _peer_gate_head(s1, s2, g, r2_ref, e2_ref, cnt_ref, cc_ref):
    k = P_TOPK
    sv1 = _top_values_sorted(s1, k)
    sv2, rank2 = _top_values(s2, k, with_rank=True)
    cand = jnp.concatenate([sv1[a:a + 1] + sv2[:k // (a + 1)] for a in range(k)], axis=0)
    tau = _top_values(cand, k)[k - 1:k]
    top = sv1[0:1] + sv2[0:1]
    z = jnp.sum(jnp.where(cand >= tau, jnp.exp(cand - top), 0.0), axis=0, keepdims=True)
    cnt = jnp.zeros(s1.shape, F32)
    for b in range(k // 2):
        cnt = cnt + jnp.where(s1 + sv2[b:b + 1] >= tau, 1.0, 0.0)
    extra = functools.reduce(jnp.add, [jnp.where(sv1[0:1] + sv2[b:b + 1] >= tau, 1.0, 0.0) for b in range(k // 2, k)])
    cnt = cnt + jnp.where(s1 == sv1[0:1], extra, 0.0)
    r2_ref[g] = rank2.astype(r2_ref.dtype)
    e2_ref[g] = jnp.exp(s2 - sv2[0:1]).astype(e2_ref.dtype)
    cnt_ref[g] = cnt
    cc_ref[g] = jnp.exp(s1 - sv1[0:1]) * (0.5 / z)


def _peer_token_tile(n_tok):
    return 512 if n_tok % 512 == 0 else 256


def _peer_gate(ht, wqt, keys):
    D, N = ht.shape
    tm = _peer_token_tile(N)
    shape = lambda dt: jax.ShapeDtypeStruct((P_HEADS, P_NKEYS, N), dt)
    hg = GATE_HEADS_PER_STEP
    ospec = pl.BlockSpec((hg, P_NKEYS, tm), lambda i, h: (h, 0, i))
    return pl.pallas_call(
        _peer_gate_kernel,
        grid=(N // tm, P_HEADS // hg),
        in_specs=[pl.BlockSpec((D, tm), lambda i, h: (0, i)),
                  pl.BlockSpec((hg * P_DKEY, D), lambda i, h: (h, 0)),
                  pl.BlockSpec((2 * hg, P_NKEYS, P_DKEY // 2), lambda i, h: (h, 0, 0))],
        out_specs=[ospec] * 4,
        out_shape=[shape(GATE_DTYPE), shape(GATE_DTYPE), shape(F32), shape(F32)],
        compiler_params=_cparams("parallel", "arbitrary"),
        name="peer_gate",
    )(ht, wqt, keys)


def _rows_packed(row, n):
    tile = jnp.broadcast_to(row, (PACKED_ROWS, row.shape[1])).astype(GATE_DTYPE)
    return jnp.concatenate([tile] * (n // PACKED_ROWS), axis=0)


def _peer_dense_kernel(ht_ref, r2_ref, e2_ref, cnt_ref, cc_ref, u_ref, vt_ref, o_ref, acc_ref):
    j = pl.program_id(1)

    @pl.when(j == 0)
    def _():
        acc_ref[...] = jnp.zeros(acc_ref.shape, F32)

    n = P_NKEYS
    zero = jnp.zeros((n, ht_ref.shape[1]), GATE_DTYPE)
    ht = ht_ref[...]
    total = None
    nsub = u_ref.shape[0] // PEER_SUB
    scores = lambda k: _mm(u_ref[k * PEER_SUB:(k + 1) * PEER_SUB, :], ht)
    zt_next = scores(0)
    for k in range(nsub):
        zt = zt_next
        if k + 1 < nsub:
            zt_next = scores(k + 1)
        rows = []
        for ii in range(PEER_SUB // n):
            i1 = k * (PEER_SUB // n) + ii
            g = None
            for h in range(P_HEADS):
                sel = r2_ref[h] < _rows_packed(cnt_ref[h, i1:i1 + 1, :], n)
                t = jnp.where(sel, e2_ref[h], zero) * _rows_packed(cc_ref[h, i1:i1 + 1, :], n)
                g = t if g is None else g + t
            z = zt[ii * n:(ii + 1) * n]
            gelu = z * (1.0 + lax.erf(z * math.sqrt(0.5)))
            rows.append((gelu.astype(GATE_DTYPE) * g).astype(MXU_DTYPE))
        acc_ref[...] += _mm(vt_ref[:, k * PEER_SUB:(k + 1) * PEER_SUB], jnp.concatenate(rows, axis=0))

    @pl.when(j == pl.num_programs(1) - 1)
    def _():
        o_ref[...] = acc_ref[...].T


def _peer_dense(ht, r2, e2, cnt, cc, u, vt):
    D, N = ht.shape
    E = u.shape[0]
    tm = _peer_token_tile(N)
    te = PEER_EXPERT_TILE
    gspec = pl.BlockSpec((P_HEADS, P_NKEYS, tm), lambda i, j: (0, 0, i))
    rspec = pl.BlockSpec((P_HEADS, te // P_NKEYS, tm), lambda i, j: (0, j, i))
    return pl.pallas_call(
        _peer_dense_kernel,
        grid=(N // tm, E // te),
        in_specs=[pl.BlockSpec((D, tm), lambda i, j: (0, i)), gspec, gspec, rspec, rspec,
                  pl.BlockSpec((te, D), lambda i, j: (j, 0)),
                  pl.BlockSpec((D, te), lambda i, j: (0, j))],
        out_specs=pl.BlockSpec((tm, D), lambda i, j: (i, 0)),
        out_shape=jax.ShapeDtypeStruct((N, D), F32),
        scratch_shapes=[pltpu.VMEM((D, tm), F32)],
        compiler_params=_cparams("parallel", "arbitrary"),
        name="peer_dense",
    )(ht, r2, e2, cnt, cc, u, vt)


def _final_kernel(xs_ref, pp_ref, g_ref, o_ref):
    o_ref[0] = xs_ref[0] + g_ref[...] * pp_ref[0]


def _final(xs, pp, gate, L):
    B, T, D = xs.shape
    tm = ROW_TILE
    LT = L // tm
    src = lambda b, i: (b, i + LT, 0)
    return pl.pallas_call(
        _final_kernel,
        grid=(B, (T - L) // tm),
        in_specs=[pl.BlockSpec((1, tm, D), src), pl.BlockSpec((1, tm, D), src),
                  pl.BlockSpec((None, None, 1, D), lambda b, i: (b, 1, 0, 0))],
        out_specs=pl.BlockSpec((1, tm, D), lambda b, i: (b, i, 0)),
        out_shape=jax.ShapeDtypeStruct((B, T - L, D), F32),
        compiler_params=_cparams("parallel", "parallel"),
        name="final_residual",
    )(xs, pp, gate)


def _even_odd(n):
    return np.concatenate([np.arange(0, n, 2), np.arange(1, n, 2)])


def _rope_angles(S, L, rot_dim):
    pos = jnp.arange(S)
    r = (pos // GRID_W).astype(F32)
    cc = (pos % GRID_W).astype(F32)
    axis_dim = rot_dim // 2
    inv = 1.0 / (ROPE_THETA ** (jnp.arange(0, axis_dim, 2, dtype=F32) / axis_dim))
    ang = jnp.concatenate([r[:, None] * inv, cc[:, None] * inv], axis=-1)
    cos = jnp.concatenate([jnp.ones((L, rot_dim // 2), F32), jnp.cos(ang)], axis=0)
    sin = jnp.concatenate([jnp.zeros((L, rot_dim // 2), F32), jnp.sin(ang)], axis=0)
    return cos, sin


def _rope_tables_64(S, L):
    cos, sin = _rope_angles(S, L, 64)
    return jnp.tile(jnp.concatenate([cos, cos], -1), (1, 2)), jnp.tile(jnp.concatenate([-sin, sin], -1), (1, 2))


def _rope_tables_mla(S, L):
    cos, sin = _rope_angles(S, L, C_ROPE)
    T = cos.shape[0]
    one, zero = jnp.ones((T, C_NOPE), F32), jnp.zeros((T, C_NOPE), F32)
    pad1, pad0 = jnp.ones((T, LANES - C_NOPE - C_ROPE), F32), jnp.zeros((T, LANES - C_NOPE - C_ROPE), F32)
    return (jnp.concatenate([one, cos, cos, pad1], -1), jnp.concatenate([zero, -sin, sin, pad0], -1))


def _prep_even(w_in, a_qn, a_kn, b_qn, b_kn):
    sizes = (512, 512, 512, 512, 128, 128)
    off = np.concatenate([[0], np.cumsum(sizes)])
    p64 = _even_odd(64)

    def seg(k, permute):
        idx = np.arange(off[k], off[k + 1])
        if permute:
            idx = idx.reshape(-1, 64)[:, p64].reshape(-1)
        return idx

    cols = np.concatenate([seg(0, True), seg(1, True), seg(3, True), seg(4, True), seg(2, False), seg(5, False)])
    w = w_in[:, cols].astype(MXU_DTYPE)
    q_scale = A_DQK ** -0.5 * LOG2_E
    gain = jnp.concatenate([jnp.tile(a_qn[p64], 8) * q_scale, jnp.tile(a_kn[p64], 8),
                            jnp.tile(b_qn[p64], 8) * (B_DH ** -0.5), jnp.tile(b_kn[p64], 2)])[None, :]
    return w, gain


def _pad_cols(a, n):
    return jnp.concatenate([a, jnp.zeros(a.shape[:-1] + (n - a.shape[-1],), a.dtype)], axis=-1)


def _prep_odd(w_in, c_wq_up, c_wkv_up, c_qn, c_kn, d_wg_f, d_bg_f, d_wg_b, d_bg_b):
    D = w_in.shape[0]
    p32 = _even_odd(C_ROPE)
    o = np.concatenate([[0], np.cumsum((C_Q_RANK, C_KV_RANK, C_ROPE, 256, 256, 512, D_GATE_RANK, D_GATE_RANK, 512))])
    seg = lambda k: w_in[:, o[k]:o[k + 1]]
    w = jnp.concatenate([seg(0), seg(1), _pad_cols(seg(2)[:, p32], LANES), seg(3), seg(4), seg(5),
                         _pad_cols(jnp.concatenate([seg(6), seg(7)], -1), LANES), seg(8)], axis=-1).astype(MXU_DTYPE)
    hd = C_NOPE + C_ROPE
    head_perm = np.concatenate([np.arange(C_NOPE), C_NOPE + p32])
    wq = c_wq_up.reshape(C_Q_RANK, C_HEADS, hd)[:, :, head_perm]
    wq = _pad_cols(wq, LANES).reshape(C_Q_RANK, C_HEADS * LANES).astype(MXU_DTYPE)
    wkv = c_wkv_up.reshape(C_KV_RANK, C_HEADS, C_NOPE + C_DV)
    wk = _pad_cols(wkv[:, :, :C_NOPE], LANES).reshape(C_KV_RANK, C_HEADS * LANES)
    wv = wkv[:, :, C_NOPE:].reshape(C_KV_RANK, C_HEADS * C_DV)
    wkv = jnp.concatenate([wk, wv], axis=-1).astype(MXU_DTYPE)
    qn = jnp.tile(_pad_cols(c_qn[head_perm] * (hd ** -0.5 * LOG2_E), LANES), C_HEADS)[None, :]
    kn = jnp.tile(_pad_cols(c_kn[head_perm], LANES), C_HEADS)[None, :]
    n = D_HEADS * D_DK
    wg = jnp.zeros((LANES, 2 * n), F32)
    wg = wg.at[:D_GATE_RANK, :n].set(d_wg_f).at[D_GATE_RANK:2 * D_GATE_RANK, n:].set(d_wg_b).astype(MXU_DTYPE)
    bg = jnp.concatenate([d_bg_f, d_bg_b])[None, :]
    return w, wq, wkv, qn, kn, wg, bg


def kernel(x, c, ctx, c_ctx, w_mod, b_mod, norm1, norm2, e_w_in, e_w_out, a_qn, a_kn, a_lq1, a_lk1, a_lq2, a_lk2, a_gn, b_qn, b_kn, b_sink, o_w_in, o_w_out, c_gq, c_gkv, c_wq_up, c_wkv_up, c_qn, c_kn, d_wg_f, d_bg_f, d_wg_b, d_bg_b, d_gn, p_wq, p_keys, p_u, p_v):
    B, S, D = x.shape
    L = ctx.shape[1]
    depth = w_mod.shape[0]
    assert L == ROW_TILE and S % ROW_TILE == 0 and S % GRID_W == 0
    LT = L // ROW_TILE
    bf = MXU_DTYPE

    cv = jnp.concatenate([c, c_ctx[None, :], jnp.zeros((8 - B - 1, D), F32)], axis=0)
    mods = _modulation(cv, w_mod, b_mod).reshape(depth, 8, 6, D)
    mod_tab = jnp.stack([jnp.broadcast_to(mods[:, B][:, None], (depth, B, 6, D)), mods[:, :B]], axis=2)
    mod = lambda i, k: mod_tab[i, :, :, k][:, :, None, :]

    cos64, sin64 = _rope_tables_64(S, L)
    cos_c, sin_c = _rope_tables_mla(S, L)

    xs = jnp.concatenate([ctx, x], axis=1)
    prev = None
    for i in range(depth):
        j = i // 2
        if i % 2 == 0:
            lam_init = 0.8 - 0.6 * math.exp(-0.3 * i)
            w, gain = _prep_even(e_w_in[j], a_qn[j], a_kn[j], b_qn[j], b_kn[j])
            xs, (aq, ak, bq, bk, av, bv) = _inproj(
                "even", xs, prev, norm1[i][None], mod(i, 0), mod(i, 1), w,
                [(gain, "full"), (cos64, "rows"), (sin64, "rows")],
                [(512, bf, True), (512, bf, False), (512, bf, False), (128, bf, False), (512, bf, True),
                 (128, bf, False)], LT)
            lqk = jnp.stack([a_lq1[j], a_lk1[j], a_lq2[j], a_lk2[j]])
            o1 = _diff_attn(aq, ak, av, lqk, a_gn[j][None], L, lam_init)
            o2 = _win_attn(bq, bk, bv, b_sink[j], L)
            xs, ht = _outproj(xs, o1, [o2], e_w_out[j].astype(bf), mod(i, 2), norm2[i][None], mod(i, 3), mod(i, 4),
                              LT, gla=False)
        else:
            w, wq, wkv, qn, kn, wg, bg = _prep_odd(o_w_in[j], c_wq_up[j], c_wkv_up[j], c_qn[j], c_kn[j],
                                                   d_wg_f[j], d_bg_f[j], d_wg_b[j], d_bg_b[j])
            xs, (cq, ck, cvv, dq, dk, dv, lg, r) = _inproj(
                "odd", xs, prev, norm1[i][None], mod(i, 0), mod(i, 1), w,
                [(c_gq[j][None], "full"), (c_gkv[j][None], "full"), (wq, "full"), (wkv, "full"), (qn, "full"),
                 (kn, "full"), (wg, "full"), (bg, "full"), (cos_c, "rows"), (sin_c, "rows")],
                [(1024, bf, True), (1024, bf, False), (512, bf, True), (256, F32, False), (256, F32, False),
                 (512, bf, False), (512, F32, False), (512, F32, False)], LT)
            o1 = _mla_attn(cq, ck, cvv, L)
            of = _gla(dq, dk, dv, lg, 0, L)
            ob = _gla(dq, dk, dv, lg, 1, L)
            xs, ht = _outproj(xs, o1, [of, ob, r, d_gn[j][None]], o_w_out[j].astype(bf), mod(i, 2), norm2[i][None],
                              mod(i, 3), mod(i, 4), LT, gla=True)
        wqt = p_wq[i].T.astype(bf)
        keys = p_keys[i].reshape(2 * P_HEADS, P_NKEYS, P_DKEY // 2).astype(bf)
        r2, e2, cnt, cc = _peer_gate(ht, wqt, keys)
        pp = _peer_dense(ht, r2, e2, cnt, cc, p_u[i].astype(bf), p_v[i].T.astype(bf))
        prev = (pp.reshape(B, L + S, D), mod(i, 5))
    return _final(xs, prev[0], prev[1], L)
```

```python
import functools
import math

import numpy as np
import jax
import jax.numpy as jnp
from jax import lax
from jax.experimental import pallas as pl
from jax.experimental.pallas import tpu as pltpu

F32 = jnp.float32
MXU_DTYPE = jnp.bfloat16
GATE_DTYPE = jnp.bfloat16
PACKED_ROWS = 16
PEER_SUB = 256
GATE_HEADS_PER_STEP = 4
PEER_EXPERT_TILE = 2048

GRID_W = 64
ROPE_THETA = 10000.0
EPS = 1e-6
NEG = -1e30
A_HEADS, A_DQK, A_DV = 4, 64, 128
B_HEADS, B_KV_HEADS, B_DH, WINDOW = 8, 2, 64, 128
C_HEADS, C_Q_RANK, C_KV_RANK, C_NOPE, C_ROPE, C_DV = 8, 256, 128, 64, 32, 64
D_HEADS, D_DK, D_DV, D_GATE_RANK, D_GATE_TAU, D_CHUNK = 4, 64, 128, 16, 16.0, 64
P_HEADS, P_NKEYS, P_DKEY, P_TOPK = 8, 128, 256, 16

LANES = 128
ROW_TILE = 256
KEY_CHUNK = 256
KEY_CHUNK_BIG = 768
QUERY_TILE = 512
LOG2_E = math.log2(math.e)
VMEM_LIMIT = 56 * 1024 * 1024


def _cparams(*sem):
    return pltpu.CompilerParams(dimension_semantics=sem, vmem_limit_bytes=VMEM_LIMIT)


def _rms(x, g):
    return x * lax.rsqrt(jnp.mean(x * x, axis=-1, keepdims=True) + EPS) * g


def _nt(a, b):
    return lax.dot_general(a, b, (((1,), (1,)), ((), ())), preferred_element_type=F32)


def _tn(a, b):
    return lax.dot_general(a, b, (((0,), (0,)), ((), ())), preferred_element_type=F32)


def _mm(a, b):
    return jnp.dot(a, b, preferred_element_type=F32)


def _mod_kernel(c_ref, w_ref, b_ref, o_ref):
    cv = c_ref[...]
    s = cv * jax.nn.sigmoid(cv)
    o_ref[0] = jnp.dot(s, w_ref[0], precision=lax.Precision.HIGHEST, preferred_element_type=F32) + b_ref[0]


def _modulation(cvecs, w_mod, b_mod):
    depth, d, n = w_mod.shape
    rows = cvecs.shape[0]
    tn = 1536
    return pl.pallas_call(
        _mod_kernel,
        grid=(depth, n // tn),
        in_specs=[pl.BlockSpec((rows, d), lambda l, j: (0, 0)),
                  pl.BlockSpec((1, d, tn), lambda l, j: (l, 0, j)),
                  pl.BlockSpec((1, 1, tn), lambda l, j: (l, 0, j))],
        out_specs=pl.BlockSpec((1, rows, tn), lambda l, j: (l, 0, j)),
        out_shape=jax.ShapeDtypeStruct((depth, rows, n), F32),
        compiler_params=_cparams("parallel", "parallel"),
        name="modulation",
    )(cvecs, w_mod, b_mod.reshape(depth, 1, n))


def _stream_prologue(has_prev, it):
    xs_ref = next(it)
    x = xs_ref[0]
    if has_prev:
        pp_ref, pg_ref = next(it), next(it)
        x = x + pg_ref[...] * pp_ref[0]
    g1_ref, sh_ref, sc_ref, w_ref = next(it), next(it), next(it), next(it)
    h = _rms(x, g1_ref[...]) * (1.0 + sc_ref[...]) + sh_ref[...]
    z = _mm(h.astype(MXU_DTYPE), w_ref[...])
    return x, z


def _norm_rope_pair64(zg, gain, cosf, sinf):
    lane = lax.broadcasted_iota(jnp.int32, zg.shape, 1)
    lo = lane < 64
    sq = zg * zg
    s_lo = jnp.sum(jnp.where(lo, sq, 0.0), axis=-1, keepdims=True)
    s_hi = jnp.sum(jnp.where(lo, 0.0, sq), axis=-1, keepdims=True)
    ms = jnp.where(lo, s_lo, s_hi) * (1.0 / 64.0)
    y = zg * lax.rsqrt(ms + EPS) * gain
    partner = jnp.where((lane & 63) < 32, pltpu.roll(y, 96, 1), pltpu.roll(y, 32, 1))
    return y * cosf + partner * sinf


def _inproj_even_kernel(has_prev, *refs):
    it = iter(refs)
    x, z = _stream_prologue(has_prev, it)
    gain_ref, cos_ref, sin_ref = next(it), next(it), next(it)
    if has_prev:
        xo_ref = next(it)
        xo_ref[0] = x
    aq_ref, ak_ref, bq_ref, bk_ref, av_ref, bv_ref = (next(it) for _ in range(6))
    cosf, sinf = cos_ref[...], sin_ref[...]
    g = 0
    for ref, ngroups, transposed in ((aq_ref, 4, True), (ak_ref, 4, False), (bq_ref, 4, False), (bk_ref, 1, False)):
        for k in range(ngroups):
            sl = slice(g * LANES, (g + 1) * LANES)
            y = _norm_rope_pair64(z[:, sl], gain_ref[:, sl], cosf, sinf)
            if transposed:
                ref[0, k * LANES:(k + 1) * LANES, :] = y.T.astype(ref.dtype)
            else:
                ref[0, :, k * LANES:(k + 1) * LANES] = y.astype(ref.dtype)
            g += 1
    o = g * LANES
    av_ref[0] = z[:, o:o + 512].T.astype(av_ref.dtype)
    bv_ref[0] = z[:, o + 512:o + 640].astype(bv_ref.dtype)


def _norm_rope_mla(y, gain, cosf, sinf):
    lane = lax.broadcasted_iota(jnp.int32, y.shape, 1)
    ms = jnp.sum(y * y, axis=-1, keepdims=True) * (1.0 / (C_NOPE + C_ROPE))
    y = y * lax.rsqrt(ms + EPS) * gain
    partner = jnp.where(lane < C_NOPE + C_ROPE // 2, pltpu.roll(y, LANES - C_ROPE // 2, 1),
                        pltpu.roll(y, C_ROPE // 2, 1))
    return y * cosf + partner * sinf


def _log_sigmoid(x):
    return jnp.minimum(x, 0.0) - jnp.log1p(jnp.exp(-jnp.abs(x)))


def _inproj_odd_kernel(has_prev, *refs):
    it = iter(refs)
    x, z = _stream_prologue(has_prev, it)
    (gq_ref, gkv_ref, wq_ref, wkv_ref, qn_ref, kn_ref, wg_ref, bg_ref, cos_ref, sin_ref) = (next(it) for _ in range(10))
    if has_prev:
        xo_ref = next(it)
        xo_ref[0] = x
    cq_ref, ck_ref, cv_ref, dq_ref, dk_ref, dv_ref, lg_ref, r_ref = (next(it) for _ in range(8))
    cosf, sinf = cos_ref[...], sin_ref[...]
    qn = _rms(z[:, 0:256], gq_ref[...])
    q_up = _mm(qn.astype(MXU_DTYPE), wq_ref[...])
    kvn = _rms(z[:, 256:384], gkv_ref[...])
    kv_up = _mm(kvn.astype(MXU_DTYPE), wkv_ref[...])
    kpe = pltpu.roll(z[:, 384:512], C_NOPE, 1)
    for h in range(C_HEADS):
        sl = slice(h * LANES, (h + 1) * LANES)
        cq_ref[0, sl, :] = _norm_rope_mla(q_up[:, sl], qn_ref[:, sl], cosf, sinf).T.astype(cq_ref.dtype)
        ck_ref[0, :, sl] = _norm_rope_mla(kv_up[:, sl] + kpe, kn_ref[:, sl], cosf, sinf).astype(ck_ref.dtype)
    cv_ref[0] = kv_up[:, C_HEADS * LANES:].T.astype(cv_ref.dtype)
    dq_ref[0] = z[:, 512:768] * (D_DK ** -0.5)
    dk_ref[0] = z[:, 768:1024]
    dv_ref[0] = z[:, 1024:1536].astype(dv_ref.dtype)
    gl = _mm(z[:, 1536:1664].astype(MXU_DTYPE), wg_ref[...]) + bg_ref[...]
    lg_ref[0] = _log_sigmoid(gl) * (1.0 / D_GATE_TAU)
    r_ref[0] = z[:, 1664:2176]


def _inproj(kind, xs, prev, norm_g, shift, scale, w, extra, out_cols, LT):
    B, T, D = xs.shape
    tm = ROW_TILE
    nT = T // tm
    has_prev = prev is not None
    row = lambda b, i: (b, i, 0)
    cls = lambda b, i: (b, jnp.where(i >= LT, 1, 0), 0, 0)
    full2 = lambda b, i: (0, 0)
    args, specs = [xs], [pl.BlockSpec((1, tm, D), row)]
    if has_prev:
        args += [prev[0], prev[1]]
        specs += [pl.BlockSpec((1, tm, D), row), pl.BlockSpec((None, None, 1, D), cls)]
    args += [norm_g, shift, scale, w]
    specs += [pl.BlockSpec((1, D), full2), pl.BlockSpec((None, None, 1, D), cls),
              pl.BlockSpec((None, None, 1, D), cls), pl.BlockSpec(w.shape, full2)]
    for arr, k in extra:
        args.append(arr)
        if k == "full":
            specs.append(pl.BlockSpec(arr.shape, full2))
        else:
            specs.append(pl.BlockSpec((tm, arr.shape[1]), lambda b, i: (i, 0)))
    out_shape, out_specs = [], []
    if has_prev:
        out_shape.append(jax.ShapeDtypeStruct((B, T, D), F32))
        out_specs.append(pl.BlockSpec((1, tm, D), row))
    for n, dt, transposed in out_cols:
        if transposed:
            out_shape.append(jax.ShapeDtypeStruct((B, n, T), dt))
            out_specs.append(pl.BlockSpec((1, n, tm), lambda b, i: (b, 0, i)))
        else:
            out_shape.append(jax.ShapeDtypeStruct((B, T, n), dt))
            out_specs.append(pl.BlockSpec((1, tm, n), row))
    body = _inproj_even_kernel if kind == "even" else _inproj_odd_kernel
    outs = pl.pallas_call(
        functools.partial(body, has_prev),
        grid=(B, nT), in_specs=specs, out_specs=out_specs, out_shape=out_shape,
        compiler_params=_cparams("parallel", "parallel"),
        name="inproj_" + kind,
    )(*args)
    if has_prev:
        return outs[0], outs[1:]
    return xs, outs


def _flash_streams(ctx, q_streams, k_cols, v_rows, k_ref, vt_ref, ml_ref, acc_ref, sc_ref=None, p_ref=None):
    T = k_ref.shape[1]
    tq = q_streams[0].shape[1]
    acc_ref[...] = jnp.zeros(acc_ref.shape, F32)
    init = (jnp.full((1, tq), -jnp.inf, F32), jnp.zeros((1, tq), F32)) * 2

    def chunk(start, size, ml):
        out = []
        scs = [_mm(k_ref[0, start:start + size, k_cols[s]], q_streams[s]) for s in range(2)]
        for s in range(2):
            m_prev, l_prev = ml[2 * s], ml[2 * s + 1]
            m_new = jnp.maximum(m_prev, jnp.max(scs[s], axis=0, keepdims=True))
            alpha = jnp.exp2(m_prev - m_new)
            p = jnp.exp2(scs[s] - m_new)
            acc_ref[s] = alpha * acc_ref[s] + _mm(vt_ref[0, v_rows[s], start:start + size], p.astype(MXU_DTYPE))
            out += [m_new, alpha * l_prev + jnp.sum(p, axis=0, keepdims=True)]
        return tuple(out)

    if ctx:
        ml = init
        for c in range(0, T, KEY_CHUNK):
            ml = chunk(c, KEY_CHUNK, ml)
        ml_ref[...] = jnp.concatenate(ml, axis=0)
        return

    big = sc_ref.shape[1]
    piece = KEY_CHUNK
    pieces = big // piece
    n_chunks = T // big

    def piece_scores(c, j, s):
        start = pl.multiple_of(c * big + j * piece, piece)
        return _mm(k_ref[0, pl.ds(start, piece), k_cols[s]], q_streams[s])

    def col_max(parts):
        return functools.reduce(jnp.maximum, [jnp.max(x, axis=0, keepdims=True) for x in parts])

    def value_product(c, s):
        start = pl.multiple_of(c * big, big)
        return _mm(vt_ref[0, v_rows[s], pl.ds(start, big)], p_ref[s])

    def pipelined(t, carry, look_ahead=True):
        ml, mx, alpha_prev = carry
        pv = [value_product(jnp.maximum(t - 1, 0), s) for s in range(2)]
        m_new = [jnp.maximum(ml[2 * s], mx[s]) for s in range(2)]
        alpha = [jnp.exp2(ml[2 * s] - m_new[s]) for s in range(2)]
        ahead, sums = ([], []), ([], [])
        for j in range(pieces):
            rows = slice(j * piece, (j + 1) * piece)
            if look_ahead:
                for s in range(2):
                    ahead[s].append(piece_scores(t + 1, j, s))
            for s in range(2):
                p = jnp.exp2(sc_ref[s, rows, :] - m_new[s])
                sums[s].append(jnp.sum(p, axis=0, keepdims=True))
                p_ref[s, rows, :] = p.astype(p_ref.dtype)
            if look_ahead:
                for s in range(2):
                    sc_ref[s, rows, :] = ahead[s][j]
        new_ml = []
        for s in range(2):
            acc_ref[s] = alpha_prev[s] * acc_ref[s] + pv[s]
            new_ml += [m_new[s], alpha[s] * ml[2 * s + 1] + functools.reduce(jnp.add, sums[s])]
        mx_next = tuple(col_max(ahead[s]) for s in range(2)) if look_ahead else mx
        return tuple(new_ml), mx_next, tuple(alpha)

    first = []
    for s in range(2):
        parts = [piece_scores(0, j, s) for j in range(pieces)]
        for j in range(pieces):
            sc_ref[s, j * piece:(j + 1) * piece, :] = parts[j]
        first.append(col_max(parts))
    p_ref[...] = jnp.zeros(p_ref.shape, p_ref.dtype)
    one = jnp.ones((1, tq), F32)
    carry = lax.fori_loop(0, n_chunks - 1, pipelined, (init, tuple(first), (one, one)))
    ml, _, alpha_last = pipelined(n_chunks - 1, carry, look_ahead=False)
    for s in range(2):
        acc_ref[s] = alpha_last[s] * acc_ref[s] + value_product(n_chunks - 1, s)
    ml_ref[...] = jnp.concatenate(ml, axis=0)


def _query_tile(n_q):
    return QUERY_TILE if n_q % QUERY_TILE == 0 else ROW_TILE


def _big_chunk(T):
    return KEY_CHUNK_BIG if T % KEY_CHUNK_BIG == 0 else KEY_CHUNK


def _flash_scratch(ctx, T, dv, tq):
    scratch = [pltpu.VMEM((4, tq), F32), pltpu.VMEM((2, dv, tq), F32)]
    if not ctx:
        scratch += [pltpu.VMEM((2, _big_chunk(T), tq), F32), pltpu.VMEM((2, _big_chunk(T), tq), MXU_DTYPE)]
    return scratch


def _split_queries(qt, L):
    return qt[:, :, :L], qt[:, :, L:]


def _diff_attn_kernel(ctx, lam_init, q_ref, k_ref, vt_ref, lqk_ref, gn_ref, o_ref, ml_ref, acc_ref, *pipe):
    q = q_ref[0]
    row = lax.broadcasted_iota(jnp.int32, q.shape, 0)
    zero = jnp.zeros_like(q)
    streams = (jnp.where(row < A_DQK, q, zero), jnp.where(row < A_DQK, zero, q))
    every = slice(None)
    _flash_streams(ctx, streams, (every, every), (every, every), k_ref, vt_ref, ml_ref, acc_ref, *pipe)
    lqk = lqk_ref[...]
    lam = (jnp.exp(jnp.sum(lqk[0:1] * lqk[1:2], axis=-1, keepdims=True))
           - jnp.exp(jnp.sum(lqk[2:3] * lqk[3:4], axis=-1, keepdims=True)) + lam_init)
    o = acc_ref[0] / ml_ref[1:2, :] - lam * (acc_ref[1] / ml_ref[3:4, :])
    o = o * lax.rsqrt(jnp.mean(o * o, axis=0, keepdims=True) + EPS) * gn_ref[...]
    o_ref[0] = (o * (1.0 - lam_init)).astype(o_ref.dtype)


def _diff_attn(qt, k, vt, lqk, gn, L, lam_init):
    B, T, _ = k.shape
    assert L % KEY_CHUNK == 0 and T % KEY_CHUNK == 0

    def call(ctx, q, keys, tq):
        n_q = q.shape[2]
        gn_b = jnp.broadcast_to(gn.reshape(A_DV, 1), (A_DV, tq))
        return pl.pallas_call(
            functools.partial(_diff_attn_kernel, ctx, lam_init),
            grid=(B, A_HEADS, n_q // tq),
            in_specs=[pl.BlockSpec((1, LANES, tq), lambda b, h, i: (b, h, i)),
                      pl.BlockSpec((1, keys, LANES), lambda b, h, i: (b, 0, h)),
                      pl.BlockSpec((1, A_DV, keys), lambda b, h, i: (b, h, 0)),
                      pl.BlockSpec(lqk.shape, lambda b, h, i: (0, 0)),
                      pl.BlockSpec(gn_b.shape, lambda b, h, i: (0, 0))],
            out_specs=pl.BlockSpec((1, A_DV, tq), lambda b, h, i: (b, h, i)),
            out_shape=jax.ShapeDtypeStruct((B, A_HEADS * A_DV, n_q), MXU_DTYPE),
            scratch_shapes=_flash_scratch(ctx, keys, A_DV, tq),
            compiler_params=_cparams("parallel", "parallel", "parallel"),
            name="diff_attn_ctx" if ctx else "diff_attn",
        )(q, k, vt, lqk, gn_b)

    q_ctx, q_lat = _split_queries(qt, L)
    return jnp.concatenate([call(True, q_ctx, L, ROW_TILE), call(False, q_lat, T, _query_tile(T - L))], axis=2)


def _mla_attn_kernel(ctx, q_ref, k_ref, vt_ref, o_ref, ml_ref, acc_ref, *pipe):
    streams = (q_ref[0, :LANES, :], q_ref[0, LANES:, :])
    k_cols = (slice(0, LANES), slice(LANES, 2 * LANES))
    v_rows = (slice(0, C_DV), slice(C_DV, 2 * C_DV))
    _flash_streams(ctx, streams, k_cols, v_rows, k_ref, vt_ref, ml_ref, acc_ref, *pipe)
    o_ref[0] = jnp.concatenate([acc_ref[0] / ml_ref[1:2, :], acc_ref[1] / ml_ref[3:4, :]], axis=0).astype(o_ref.dtype)


def _mla_attn(qt, k, vt, L):
    B, T, _ = k.shape
    assert L % KEY_CHUNK == 0 and T % KEY_CHUNK == 0

    def call(ctx, q, keys, tq):
        n_q = q.shape[2]
        return pl.pallas_call(
            functools.partial(_mla_attn_kernel, ctx),
            grid=(B, C_HEADS // 2, n_q // tq),
            in_specs=[pl.BlockSpec((1, 2 * LANES, tq), lambda b, h, i: (b, h, i)),
                      pl.BlockSpec((1, keys, 2 * LANES), lambda b, h, i: (b, 0, h)),
                      pl.BlockSpec((1, 2 * C_DV, keys), lambda b, h, i: (b, h, 0))],
            out_specs=pl.BlockSpec((1, 2 * C_DV, tq), lambda b, h, i: (b, h, i)),
            out_shape=jax.ShapeDtypeStruct((B, C_HEADS * C_DV, n_q), MXU_DTYPE),
            scratch_shapes=_flash_scratch(ctx, keys, C_DV, tq),
            compiler_params=_cparams("parallel", "parallel", "parallel"),
            name="mla_attn_ctx" if ctx else "mla_attn",
        )(q, k, vt)

    q_ctx, q_lat = _split_queries(qt, L)
    return jnp.concatenate([call(True, q_ctx, L, ROW_TILE), call(False, q_lat, T, _query_tile(T - L))], axis=2)


def _win_attn_kernel(LT, sink_ref, q_ref, kc_ref, kp_ref, kk_ref, kn_ref, vc_ref, vp_ref, vk_ref, vn_ref, o_ref):
    i, nT = pl.program_id(1), pl.num_programs(1)
    tq = q_ref.shape[1]
    tk = kc_ref.shape[1]
    r = lax.broadcasted_iota(jnp.int32, (2 * tq, tk), 0)
    c = lax.broadcasted_iota(jnp.int32, (2 * tq, tk), 1)
    d = c - jnp.where(r >= tq, r - tq, r)
    is_x = i >= LT
    far = 4 * tk
    ok_p = d >= jnp.where(jnp.logical_and(is_x, i - 1 >= LT), tk - WINDOW, far)
    ok_k = jnp.abs(d) <= jnp.where(is_x, WINDOW, -1)
    ok_n = d <= jnp.where(jnp.logical_and(is_x, i + 1 <= nT - 1), WINDOW - tk, -far)
    k_tiles = (kc_ref[0], kp_ref[0], kk_ref[0], kn_ref[0])
    masks = (None, ok_p, ok_k, ok_n)
    vals = jnp.concatenate([vc_ref[0], vp_ref[0], vk_ref[0], vn_ref[0]], axis=0)
    lane = lax.broadcasted_iota(jnp.int32, (tq, LANES), 1)
    lo = lane < B_DH
    rows = lax.broadcasted_iota(jnp.int32, (2 * tq, 1), 0)
    group = B_HEADS // B_KV_HEADS
    for g in range(B_HEADS // 2):
        kv = (2 * g) // group
        qg = q_ref[0, :, g * LANES:(g + 1) * LANES].astype(F32)
        qr = pltpu.roll(qg, B_DH, 1)
        if kv == 0:
            qa, qb = jnp.where(lo, qg, 0.0), jnp.where(lo, qr, 0.0)
        else:
            qa, qb = jnp.where(lo, 0.0, qr), jnp.where(lo, 0.0, qg)
        qs = jnp.concatenate([qa, qb], axis=0).astype(MXU_DTYPE)
        s = jnp.concatenate([_nt(qs, kt) if ok is None else jnp.where(ok, _nt(qs, kt), NEG)
                             for kt, ok in zip(k_tiles, masks)], axis=1)
        sk = jnp.where(rows < tq, sink_ref[2 * g], sink_ref[2 * g + 1])
        m = jnp.maximum(jnp.max(s, axis=-1, keepdims=True), sk)
        p = jnp.exp(s - m)
        l = jnp.sum(p, axis=-1, keepdims=True) + jnp.exp(sk - m)
        acc = _mm(p.astype(MXU_DTYPE), vals) / l
        if kv == 0:
            out = jnp.where(lo, acc[:tq], pltpu.roll(acc[tq:], B_DH, 1))
        else:
            out = jnp.where(lo, pltpu.roll(acc[:tq], B_DH, 1), acc[tq:])
        o_ref[0, :, g * LANES:(g + 1) * LANES] = out.astype(o_ref.dtype)


def _win_attn(q, k, v, sink, L):
    B, T, _ = q.shape
    t = ROW_TILE
    assert L == t and WINDOW <= t
    nT, LT = T // t, L // t
    kspec = lambda f: pl.BlockSpec((1, t, LANES), f)
    maps = [lambda b, i: (b, 0, 0),
            lambda b, i: (b, jnp.maximum(i - 1, LT), 0),
            lambda b, i: (b, i, 0),
            lambda b, i: (b, jnp.minimum(i + 1, nT - 1), 0)]
    return pl.pallas_call(
        functools.partial(_win_attn_kernel, LT),
        grid=(B, nT),
        in_specs=[pl.BlockSpec(memory_space=pltpu.SMEM),
                  pl.BlockSpec((1, t, B_HEADS * B_DH), lambda b, i: (b, i, 0))]
                 + [kspec(f) for f in maps] + [kspec(f) for f in maps],
        out_specs=pl.BlockSpec((1, t, B_HEADS * B_DH), lambda b, i: (b, i, 0)),
        out_shape=jax.ShapeDtypeStruct((B, T, B_HEADS * B_DH), MXU_DTYPE),
        compiler_params=_cparams("parallel", "parallel"),
        name="window_attn",
    )(sink, q, k, k, k, k, v, v, v, v)


def _gla_kernel(reverse, q_ref, k_ref, v_ref, g_ref, o_ref, st_ref):
    @pl.when(pl.program_id(1) == 0)
    def _():
        st_ref[...] = jnp.zeros(st_ref.shape, F32)

    C = D_CHUNK
    tb = q_ref.shape[1]
    r = lax.broadcasted_iota(jnp.int32, (C, C), 0)
    c = lax.broadcasted_iota(jnp.int32, (C, C), 1)
    keep = (c >= r) if reverse else (c <= r)
    tri = jnp.where(keep, 1.0, 0.0).astype(F32)
    chunks = range(tb // C)
    for ch in (reversed(chunks) if reverse else chunks):
        rows = slice(ch * C, (ch + 1) * C)
        b = jnp.dot(tri, g_ref[0, rows, :], precision=lax.Precision.HIGHEST, preferred_element_type=F32)
        b_last = b[0:1] if reverse else b[C - 1:C]
        q_t = q_ref[0, rows, :] * jnp.exp(b)
        kk = k_ref[0, rows, :]
        k_in = kk * jnp.exp(-b)
        k_st = kk * jnp.exp(b_last - b)
        decay = jnp.exp(b_last)
        for h in range(D_HEADS):
            sl = slice(h * D_DK, (h + 1) * D_DK)
            vsl = slice(h * D_DV, (h + 1) * D_DV)
            qh = q_t[:, sl].astype(MXU_DTYPE)
            a = jnp.where(keep, _nt(qh, k_in[:, sl].astype(MXU_DTYPE)), 0.0)
            vh = v_ref[0, rows, vsl]
            st = st_ref[h]
            o_ref[0, rows, vsl] = _mm(a.astype(MXU_DTYPE), vh) + _nt(qh, st.astype(MXU_DTYPE))
            st_ref[h] = st * decay[:, sl] + _tn(vh, k_st[:, sl].astype(MXU_DTYPE))


def _gla(q, k, v, lg, direction, L):
    B, T, _ = q.shape
    tb = ROW_TILE
    nB, LB = T // tb, L // tb
    reverse = direction == 1
    if reverse:
        blk = lambda s: jnp.where(s < LB, LB - 1 - s, LB + nB - 1 - s)
    else:
        blk = lambda s: s
    return pl.pallas_call(
        functools.partial(_gla_kernel, reverse),
        grid=(B, nB),
        in_specs=[pl.BlockSpec((1, tb, D_HEADS * D_DK), lambda b, s: (b, blk(s), 0)),
                  pl.BlockSpec((1, tb, D_HEADS * D_DK), lambda b, s: (b, blk(s), 0)),
                  pl.BlockSpec((1, tb, D_HEADS * D_DV), lambda b, s: (b, blk(s), 0)),
                  pl.BlockSpec((1, tb, D_HEADS * D_DK), lambda b, s: (b, blk(s), direction))],
        out_specs=pl.BlockSpec((1, tb, D_HEADS * D_DV), lambda b, s: (b, blk(s), 0)),
        out_shape=jax.ShapeDtypeStruct((B, T, D_HEADS * D_DV), F32),
        scratch_shapes=[pltpu.VMEM((D_HEADS, D_DV, D_DK), F32)],
        compiler_params=_cparams("parallel", "arbitrary"),
        name="gla_bwd" if reverse else "gla_fwd",
    )(q, k, v, lg)


def _outproj_kernel(gla, *refs):
    it = iter(refs)
    xs_ref, o1_ref = next(it), next(it)
    if gla:
        of_ref, ob_ref, r_ref, dgn_ref = (next(it) for _ in range(4))
        og = of_ref[0] + ob_ref[0]
        rr = r_ref[0]
        parts = []
        for h in range(D_HEADS):
            sl = slice(h * D_DV, (h + 1) * D_DV)
            rh = rr[:, sl]
            parts.append((_rms(og[:, sl], dgn_ref[...]) * (rh * jax.nn.sigmoid(rh))).astype(MXU_DTYPE))
        o2 = jnp.concatenate(parts, axis=1)
    else:
        o2 = next(it)[0]
    w_ref, gate_ref, n2_ref, sh_ref, sc_ref, xo_ref, ht_ref = (next(it) for _ in range(7))
    half = o1_ref.shape[1]
    x = xs_ref[0] + gate_ref[...] * (_tn(o1_ref[0], w_ref[:half, :]) + _mm(o2, w_ref[half:, :]))
    xo_ref[0] = x
    h2 = _rms(x, n2_ref[...]) * (1.0 + sc_ref[...]) + sh_ref[...]
    ht_ref[...] = h2.T.astype(ht_ref.dtype)


def _outproj(xs, o1, o2s, w, gate, norm_g, shift, scale, LT, gla):
    B, T, D = xs.shape
    tm = ROW_TILE
    nT = T // tm
    row = lambda b, i: (b, i, 0)
    cls = lambda b, i: (b, jnp.where(i >= LT, 1, 0), 0, 0)
    full2 = lambda b, i: (0, 0)
    args = [xs, o1]
    specs = [pl.BlockSpec((1, tm, D), row), pl.BlockSpec((1, o1.shape[1], tm), lambda b, i: (b, 0, i))]
    for a in o2s:
        args.append(a)
        specs.append(pl.BlockSpec(a.shape, full2) if a.ndim == 2 else pl.BlockSpec((1, tm, a.shape[2]), row))
    args += [w, gate, norm_g, shift, scale]
    specs += [pl.BlockSpec(w.shape, full2), pl.BlockSpec((None, None, 1, D), cls), pl.BlockSpec((1, D), full2),
              pl.BlockSpec((None, None, 1, D), cls), pl.BlockSpec((None, None, 1, D), cls)]
    return pl.pallas_call(
        functools.partial(_outproj_kernel, gla),
        grid=(B, nT), in_specs=specs,
        out_specs=[pl.BlockSpec((1, tm, D), row), pl.BlockSpec((D, tm), lambda b, i: (0, b * nT + i))],
        out_shape=[jax.ShapeDtypeStruct((B, T, D), F32), jax.ShapeDtypeStruct((D, B * T), MXU_DTYPE)],
        compiler_params=_cparams("parallel", "parallel"),
        name="outproj_gla" if gla else "outproj",
    )(*args)


def _top_values(s, k, with_rank=False):
    vals = []
    rank = jnp.full(s.shape, float(k), F32) if with_rank else None
    for a in range(k):
        m = jnp.max(s, axis=0, keepdims=True)
        vals.append(m)
        hit = s == m
        if with_rank:
            rank = jnp.where(hit, float(a), rank)
        if a + 1 < k:
            s = jnp.where(hit, -jnp.inf, s)
    vals = jnp.concatenate(vals, axis=0)
    return (vals, rank) if with_rank else vals


def _merge_exchanges(lo, n, r):
    step = r * 2
    if step < n:
        pairs = _merge_exchanges(lo, n, step) + _merge_exchanges(lo + r, n, step)
        return pairs + [(i, i + r) for i in range(lo + r, lo + n - r, step)]
    return [(lo, lo + r)]


def _sort_exchanges(lo, n):
    if n <= 1:
        return []
    half = n // 2
    return _sort_exchanges(lo, half) + _sort_exchanges(lo + half, half) + _merge_exchanges(lo, n, 1)


def _top_values_sorted(s, k):
    slabs = [s[8 * r:8 * (r + 1)] for r in range(s.shape[0] // 8)]
    assert len(slabs) == k
    for i, j in _sort_exchanges(0, k):
        slabs[i], slabs[j] = jnp.maximum(slabs[i], slabs[j]), jnp.minimum(slabs[i], slabs[j])
    vals = []
    for a in range(k):
        m = jnp.max(slabs[0], axis=0, keepdims=True)
        vals.append(m)
        hit = slabs[0] == m
        for r in range(k - a - 1):
            slabs[r] = jnp.where(hit, slabs[r + 1], slabs[r])
    return jnp.concatenate(vals, axis=0)


def _peer_gate_kernel(ht_ref, wqt_ref, keys_ref, *out_refs):
    n = P_NKEYS
    scores = []
    for g in range(wqt_ref.shape[0] // P_DKEY):
        qt = _mm(wqt_ref[g * P_DKEY:(g + 1) * P_DKEY, :], ht_ref[...])
        scores.append((_mm(keys_ref[2 * g], qt[:n].astype(MXU_DTYPE)),
                       _mm(keys_ref[2 * g + 1], qt[n:].astype(MXU_DTYPE))))
    for g, (s1, s2) in enumerate(scores):
        _peer_gate_head(s1, s2, g, *out_refs)


def _peer_gate_head(s1, s2, g, r2_ref, e2_ref, cnt_ref, cc_ref):
    k = P_TOPK
    sv1 = _top_values_sorted(s1, k)
    sv2, rank2 = _top_values(s2, k, with_rank=True)
    cand = jnp.concatenate([sv1[a:a + 1] + sv2[:k // (a + 1)] for a in range(k)], axis=0)
    tau = _top_values(cand, k)[k - 1:k]
    top = sv1[0:1] + sv2[0:1]
    z = jnp.sum(jnp.where(cand >= tau, jnp.exp(cand - top), 0.0), axis=0, keepdims=True)
    cnt = jnp.zeros(s1.shape, F32)
    for b in range(k // 2):
        cnt = cnt + jnp.where(s1 + sv2[b:b + 1] >= tau, 1.0, 0.0)
    extra = functools.reduce(jnp.add, [jnp.where(sv1[0:1] + sv2[b:b + 1] >= tau, 1.0, 0.0) for b in range(k // 2, k)])
    cnt = cnt + jnp.where(s1 == sv1[0:1], extra, 0.0)
    r2_ref[g] = rank2.astype(r2_ref.dtype)
    e2_ref[g] = jnp.exp(s2 - sv2[0:1]).astype(e2_ref.dtype)
    cnt_ref[g] = cnt
    cc_ref[g] = jnp.exp(s1 - sv1[0:1]) * (math.sqrt(0.5) / z)


def _peer_token_tile(n_tok):
    return 512 if n_tok % 512 == 0 else 256


def _peer_gate(ht, wqt, keys):
    D, N = ht.shape
    tm = _peer_token_tile(N)
    shape = lambda dt: jax.ShapeDtypeStruct((P_HEADS, P_NKEYS, N), dt)
    hg = GATE_HEADS_PER_STEP
    ospec = pl.BlockSpec((hg, P_NKEYS, tm), lambda i, h: (h, 0, i))
    return pl.pallas_call(
        _peer_gate_kernel,
        grid=(N // tm, P_HEADS // hg),
        in_specs=[pl.BlockSpec((D, tm), lambda i, h: (0, i)),
                  pl.BlockSpec((hg * P_DKEY, D), lambda i, h: (h, 0)),
                  pl.BlockSpec((2 * hg, P_NKEYS, P_DKEY // 2), lambda i, h: (h, 0, 0))],
        out_specs=[ospec] * 4,
        out_shape=[shape(GATE_DTYPE), shape(GATE_DTYPE), shape(F32), shape(F32)],
        compiler_params=_cparams("parallel", "arbitrary"),
        name="peer_gate",
    )(ht, wqt, keys)


def _rows_packed(row, n):
    tile = jnp.broadcast_to(row, (PACKED_ROWS, row.shape[1])).astype(GATE_DTYPE)
    return jnp.concatenate([tile] * (n // PACKED_ROWS), axis=0)


def _peer_dense_kernel(ht_ref, r2_ref, e2_ref, cnt_ref, cc_ref, u_ref, vt_ref, o_ref, acc_ref):
    j = pl.program_id(1)

    @pl.when(j == 0)
    def _():
        acc_ref[...] = jnp.zeros(acc_ref.shape, F32)

    n = P_NKEYS
    zero = jnp.zeros((n, ht_ref.shape[1]), GATE_DTYPE)
    ht = ht_ref[...]
    total = None
    nsub = u_ref.shape[0] // PEER_SUB
    scores = lambda k: _mm(u_ref[k * PEER_SUB:(k + 1) * PEER_SUB, :], ht)
    zt_next = scores(0)
    for k in range(nsub):
        zt = zt_next
        if k + 1 < nsub:
            zt_next = scores(k + 1)
        rows = []
        for ii in range(PEER_SUB // n):
            i1 = k * (PEER_SUB // n) + ii
            g = None
            for h in range(P_HEADS):
                sel = r2_ref[h] < _rows_packed(cnt_ref[h, i1:i1 + 1, :], n)
                t = jnp.where(sel, e2_ref[h], zero) * _rows_packed(cc_ref[h, i1:i1 + 1, :], n)
                g = t if g is None else g + t
            z = zt[ii * n:(ii + 1) * n]
            gelu = z * (1.0 + lax.erf(z))
            rows.append((gelu.astype(GATE_DTYPE) * g).astype(MXU_DTYPE))
        acc_ref[...] += _mm(vt_ref[:, k * PEER_SUB:(k + 1) * PEER_SUB], jnp.concatenate(rows, axis=0))

    @pl.when(j == pl.num_programs(1) - 1)
    def _():
        o_ref[...] = acc_ref[...].T


def _peer_dense(ht, r2, e2, cnt, cc, u_all, vt_all, layer):
    D, N = ht.shape
    E = u_all.shape[1]
    tm = _peer_token_tile(N)
    te = PEER_EXPERT_TILE
    gspec = pl.BlockSpec((P_HEADS, P_NKEYS, tm), lambda i, j: (0, 0, i))
    rspec = pl.BlockSpec((P_HEADS, te // P_NKEYS, tm), lambda i, j: (0, j, i))
    return pl.pallas_call(
        _peer_dense_kernel,
        grid=(N // tm, E // te),
        in_specs=[pl.BlockSpec((D, tm), lambda i, j: (0, i)), gspec, gspec, rspec, rspec,
                  pl.BlockSpec((None, te, D), lambda i, j: (layer, j, 0)),
                  pl.BlockSpec((None, D, te), lambda i, j: (layer, 0, j))],
        out_specs=pl.BlockSpec((tm, D), lambda i, j: (i, 0)),
        out_shape=jax.ShapeDtypeStruct((N, D), F32),
        scratch_shapes=[pltpu.VMEM((D, tm), F32)],
        compiler_params=_cparams("parallel", "arbitrary"),
        name="peer_dense",
    )(ht, r2, e2, cnt, cc, u_all, vt_all)


def _final_kernel(xs_ref, pp_ref, g_ref, o_ref):
    o_ref[0] = xs_ref[0] + g_ref[...] * pp_ref[0]


def _final(xs, pp, gate, L):
    B, T, D = xs.shape
    tm = ROW_TILE
    LT = L // tm
    src = lambda b, i: (b, i + LT, 0)
    return pl.pallas_call(
        _final_kernel,
        grid=(B, (T - L) // tm),
        in_specs=[pl.BlockSpec((1, tm, D), src), pl.BlockSpec((1, tm, D), src),
                  pl.BlockSpec((None, None, 1, D), lambda b, i: (b, 1, 0, 0))],
        out_specs=pl.BlockSpec((1, tm, D), lambda b, i: (b, i, 0)),
        out_shape=jax.ShapeDtypeStruct((B, T - L, D), F32),
        compiler_params=_cparams("parallel", "parallel"),
        name="final_residual",
    )(xs, pp, gate)


def _even_odd(n):
    return np.concatenate([np.arange(0, n, 2), np.arange(1, n, 2)])


def _rope_angles(S, L, rot_dim):
    pos = jnp.arange(S)
    r = (pos // GRID_W).astype(F32)
    cc = (pos % GRID_W).astype(F32)
    axis_dim = rot_dim // 2
    inv = 1.0 / (ROPE_THETA ** (jnp.arange(0, axis_dim, 2, dtype=F32) / axis_dim))
    ang = jnp.concatenate([r[:, None] * inv, cc[:, None] * inv], axis=-1)
    cos = jnp.concatenate([jnp.ones((L, rot_dim // 2), F32), jnp.cos(ang)], axis=0)
    sin = jnp.concatenate([jnp.zeros((L, rot_dim // 2), F32), jnp.sin(ang)], axis=0)
    return cos, sin


def _rope_tables_64(S, L):
    cos, sin = _rope_angles(S, L, 64)
    return jnp.tile(jnp.concatenate([cos, cos], -1), (1, 2)), jnp.tile(jnp.concatenate([-sin, sin], -1), (1, 2))


def _rope_tables_mla(S, L):
    cos, sin = _rope_angles(S, L, C_ROPE)
    T = cos.shape[0]
    one, zero = jnp.ones((T, C_NOPE), F32), jnp.zeros((T, C_NOPE), F32)
    pad1, pad0 = jnp.ones((T, LANES - C_NOPE - C_ROPE), F32), jnp.zeros((T, LANES - C_NOPE - C_ROPE), F32)
    return (jnp.concatenate([one, cos, cos, pad1], -1), jnp.concatenate([zero, -sin, sin, pad0], -1))


def _prep_even(w_in, a_qn, a_kn, b_qn, b_kn):
    sizes = (512, 512, 512, 512, 128, 128)
    off = np.concatenate([[0], np.cumsum(sizes)])
    p64 = _even_odd(64)

    def seg(k, permute):
        idx = np.arange(off[k], off[k + 1])
        if permute:
            idx = idx.reshape(-1, 64)[:, p64].reshape(-1)
        return idx

    cols = np.concatenate([seg(0, True), seg(1, True), seg(3, True), seg(4, True), seg(2, False), seg(5, False)])
    w = w_in[:, cols].astype(MXU_DTYPE)
    q_scale = A_DQK ** -0.5 * LOG2_E
    gain = jnp.concatenate([jnp.tile(a_qn[p64], 8) * q_scale, jnp.tile(a_kn[p64], 8),
                            jnp.tile(b_qn[p64], 8) * (B_DH ** -0.5), jnp.tile(b_kn[p64], 2)])[None, :]
    return w, gain


def _pad_cols(a, n):
    return jnp.concatenate([a, jnp.zeros(a.shape[:-1] + (n - a.shape[-1],), a.dtype)], axis=-1)


def _prep_odd(w_in, c_wq_up, c_wkv_up, c_qn, c_kn, d_wg_f, d_bg_f, d_wg_b, d_bg_b):
    D = w_in.shape[0]
    p32 = _even_odd(C_ROPE)
    o = np.concatenate([[0], np.cumsum((C_Q_RANK, C_KV_RANK, C_ROPE, 256, 256, 512, D_GATE_RANK, D_GATE_RANK, 512))])
    seg = lambda k: w_in[:, o[k]:o[k + 1]]
    w = jnp.concatenate([seg(0), seg(1), _pad_cols(seg(2)[:, p32], LANES), seg(3), seg(4), seg(5),
                         _pad_cols(jnp.concatenate([seg(6), seg(7)], -1), LANES), seg(8)], axis=-1).astype(MXU_DTYPE)
    hd = C_NOPE + C_ROPE
    head_perm = np.concatenate([np.arange(C_NOPE), C_NOPE + p32])
    wq = c_wq_up.reshape(C_Q_RANK, C_HEADS, hd)[:, :, head_perm]
    wq = _pad_cols(wq, LANES).reshape(C_Q_RANK, C_HEADS * LANES).astype(MXU_DTYPE)
    wkv = c_wkv_up.reshape(C_KV_RANK, C_HEADS, C_NOPE + C_DV)
    wk = _pad_cols(wkv[:, :, :C_NOPE], LANES).reshape(C_KV_RANK, C_HEADS * LANES)
    wv = wkv[:, :, C_NOPE:].reshape(C_KV_RANK, C_HEADS * C_DV)
    wkv = jnp.concatenate([wk, wv], axis=-1).astype(MXU_DTYPE)
    qn = jnp.tile(_pad_cols(c_qn[head_perm] * (hd ** -0.5 * LOG2_E), LANES), C_HEADS)[None, :]
    kn = jnp.tile(_pad_cols(c_kn[head_perm], LANES), C_HEADS)[None, :]
    n = D_HEADS * D_DK
    wg = jnp.zeros((LANES, 2 * n), F32)
    wg = wg.at[:D_GATE_RANK, :n].set(d_wg_f).at[D_GATE_RANK:2 * D_GATE_RANK, n:].set(d_wg_b).astype(MXU_DTYPE)
    bg = jnp.concatenate([d_bg_f, d_bg_b])[None, :]
    return w, wq, wkv, qn, kn, wg, bg


def kernel(x, c, ctx, c_ctx, w_mod, b_mod, norm1, norm2, e_w_in, e_w_out, a_qn, a_kn, a_lq1, a_lk1, a_lq2, a_lk2, a_gn, b_qn, b_kn, b_sink, o_w_in, o_w_out, c_gq, c_gkv, c_wq_up, c_wkv_up, c_qn, c_kn, d_wg_f, d_bg_f, d_wg_b, d_bg_b, d_gn, p_wq, p_keys, p_u, p_v):
    B, S, D = x.shape
    L = ctx.shape[1]
    depth = w_mod.shape[0]
    assert L == ROW_TILE and S % ROW_TILE == 0 and S % GRID_W == 0
    LT = L // ROW_TILE
    bf = MXU_DTYPE

    cv = jnp.concatenate([c, c_ctx[None, :], jnp.zeros((8 - B - 1, D), F32)], axis=0)
    mods = _modulation(cv, w_mod, b_mod).reshape(depth, 8, 6, D)
    mod_tab = jnp.stack([jnp.broadcast_to(mods[:, B][:, None], (depth, B, 6, D)), mods[:, :B]], axis=2)
    mod = lambda i, k: mod_tab[i, :, :, k][:, :, None, :]

    cos64, sin64 = _rope_tables_64(S, L)
    u_all = (p_u * math.sqrt(0.5)).astype(bf)
    vt_all = jnp.swapaxes(p_v, 1, 2).astype(bf)
    cos_c, sin_c = _rope_tables_mla(S, L)

    xs = jnp.concatenate([ctx, x], axis=1)
    prev = None
    for i in range(depth):
        j = i // 2
        if i % 2 == 0:
            lam_init = 0.8 - 0.6 * math.exp(-0.3 * i)
            w, gain = _prep_even(e_w_in[j], a_qn[j], a_kn[j], b_qn[j], b_kn[j])
            xs, (aq, ak, bq, bk, av, bv) = _inproj(
                "even", xs, prev, norm1[i][None], mod(i, 0), mod(i, 1), w,
                [(gain, "full"), (cos64, "rows"), (sin64, "rows")],
                [(512, bf, True), (512, bf, False), (512, bf, False), (128, bf, False), (512, bf, True),
                 (128, bf, False)], LT)
            lqk = jnp.stack([a_lq1[j], a_lk1[j], a_lq2[j], a_lk2[j]])
            o1 = _diff_attn(aq, ak, av, lqk, a_gn[j][None], L, lam_init)
            o2 = _win_attn(bq, bk, bv, b_sink[j], L)
            xs, ht = _outproj(xs, o1, [o2], e_w_out[j].astype(bf), mod(i, 2), norm2[i][None], mod(i, 3), mod(i, 4),
                              LT, gla=False)
        else:
            w, wq, wkv, qn, kn, wg, bg = _prep_odd(o_w_in[j], c_wq_up[j], c_wkv_up[j], c_qn[j], c_kn[j],
                                                   d_wg_f[j], d_bg_f[j], d_wg_b[j], d_bg_b[j])
            xs, (cq, ck, cvv, dq, dk, dv, lg, r) = _inproj(
                "odd", xs, prev, norm1[i][None], mod(i, 0), mod(i, 1), w,
                [(c_gq[j][None], "full"), (c_gkv[j][None], "full"), (wq, "full"), (wkv, "full"), (qn, "full"),
                 (kn, "full"), (wg, "full"), (bg, "full"), (cos_c, "rows"), (sin_c, "rows")],
                [(1024, bf, True), (1024, bf, False), (512, bf, True), (256, F32, False), (256, F32, False),
                 (512, bf, False), (512, F32, False), (512, F32, False)], LT)
            o1 = _mla_attn(cq, ck, cvv, L)
            of = _gla(dq, dk, dv, lg, 0, L)
            ob = _gla(dq, dk, dv, lg, 1, L)
            xs, ht = _outproj(xs, o1, [of, ob, r, d_gn[j][None]], o_w_out[j].astype(bf), mod(i, 2), norm2[i][None],
                              mod(i, 3), mod(i, 4), LT, gla=True)
        wqt = p_wq[i].T.astype(bf)
        keys = p_keys[i].reshape(2 * P_HEADS, P_NKEYS, P_DKEY // 2).astype(bf)
        r2, e2, cnt, cc = _peer_gate(ht, wqt, keys)
        pp = _peer_dense(ht, r2, e2, cnt, cc, u_all, vt_all, i)
        prev = (pp.reshape(B, L + S, D), mod(i, 5))
    return _final(xs, prev[0], prev[1], L)
```
